```python
import math
import jax
import jax.numpy as jnp
from jax import lax
import numpy as np

D_MODEL = 1024
BATCH = 4
SEQ = 8192
DEPTH = 2

GRID_W = 64
CTX_LEN = 256
EPS = 1e-6
ROPE_BASE = 10000.0
Q_BLOCK = 128
F32 = jnp.float32

ML_HEADS = 4
ML_DH = 64
ML_W = ML_HEADS * ML_DH
ML_CHUNK = 128
ML_COLS = 4 * ML_W + 4 * ML_HEADS

MLA_HEADS = 4
MLA_Q_RANK = 192
MLA_KV_RANK = 128
MLA_NOPE = 64
MLA_ROPE = 32
MLA_V = 64
MLA_W = MLA_HEADS * MLA_V
MLA_COLS = MLA_Q_RANK + MLA_KV_RANK + MLA_ROPE

SW_HEADS = 4
SW_KV_HEADS = 2
SW_DH = 64
SW_WINDOW = 128
SW_BLOCK = 128
SW_W = SW_HEADS * SW_DH
SW_COLS = (SW_HEADS + 2 * SW_KV_HEADS) * SW_DH

LRU_W = 256
LRU_BLOCKS = 4
LRU_BW = LRU_W // LRU_BLOCKS
LRU_CONV = 4
LRU_C = 8.0
LRU_COLS = 2 * LRU_W

D_IN = ML_COLS + MLA_COLS + SW_COLS + LRU_COLS
D_MIX = ML_W + MLA_W + SW_W + LRU_W
COL_SPLITS = (ML_COLS, ML_COLS + MLA_COLS, ML_COLS + MLA_COLS + SW_COLS)

D_FF = 2816
N_EXPERTS = 8
TOP_K = 2
D_FF_EXPERT = 3584

kernel_name = 'hybrid_mlstm_mla_swa_rglru_moe_dit'


def rmsnorm(x, g):
    xf = x.astype(F32)
    y = xf * lax.rsqrt(jnp.mean(xf * xf, axis=-1, keepdims=True) + EPS)
    return (y * g.astype(F32)).astype(x.dtype)


def modulate(h, shift, scale):
    return h * (1.0 + scale) + shift


def _rotate(x, pos):
    nf = x.shape[-1] // 2
    freqs = ROPE_BASE ** (-jnp.arange(nf, dtype=F32) / nf)
    ang = pos.astype(F32)[:, None] * freqs[None, :]
    cos, sin = jnp.cos(ang), jnp.sin(ang)
    x1 = x[..., :nf].astype(F32)
    x2 = x[..., nf:].astype(F32)
    return jnp.concatenate([x1 * cos - x2 * sin, x1 * sin + x2 * cos], axis=-1)


def rope_2d(x, rows, cols):
    half = x.shape[-1] // 2
    y = jnp.concatenate([_rotate(x[..., :half], rows), _rotate(x[..., half:], cols)], axis=-1)
    return y.astype(x.dtype)


def softmax_with_sink(s, sink):
    m = jnp.maximum(jnp.max(s, axis=-1, keepdims=True), sink)
    e = jnp.exp(s - m)
    return e / (jnp.sum(e, axis=-1, keepdims=True) + jnp.exp(sink - m))


def block_attention(q, k, v):
    B, H, T, dq = q.shape
    nb = T // Q_BLOCK
    scale = dq ** -0.5
    qb = jnp.moveaxis(q.reshape(B, H, nb, Q_BLOCK, dq), 2, 0)

    def one_block(qblk):
        s = jnp.einsum('bhqd,bhkd->bhqk', qblk, k).astype(F32) * scale
        p = jax.nn.softmax(s, axis=-1)
        return jnp.einsum('bhqk,bhkd->bhqd', p.astype(v.dtype), v)

    o = lax.map(one_block, qb)
    return jnp.moveaxis(o, 0, 2).reshape(B, H, T, v.shape[-1])


def mlstm_chunked(q, k, v, ig, fg, state, with_out):
    B, H, T, d = k.shape
    nc = T // ML_CHUNK

    def chunks(a):
        return jnp.moveaxis(a.astype(F32).reshape(B, H, nc, ML_CHUNK, *a.shape[3:]), 2, 0)

    xs = (chunks(k), chunks(v), chunks(ig), chunks(jax.nn.log_sigmoid(fg.astype(F32))))
    if with_out:
        xs = xs + (chunks(q),)
    causal = jnp.tril(jnp.ones((ML_CHUNK, ML_CHUNK), dtype=bool))

    def step(carry, blk):
        C, n, m = carry
        kc, vc, ic, lfc = blk[:4]
        b = jnp.cumsum(lfc, axis=-1)
        h = None
        if with_out:
            qc = blk[4]
            d_mat = jnp.where(causal, b[..., :, None] - b[..., None, :] + ic[..., None, :], -jnp.inf)
            m_inter = b + m[..., None]
            m_t = jnp.maximum(m_inter, jnp.max(d_mat, axis=-1))
            w_inter = jnp.exp(m_inter - m_t)
            s = jnp.einsum('bhtd,bhsd->bhts', qc, kc) * jnp.exp(d_mat - m_t[..., None])
            num = jnp.einsum('bhts,bhsd->bhtd', s, vc) + w_inter[..., None] * jnp.einsum('bhvk,bhtk->bhtv', C, qc)
            den = jnp.sum(s, axis=-1) + w_inter * jnp.einsum('bhk,bhtk->bht', n, qc)
            h = num / jnp.maximum(jnp.abs(den), jnp.exp(-m_t))[..., None]
        b_last = b[..., -1]
        g = b_last[..., None] - b + ic
        m_new = jnp.maximum(b_last + m, jnp.max(g, axis=-1))
        w = jnp.exp(g - m_new[..., None])
        decay = jnp.exp(b_last + m - m_new)
        C = decay[..., None, None] * C + jnp.einsum('bhs,bhsv,bhsk->bhvk', w, vc, kc)
        n = decay[..., None] * n + jnp.einsum('bhs,bhsk->bhk', w, kc)
        return (C, n, m_new), h

    state, hs = lax.scan(step, state, xs)
    if with_out:
        hs = jnp.moveaxis(hs, 0, 2).reshape(B, H, T, d)
    return hs, state


def mlstm_mixer(zc, zl, gate_b, out_gain, ctx_out):
    def split_heads(a):
        B, T, _ = a.shape
        return a.reshape(B, T, ML_HEADS, ML_DH).transpose(0, 2, 1, 3)

    def prep(z):
        q, k, v, o, g = jnp.split(z, [ML_W, 2 * ML_W, 3 * ML_W, 4 * ML_W], axis=-1)
        g = (g.astype(F32) + gate_b.astype(F32)).transpose(0, 2, 1)
        gates = jnp.split(g, 4, axis=1)
        return split_heads(q), split_heads(k) * (ML_DH ** -0.5), split_heads(v), o, gates

    def flip(a):
        return jnp.flip(a, axis=2)

    qc, kc, vc, oc, gc = prep(zc)
    ql, kl, vl, ol, gl = prep(zl)
    B = zl.shape[0]
    zero = (jnp.zeros((B, ML_HEADS, ML_DH, ML_DH), F32), jnp.zeros((B, ML_HEADS, ML_DH), F32),
            jnp.zeros((B, ML_HEADS), F32))
    hc_f, st_f = mlstm_chunked(qc if ctx_out else None, kc, vc, gc[0], gc[1], zero, ctx_out)
    hc_b, st_b = mlstm_chunked(flip(qc) if ctx_out else None, flip(kc), flip(vc), flip(gc[2]), flip(gc[3]), zero, ctx_out)
    hl_f, _ = mlstm_chunked(ql, kl, vl, gl[0], gl[1], st_f, True)
    hl_b, _ = mlstm_chunked(flip(ql), flip(kl), flip(vl), flip(gl[2]), flip(gl[3]), st_b, True)
    g_norm = out_gain.reshape(ML_HEADS, 1, ML_DH)

    def finish(h_f, h_b, o):
        B_, H, T, d = h_f.shape
        h = rmsnorm(h_f + flip(h_b), g_norm).transpose(0, 2, 1, 3).reshape(B_, T, ML_W)
        return (jax.nn.sigmoid(o.astype(F32)) * h).astype(zl.dtype)

    yl = finish(hl_f, hl_b, ol)
    yc = finish(hc_f, hc_b, oc) if ctx_out else None
    return yc, yl


def mla_mixer(zc, zl, q_norm, w_uq, kv_norm, w_ukv, q_gain, k_gain, rows, cols, ctx_out):
    def project(z, with_q, rope):
        B, T, _ = z.shape
        cq, ckv, kr = jnp.split(z, [MLA_Q_RANK, MLA_Q_RANK + MLA_KV_RANK], axis=-1)
        kv = (rmsnorm(ckv, kv_norm) @ w_ukv).reshape(B, T, MLA_HEADS, MLA_NOPE + MLA_V).transpose(0, 2, 1, 3)
        k_nope = rmsnorm(kv[..., :MLA_NOPE], k_gain[:MLA_NOPE])
        v = kv[..., MLA_NOPE:]
        k_rope = rmsnorm(kr, k_gain[MLA_NOPE:])[:, None]
        if rope:
            k_rope = rope_2d(k_rope, rows, cols)
        k = jnp.concatenate([k_nope, jnp.broadcast_to(k_rope, (B, MLA_HEADS, T, MLA_ROPE))], axis=-1)
        q = None
        if with_q:
            qf = (rmsnorm(cq, q_norm) @ w_uq).reshape(B, T, MLA_HEADS, MLA_NOPE + MLA_ROPE).transpose(0, 2, 1, 3)
            q_nope = rmsnorm(qf[..., :MLA_NOPE], q_gain[:MLA_NOPE])
            q_rope = rmsnorm(qf[..., MLA_NOPE:], q_gain[MLA_NOPE:])
            if rope:
                q_rope = rope_2d(q_rope, rows, cols)
            q = jnp.concatenate([q_nope, q_rope], axis=-1)
        return q, k, v

    def merge(o):
        return o.transpose(0, 2, 1, 3).reshape(o.shape[0], o.shape[2], MLA_W)

    qc, kc, vc = project(zc, ctx_out, False)
    ql, kl, vl = project(zl, True, True)
    yl = merge(block_attention(ql, jnp.concatenate([kl, kc], axis=2), jnp.concatenate([vl, vc], axis=2)))
    yc = merge(block_attention(qc, kc, vc)) if ctx_out else None
    return yc, yl


def swa_mixer(zc, zl, q_gain, k_gain, sink, rows, cols, ctx_out):
    G = SW_HEADS // SW_KV_HEADS

    def project(z, with_q, rope):
        B, T, _ = z.shape
        q, k, v = jnp.split(z, [SW_W, SW_W + SW_KV_HEADS * SW_DH], axis=-1)
        k = rmsnorm(k.reshape(B, T, SW_KV_HEADS, SW_DH).transpose(0, 2, 1, 3), k_gain)
        v = v.reshape(B, T, SW_KV_HEADS, SW_DH).transpose(0, 2, 1, 3)
        if rope:
            k = rope_2d(k, rows, cols)
        if with_q:
            q = rmsnorm(q.reshape(B, T, SW_KV_HEADS, G, SW_DH).transpose(0, 2, 3, 1, 4), q_gain)
            if rope:
                q = rope_2d(q, rows, cols)
        else:
            q = None
        return q, k, v

    scale = SW_DH ** -0.5
    sink_h = sink.astype(F32).reshape(SW_KV_HEADS, G)
    qc, kc, vc = project(zc, ctx_out, False)
    ql, kl, vl = project(zl, True, True)
    B, T = zl.shape[0], zl.shape[1]
    nb = T // SW_BLOCK
    KL = SW_BLOCK + 2 * SW_WINDOW
    idx = jnp.arange(nb)[:, None] * SW_BLOCK + jnp.arange(KL)[None, :]
    pad = ((0, 0), (0, 0), (SW_WINDOW, SW_WINDOW), (0, 0))
    kb = jnp.pad(kl, pad)[:, :, idx]
    vb = jnp.pad(vl, pad)[:, :, idx]
    qb = ql.reshape(B, SW_KV_HEADS, G, nb, SW_BLOCK, SW_DH)
    s_loc = jnp.einsum('bhgnqd,bhnkd->bhgnqk', qb, kb).astype(F32) * scale
    s_ctx = jnp.einsum('bhgnqd,bhkd->bhgnqk', qb, kc).astype(F32) * scale
    qpos = jnp.arange(nb)[:, None, None] * SW_BLOCK + jnp.arange(SW_BLOCK)[None, :, None]
    kpos = jnp.arange(nb)[:, None, None] * SW_BLOCK + jnp.arange(KL)[None, None, :] - SW_WINDOW
    valid = (kpos >= 0) & (kpos < T) & (jnp.abs(qpos - kpos) <= SW_WINDOW)
    s = jnp.concatenate([jnp.where(valid, s_loc, -jnp.inf), s_ctx], axis=-1)
    p = softmax_with_sink(s, sink_h[:, :, None, None, None]).astype(vl.dtype)
    o = (jnp.einsum('bhgnqk,bhnkd->bhgnqd', p[..., :KL], vb)
         + jnp.einsum('bhgnqk,bhkd->bhgnqd', p[..., KL:], vc))
    yl = o.reshape(B, SW_KV_HEADS, G, T, SW_DH).transpose(0, 3, 1, 2, 4).reshape(B, T, SW_W)
    yc = None
    if ctx_out:
        sc = jnp.einsum('bhgqd,bhkd->bhgqk', qc, kc).astype(F32) * scale
        pc = softmax_with_sink(sc, sink_h[:, :, None, None]).astype(vc.dtype)
        oc = jnp.einsum('bhgqk,bhkd->bhgqd', pc, vc)
        yc = oc.transpose(0, 3, 1, 2, 4).reshape(zc.shape[0], zc.shape[1], SW_W)
    return yc, yl


def conv_centred(u, w, b):
    C = u.shape[-1]
    y = lax.conv_general_dilated(u, w[:, None, :].astype(u.dtype), window_strides=(1,),
                                 padding=[(LRU_CONV // 2, LRU_CONV - 1 - LRU_CONV // 2)],
                                 dimension_numbers=('NWC', 'WIO', 'NWC'), feature_group_count=C)
    return y + b.astype(u.dtype)


def rglru_coeffs(u, wa, ba, wx, bx, lam):
    B, T, _ = u.shape
    uf = u.astype(F32)
    ub = uf.reshape(B, T, LRU_BLOCKS, LRU_BW)
    r = jax.nn.sigmoid(jnp.einsum('btnc,ncd->btnd', ub, wa.astype(F32)).reshape(B, T, LRU_W) + ba)
    i = jax.nn.sigmoid(jnp.einsum('btnc,ncd->btnd', ub, wx.astype(F32)).reshape(B, T, LRU_W) + bx)
    log_a = -LRU_C * r * jax.nn.softplus(-lam.astype(F32))
    a = jnp.exp(log_a)
    b = jnp.sqrt(-jnp.expm1(2.0 * log_a)) * (i * uf)
    return a, b


def linear_scan(a, b, h0):
    def combine(l, r):
        return l[0] * r[0], r[0] * l[1] + r[1]
    a_cum, h = lax.associative_scan(combine, (a, b), axis=1)
    h = h + a_cum * h0[:, None, :]
    return h, h[:, -1]


def rglru_mixer(zc, zl, conv_w, conv_b, wa, ba, wx, bx, lam, ctx_out):
    uc_in, gc = jnp.split(zc, 2, axis=-1)
    ul_in, gl = jnp.split(zl, 2, axis=-1)
    uc = conv_centred(uc_in, conv_w, conv_b)
    ul = conv_centred(ul_in, conv_w, conv_b)
    zero = jnp.zeros((zl.shape[0], LRU_W), F32)

    def flip(a):
        return jnp.flip(a, axis=1)

    def run(d, u_ctx, u_lat):
        hc, hc_last = linear_scan(*rglru_coeffs(u_ctx, wa[d], ba[d], wx[d], bx[d], lam[d]), zero)
        hl, _ = linear_scan(*rglru_coeffs(u_lat, wa[d], ba[d], wx[d], bx[d], lam[d]), hc_last)
        return hc, hl

    hc_f, hl_f = run(0, uc, ul)
    hc_b, hl_b = run(1, flip(uc), flip(ul))

    def finish(h_f, h_b, g):
        return (jax.nn.gelu(g.astype(F32)) * (h_f + flip(h_b))).astype(zl.dtype)

    yl = finish(hl_f, hl_b, gl)
    yc = finish(hc_f, hc_b, gc) if ctx_out else None
    return yc, yl


def swiglu(h, w13, w2):
    a, g = jnp.split(h @ w13, 2, axis=-1)
    return (jax.nn.silu(g) * a) @ w2


def moe_ffn(h, router, router_b, w13, w2):
    B, T, D = h.shape
    hf = h.reshape(B * T, D)
    logits = (hf @ router).astype(F32) + router_b.astype(F32)
    top_val, top_idx = lax.top_k(logits, TOP_K)
    gates = jax.nn.softmax(top_val, axis=-1)
    combine = jnp.einsum('nk,nke->ne', gates, jax.nn.one_hot(top_idx, N_EXPERTS, dtype=F32))
    out = jnp.zeros_like(hf)
    for e in range(N_EXPERTS):
        out = out + combine[:, e:e + 1].astype(hf.dtype) * swiglu(hf, w13[e], w2[e])
    return out.reshape(B, T, D)


def setup_inputs(seed: int = 0) -> dict:
    key = jax.random.key(seed)
    ks = iter(jax.random.split(key, 48))
    D = D_MODEL
    n_dense = (DEPTH + 1) // 2
    n_moe = DEPTH // 2

    def nrm(shape, scale):
        return jax.random.normal(next(ks), shape, F32) * scale

    def gain(shape):
        return 1.0 + nrm(shape, 0.05)

    i_b = nrm((DEPTH, 2, ML_HEADS), 0.1)
    f_b = 3.0 + 3.0 * jax.random.uniform(next(ks), (DEPTH, 2, ML_HEADS), F32)
    ml_gate_b = jnp.stack([i_b[:, 0], f_b[:, 0], i_b[:, 1], f_b[:, 1]], axis=1).reshape(DEPTH, 4 * ML_HEADS)
    u = jax.random.uniform(next(ks), (DEPTH, 2, LRU_W), F32, minval=0.9, maxval=0.999)
    lru_lam = jnp.log(u) - jnp.log1p(-u)
    return {
        'x': nrm((BATCH, SEQ, D), 1.0),
        'c': nrm((BATCH, D), 1.0),
        'ctx': nrm((BATCH, CTX_LEN, D), 1.0),
        'c_ctx': nrm((D,), 1.0),
        'ada_w': nrm((DEPTH, D, 6 * D), 0.5 * D ** -0.5),
        'ada_b': nrm((DEPTH, 6 * D), 0.02),
        'norm_mix': gain((DEPTH, D)),
        'norm_ffn': gain((DEPTH, D)),
        'w_in': nrm((DEPTH, D, D_IN), D ** -0.5),
        'w_out': nrm((DEPTH, D_MIX, D), D_MIX ** -0.5),
        'ml_gate_b': ml_gate_b,
        'ml_out_norm': gain((DEPTH, ML_W)),
        'mla_q_norm': gain((DEPTH, MLA_Q_RANK)),
        'mla_w_uq': nrm((DEPTH, MLA_Q_RANK, MLA_HEADS * (MLA_NOPE + MLA_ROPE)), MLA_Q_RANK ** -0.5),
        'mla_kv_norm': gain((DEPTH, MLA_KV_RANK)),
        'mla_w_ukv': nrm((DEPTH, MLA_KV_RANK, MLA_HEADS * (MLA_NOPE + MLA_V)), MLA_KV_RANK ** -0.5),
        'mla_q_gain': gain((DEPTH, MLA_NOPE + MLA_ROPE)),
        'mla_k_gain': gain((DEPTH, MLA_NOPE + MLA_ROPE)),
        'sw_q_gain': gain((DEPTH, SW_DH)),
        'sw_k_gain': gain((DEPTH, SW_DH)),
        'sw_sink': nrm((DEPTH, SW_HEADS), 0.5),
        'lru_conv_w': nrm((DEPTH, LRU_CONV, LRU_W), 0.5),
        'lru_conv_b': nrm((DEPTH, LRU_W), 0.02),
        'lru_wa': nrm((DEPTH, 2, LRU_BLOCKS, LRU_BW, LRU_BW), LRU_BW ** -0.5),
        'lru_ba': nrm((DEPTH, 2, LRU_W), 0.02),
        'lru_wx': nrm((DEPTH, 2, LRU_BLOCKS, LRU_BW, LRU_BW), LRU_BW ** -0.5),
        'lru_bx': nrm((DEPTH, 2, LRU_W), 0.02),
        'lru_lam': lru_lam,
        'ffn_w13': nrm((n_dense, D, 2 * D_FF), D ** -0.5),
        'ffn_w2': nrm((n_dense, D_FF, D), D_FF ** -0.5),
        'moe_router': nrm((n_moe, D, N_EXPERTS), D ** -0.5),
        'moe_router_b': nrm((n_moe, N_EXPERTS), 0.01),
        'moe_w13': nrm((n_moe, N_EXPERTS, D, 2 * D_FF_EXPERT), D ** -0.5),
        'moe_w2': nrm((n_moe, N_EXPERTS, D_FF_EXPERT, D), D_FF_EXPERT ** -0.5),
    }


def reference(x, c, ctx, c_ctx, ada_w, ada_b, norm_mix, norm_ffn, w_in, w_out, ml_gate_b, ml_out_norm,
              mla_q_norm, mla_w_uq, mla_kv_norm, mla_w_ukv, mla_q_gain, mla_k_gain, sw_q_gain, sw_k_gain,
              sw_sink, lru_conv_w, lru_conv_b, lru_wa, lru_ba, lru_wx, lru_bx, lru_lam, ffn_w13, ffn_w2,
              moe_router, moe_router_b, moe_w13, moe_w2):
    T = x.shape[1]
    ROWS = T // GRID_W
    rows = jnp.repeat(jnp.arange(ROWS, dtype=jnp.int32), GRID_W)
    cols = jnp.tile(jnp.arange(GRID_W, dtype=jnp.int32), ROWS)
    xl, xc = x, ctx
    for l in range(DEPTH):
        ctx_out = l < DEPTH - 1
        mod_l = (jax.nn.silu(c) @ ada_w[l] + ada_b[l])[:, None, :]
        mod_c = jax.nn.silu(c_ctx) @ ada_w[l] + ada_b[l]
        sh1, sc1, g1, sh2, sc2, g2 = jnp.split(mod_l, 6, axis=-1)
        csh1, csc1, cg1, csh2, csc2, cg2 = jnp.split(mod_c, 6, axis=-1)

        hl = modulate(rmsnorm(xl, norm_mix[l]), sh1, sc1)
        hc = modulate(rmsnorm(xc, norm_mix[l]), csh1, csc1)
        za, zb, zc_, zd = jnp.split(hl @ w_in[l], COL_SPLITS, axis=-1)
        ca, cb, cc, cd = jnp.split(hc @ w_in[l], COL_SPLITS, axis=-1)
        ya_c, ya_l = mlstm_mixer(ca, za, ml_gate_b[l], ml_out_norm[l], ctx_out)
        yb_c, yb_l = mla_mixer(cb, zb, mla_q_norm[l], mla_w_uq[l], mla_kv_norm[l], mla_w_ukv[l],
                               mla_q_gain[l], mla_k_gain[l], rows, cols, ctx_out)
        yc_c, yc_l = swa_mixer(cc, zc_, sw_q_gain[l], sw_k_gain[l], sw_sink[l], rows, cols, ctx_out)
        yd_c, yd_l = rglru_mixer(cd, zd, lru_conv_w[l], lru_conv_b[l], lru_wa[l], lru_ba[l], lru_wx[l],
                                 lru_bx[l], lru_lam[l], ctx_out)
        yl = jnp.concatenate([ya_l, yb_l, yc_l, yd_l], axis=-1) @ w_out[l]
        xl = xl + g1 * yl

        hl = modulate(rmsnorm(xl, norm_ffn[l]), sh2, sc2)
        if l % 2 == 0:
            xl = xl + g2 * swiglu(hl, ffn_w13[l // 2], ffn_w2[l // 2])
        else:
            xl = xl + g2 * moe_ffn(hl, moe_router[l // 2], moe_router_b[l // 2], moe_w13[l // 2], moe_w2[l // 2])

        if ctx_out:
            yc = jnp.concatenate([ya_c, yb_c, yc_c, yd_c], axis=-1) @ w_out[l]
            xc = xc + cg1 * yc
            hc = modulate(rmsnorm(xc, norm_ffn[l]), csh2, csc2)
            if l % 2 == 0:
                xc = xc + cg2 * swiglu(hc, ffn_w13[l // 2], ffn_w2[l // 2])
            else:
                xc = xc + cg2 * moe_ffn(hc, moe_router[l // 2], moe_router_b[l // 2], moe_w13[l // 2], moe_w2[l // 2])
    return xl
```

```python
import functools

import numpy as np
import jax
import jax.numpy as jnp
from jax import lax
from jax.experimental import pallas as pl
from jax.experimental.pallas import tpu as pltpu

F32 = jnp.float32
BF16 = jnp.bfloat16
HI = lax.Precision.HIGHEST

D = 1024
GRID_W = 64
EPS = 1e-6
ROPE_BASE = 10000.0
ML_H, ML_DH, ML_W, ML_CHUNK = 4, 64, 256, 128
MLA_H, MLA_QR, MLA_KVR, MLA_NOPE, MLA_ROPE, MLA_V = 4, 192, 128, 64, 32, 64
SW_H, SW_KV, SW_DH, SW_WINDOW, SW_BLOCK = 4, 2, 64, 128, 128
LRU_W, LRU_BLOCKS, LRU_C = 256, 4, 8.0
N_EXP = 8

LANE = 128
SUB = 8
TM = 256
NEG = -1e30
VMEM_LIMIT = 56 * 1024 * 1024


def _cp(sem, vmem=None):
    return pltpu.CompilerParams(dimension_semantics=sem, vmem_limit_bytes=vmem)


def _mod_spec(chunk, nb, ctx_tiles, off):
    return pl.BlockSpec((None, 1, D), lambda b, j: (jnp.where(j + off < ctx_tiles, nb, b), 0, chunk))


def _rows(width, off=0, col=0, tm=TM):
    return pl.BlockSpec((None, tm, width), lambda b, j: (b, j + off, col))


def _whole(shape):
    nd = len(shape)
    return pl.BlockSpec(shape, lambda *_: (0,) * nd)


def _rms(x):
    return x * lax.rsqrt(jnp.mean(x * x, axis=-1, keepdims=True) + EPS)


def _lane(shape):
    return lax.broadcasted_iota(jnp.int32, shape, len(shape) - 1)


def _group_rms(blk, bounds):
    lane = _lane(blk.shape)
    sq = blk * blk
    scale = jnp.zeros_like(blk)
    for lo, hi in bounds:
        msk = (lane >= lo) & (lane < hi)
        ms = jnp.sum(jnp.where(msk, sq, 0.0), axis=-1, keepdims=True) * (1.0 / (hi - lo))
        scale = jnp.where(msk, lax.rsqrt(ms + EPS), scale)
    return blk * scale


def _rope(blk, tab_ref, shift):
    n = blk.shape[-1]
    return (blk * tab_ref[0] + pltpu.roll(blk, n - shift, 1) * tab_ref[1]
            + pltpu.roll(blk, shift, 1) * tab_ref[2])


def _rope_tables(t_len, lc, width, dims, offset):
    half, nf = dims // 2, dims // 4
    p = np.arange(dims)
    i = p % half
    f = i % nf
    first = i < nf
    freq = ROPE_BASE ** (-f.astype(np.float64) / nf)
    t = np.arange(t_len)
    pos = np.where(p[None, :] < half, (t // GRID_W)[:, None], (t % GRID_W)[:, None]).astype(np.float64)
    ang = pos * freq[None, :]
    cos, sin = np.cos(ang), np.sin(ang)
    tab = np.zeros((3, lc + t_len, width), np.float64)
    tab[0] = 1.0
    tab[0, lc:, offset:offset + dims] = cos
    tab[1, lc:, offset:offset + dims] = np.where(first[None, :], -sin, 0.0)
    tab[2, lc:, offset:offset + dims] = np.where(first[None, :], 0.0, sin)
    reps = LANE // width
    return jnp.asarray(np.tile(tab, (1, 1, reps)), F32)


def _ada_kernel(c_ref, w_ref, b_ref, o_ref):
    c = c_ref[...]
    s = c * jax.nn.sigmoid(c)
    o_ref[...] = jnp.dot(s, w_ref[...], precision=HI, preferred_element_type=F32) + b_ref[...]


def _ada_mod(cc, ada_w, ada_b):
    depth = ada_w.shape[0]
    n = ada_w.shape[2]
    tn = 1024
    return pl.pallas_call(
        _ada_kernel,
        grid=(depth, n // tn),
        in_specs=[
            pl.BlockSpec((SUB, D), lambda l, j: (0, 0)),
            pl.BlockSpec((None, D, tn), lambda l, j: (l, 0, j)),
            pl.BlockSpec((None, 1, tn), lambda l, j: (l, 0, j)),
        ],
        out_specs=pl.BlockSpec((None, SUB, tn), lambda l, j: (l, 0, j)),
        out_shape=jax.ShapeDtypeStruct((depth, SUB, n), F32),
        compiler_params=_cp(("parallel", "parallel")),
        name="ada_mod",
    )(cc, ada_w, ada_b.reshape(depth, 1, n))


def _proj_kernel(x_ref, g_ref, sh_ref, sc_ref, w_ref, *out_refs, segs):
    h = _rms(x_ref[...]) * g_ref[...]
    h = h * (1.0 + sc_ref[...]) + sh_ref[...]
    hb = h.astype(BF16)
    for (start, width), o_ref in zip(segs, out_refs):
        o_ref[...] = jnp.dot(hb, w_ref[:, start:start + width], preferred_element_type=F32).astype(o_ref.dtype)


def _proj(xa, gain, mod, w, segs, dtypes, nb, ctx_tiles):
    b, s, _ = xa.shape
    n = w.shape[1]
    return pl.pallas_call(
        functools.partial(_proj_kernel, segs=segs),
        grid=(b, s // TM),
        in_specs=[
            _rows(D),
            _whole((1, D)),
            _mod_spec(0, nb, ctx_tiles, 0),
            _mod_spec(1, nb, ctx_tiles, 0),
            _whole((D, n)),
        ],
        out_specs=[_rows(wd) for _, wd in segs],
        out_shape=[jax.ShapeDtypeStruct((b, s, wd), dt) for (_, wd), dt in zip(segs, dtypes)],
        compiler_params=_cp(("parallel", "parallel"), VMEM_LIMIT),
        name="in_proj",
    )(xa, gain.reshape(1, D), mod, mod, w)


def _prep_kernel(zb_ref, zc_ref, tm_ref, ts_ref, qn_ref, wuq_ref, kvn_ref, wk_ref, wv_ref, qg_ref, kg_ref,
                 sqg_ref, skg_ref, qm_ref, km_ref, vm_ref, qs_ref, ks_ref, vs_ref):
    zb = zb_ref[...]
    cq = zb[:, :2 * LANE]
    cqn = cq * lax.rsqrt(jnp.sum(cq * cq, axis=-1, keepdims=True) * (1.0 / MLA_QR) + EPS) * qn_ref[...]
    qf = jnp.dot(cqn.astype(BF16), wuq_ref[...], preferred_element_type=F32)
    q_scale = (MLA_NOPE + MLA_ROPE) ** -0.5
    head_bounds = [(0, MLA_NOPE), (MLA_NOPE, MLA_NOPE + MLA_ROPE)]
    for h in range(MLA_H):
        blk = _group_rms(qf[:, h * LANE:(h + 1) * LANE], head_bounds) * qg_ref[...]
        qm_ref[h] = (_rope(blk, tm_ref, MLA_ROPE // 4) * q_scale).astype(BF16)
    ckvn = _rms(zb[:, 2 * LANE:3 * LANE]) * kvn_ref[...]
    ckvb = ckvn.astype(BF16)
    kf = jnp.dot(ckvb, wk_ref[...], preferred_element_type=F32)
    vf = jnp.dot(ckvb, wv_ref[...], preferred_element_type=F32)
    kg = kg_ref[...]
    kr = _group_rms(zb[:, 3 * LANE:4 * LANE], head_bounds[1:]) * kg
    kr = _rope(kr, tm_ref, MLA_ROPE // 4)
    for h in range(MLA_H):
        kn = _group_rms(kf[:, h * LANE:(h + 1) * LANE], head_bounds[:1]) * kg
        km_ref[h] = (kn + kr).astype(BF16)
        vm_ref[h] = vf[:, h * MLA_V:(h + 1) * MLA_V].astype(BF16)
    zc = zc_ref[...]
    pair = [(0, SW_DH), (SW_DH, 2 * SW_DH)]
    sw_scale = SW_DH ** -0.5
    for half in range(2):
        blk = _group_rms(zc[:, half * LANE:(half + 1) * LANE], pair) * sqg_ref[...]
        qs_ref[:, half * LANE:(half + 1) * LANE] = (_rope(blk, ts_ref, SW_DH // 4) * sw_scale).astype(BF16)
    kb = _rope(_group_rms(zc[:, 2 * LANE:3 * LANE], pair) * skg_ref[...], ts_ref, SW_DH // 4).astype(BF16)
    vb = zc[:, 3 * LANE:4 * LANE].astype(BF16)
    for j in range(SW_KV):
        ks_ref[j] = kb[:, j * SW_DH:(j + 1) * SW_DH]
        vs_ref[j] = vb[:, j * SW_DH:(j + 1) * SW_DH]


def _prep(zb, zc, tab_mla, tab_sw, p):
    b, s, _ = zb.shape
    tab = pl.BlockSpec((3, TM, LANE), lambda bb, j: (0, j, 0))
    heads = lambda nh, w: pl.BlockSpec((None, nh, TM, w), lambda bb, j: (bb, 0, j, 0))
    return pl.pallas_call(
        _prep_kernel,
        grid=(b, s // TM),
        in_specs=[_rows(4 * LANE), _rows(4 * LANE), tab, tab,
                  _whole((1, 2 * LANE)), _whole((2 * LANE, 4 * LANE)), _whole((1, LANE)),
                  _whole((LANE, 4 * LANE)), _whole((LANE, 2 * LANE)), _whole((1, LANE)), _whole((1, LANE)),
                  _whole((1, LANE)), _whole((1, LANE))],
        out_specs=[heads(MLA_H, LANE), heads(MLA_H, LANE), heads(MLA_H, MLA_V),
                   _rows(2 * LANE), heads(SW_KV, SW_DH), heads(SW_KV, SW_DH)],
        out_shape=[jax.ShapeDtypeStruct((b, MLA_H, s, LANE), BF16),
                   jax.ShapeDtypeStruct((b, MLA_H, s, LANE), BF16),
                   jax.ShapeDtypeStruct((b, MLA_H, s, MLA_V), BF16),
                   jax.ShapeDtypeStruct((b, s, 2 * LANE), BF16),
                   jax.ShapeDtypeStruct((b, SW_KV, s, SW_DH), BF16),
                   jax.ShapeDtypeStruct((b, SW_KV, s, SW_DH), BF16)],
        compiler_params=_cp(("parallel", "parallel"), VMEM_LIMIT),
        name="head_prep",
    )(zb, zc, tab_mla, tab_sw, p["q_norm"], p["w_uq"], p["kv_norm"], p["w_uk"], p["w_uv"], p["q_gain"],
      p["k_gain"], p["sw_q_gain"], p["sw_k_gain"])


MLA_TQ = 256
MLA_KC = 256


def _mla_kernel(q_ref, k_ref, v_ref, o_ref, *, n_all, n_ctx, ctx_tiles):
    i = pl.program_id(2)
    q = q_ref[...]
    n_kv = jnp.where(i < ctx_tiles, n_ctx, n_all)

    def body(c, carry):
        m, l, acc = carry
        off = pl.multiple_of(c * MLA_KC, MLA_KC)
        k = k_ref[pl.ds(off, MLA_KC), :]
        v = v_ref[pl.ds(off, MLA_KC), :]
        s = lax.dot_general(q, k, (((1,), (1,)), ((), ())), preferred_element_type=F32)
        m_new = jnp.maximum(m, jnp.max(s, axis=-1, keepdims=True))
        p = jnp.exp(s - m_new)
        alpha = jnp.exp(m - m_new)
        l = alpha * l + jnp.sum(p, axis=-1, keepdims=True)
        acc = alpha * acc + jnp.dot(p.astype(BF16), v, preferred_element_type=F32)
        return m_new, l, acc

    init = (jnp.full((MLA_TQ, 1), NEG, F32), jnp.zeros((MLA_TQ, 1), F32), jnp.zeros((MLA_TQ, MLA_V), F32))
    _, l, acc = lax.fori_loop(0, n_kv, body, init)
    o_ref[...] = (acc / l).astype(o_ref.dtype)


def _mla_attn(q, k, v, lc):
    b, h, s, _ = q.shape
    kern = functools.partial(_mla_kernel, n_all=s // MLA_KC, n_ctx=lc // MLA_KC, ctx_tiles=lc // MLA_TQ)
    return pl.pallas_call(
        kern,
        grid=(b, h, s // MLA_TQ),
        in_specs=[pl.BlockSpec((None, None, MLA_TQ, LANE), lambda bb, hh, i: (bb, hh, i, 0)),
                  pl.BlockSpec((None, None, s, LANE), lambda bb, hh, i: (bb, hh, 0, 0)),
                  pl.BlockSpec((None, None, s, MLA_V), lambda bb, hh, i: (bb, hh, 0, 0))],
        out_specs=pl.BlockSpec((None, None, MLA_TQ, MLA_V), lambda bb, hh, i: (bb, hh, i, 0)),
        out_shape=jax.ShapeDtypeStruct((b, h, s, MLA_V), BF16),
        compiler_params=_cp(("parallel", "parallel", "arbitrary"), VMEM_LIMIT),
        name="mla_attn",
    )(q, k, v)


def _swa_kernel(sink_ref, q_ref, k_ref, v_ref, o_ref, *, lc, t_len):
    j = pl.program_id(1)
    n = pl.program_id(2)
    blk = SW_BLOCK
    q2 = q_ref[...]
    qs = jnp.concatenate([q2[:, :SW_DH], q2[:, SW_DH:]], axis=0)
    start = pl.multiple_of(jnp.maximum(n - 1, 0) * blk, blk)
    nw = blk + 2 * SW_WINDOW
    kw = k_ref[pl.ds(start, nw), :]
    vw = v_ref[pl.ds(start, nw), :]
    kc = k_ref[pl.ds(0, lc), :]
    vc = v_ref[pl.ds(0, lc), :]
    dn = (((1,), (1,)), ((), ()))
    s_loc = lax.dot_general(qs, kw, dn, preferred_element_type=F32)
    s_ctx = lax.dot_general(qs, kc, dn, preferred_element_type=F32)
    row = lax.broadcasted_iota(jnp.int32, s_loc.shape, 0)
    col = lax.broadcasted_iota(jnp.int32, s_loc.shape, 1)
    qpos = n * blk - lc + (row & (blk - 1))
    kpos = start - lc + col
    valid = (n * blk >= lc) & (kpos >= 0) & (kpos < t_len) & (jnp.abs(qpos - kpos) <= SW_WINDOW)
    s_loc = jnp.where(valid, s_loc, NEG)
    rows1 = lax.broadcasted_iota(jnp.int32, (2 * blk, 1), 0)
    sink = jnp.where(rows1 < blk, sink_ref[2 * j], sink_ref[2 * j + 1])
    m = jnp.maximum(jnp.maximum(jnp.max(s_loc, axis=-1, keepdims=True), jnp.max(s_ctx, axis=-1, keepdims=True)), sink)
    e_loc = jnp.exp(s_loc - m)
    e_ctx = jnp.exp(s_ctx - m)
    den = jnp.sum(e_loc, axis=-1, keepdims=True) + jnp.sum(e_ctx, axis=-1, keepdims=True) + jnp.exp(sink - m)
    o = (jnp.dot(e_loc.astype(BF16), vw, preferred_element_type=F32)
         + jnp.dot(e_ctx.astype(BF16), vc, preferred_element_type=F32)) / den
    o_ref[...] = jnp.concatenate([o[:blk], o[blk:]], axis=-1).astype(o_ref.dtype)


def _swa_attn(q, k, v, sink, lc):
    b, s, _ = q.shape
    t_len = s - lc
    sp = k.shape[2]
    kern = functools.partial(_swa_kernel, lc=lc, t_len=t_len)
    return pl.pallas_call(
        kern,
        grid=(b, SW_KV, s // SW_BLOCK),
        in_specs=[pl.BlockSpec(memory_space=pltpu.SMEM),
                  pl.BlockSpec((None, SW_BLOCK, LANE), lambda bb, j, n: (bb, n, j)),
                  pl.BlockSpec((None, None, sp, SW_DH), lambda bb, j, n: (bb, j, 0, 0)),
                  pl.BlockSpec((None, None, sp, SW_DH), lambda bb, j, n: (bb, j, 0, 0))],
        out_specs=pl.BlockSpec((None, SW_BLOCK, LANE), lambda bb, j, n: (bb, n, j)),
        out_shape=jax.ShapeDtypeStruct((b, s, 2 * LANE), BF16),
        compiler_params=_cp(("parallel", "parallel", "arbitrary"), VMEM_LIMIT),
        name="swa_attn",
    )(sink, q, k, v)


def _mirror(j, n_ctx, n_all):
    return jnp.where(j < n_ctx, n_ctx - 1 - j, n_all + n_ctx - 1 - j)


def _mlstm_kernel(xf_ref, xb_ref, gf_ref, gb_ref, bias_ref, hf_ref, hb_ref, c_ref, n_ref, m_ref):
    L = ML_CHUNK

    @pl.when(pl.program_id(1) == 0)
    def _():
        c_ref[...] = jnp.zeros_like(c_ref)
        n_ref[...] = jnp.zeros_like(n_ref)
        m_ref[...] = jnp.zeros_like(m_ref)

    row = lax.broadcasted_iota(jnp.int32, (L, L), 0)
    col = lax.broadcasted_iota(jnp.int32, (L, L), 1)
    dn_t = (((1,), (1,)), ((), ()))
    for direction, (x_ref, g_ref, o_ref) in enumerate(((xf_ref, gf_ref, hf_ref), (xb_ref, gb_ref, hb_ref))):
        causal = (col <= row) if direction == 0 else (col >= row)
        tri = causal.astype(F32)
        x = x_ref[...]
        g = g_ref[...] + bias_ref[...]
        gt = g.T
        lf = jax.nn.log_sigmoid(g)
        lft = jax.nn.log_sigmoid(gt)
        bcol = jnp.dot(tri, lf, precision=HI, preferred_element_type=F32)
        brow = lax.dot_general(lft, tri, dn_t, precision=HI, preferred_element_type=F32)
        last = L - 1 if direction == 0 else 0
        outs = []
        for h in range(ML_H):
            chain = direction * ML_H + h
            gi, gf = 2 * ML_H * direction + h, 2 * ML_H * direction + ML_H + h
            ic_col, b_col = g[:, gi:gi + 1], bcol[:, gf:gf + 1]
            ic_row, b_row = gt[gi:gi + 1, :], brow[gf:gf + 1, :]
            q = x[:, h * ML_DH:(h + 1) * ML_DH]
            k = x[:, ML_W + h * ML_DH:ML_W + (h + 1) * ML_DH] * (ML_DH ** -0.5)
            v = x[:, 2 * ML_W + h * ML_DH:2 * ML_W + (h + 1) * ML_DH]
            c_st, n_st, m_st = c_ref[chain], n_ref[chain], m_ref[chain]
            d = jnp.where(causal, b_col + (ic_row - b_row), NEG)
            m_inter = b_col + m_st
            m_t = jnp.maximum(m_inter, jnp.max(d, axis=-1, keepdims=True))
            w_inter = jnp.exp(m_inter - m_t)
            s = lax.dot_general(q, k, dn_t, preferred_element_type=F32) * jnp.exp(d - m_t)
            num = (jnp.dot(s.astype(BF16), v, preferred_element_type=F32)
                   + w_inter * lax.dot_general(q, c_st.astype(BF16), dn_t, preferred_element_type=F32))
            den = (jnp.sum(s, axis=-1, keepdims=True)
                   + w_inter * jnp.sum(q.astype(F32) * n_st, axis=-1, keepdims=True))
            outs.append(num / jnp.maximum(jnp.abs(den), jnp.exp(-m_t)))
            b_last = b_col[last:last + 1, :]
            g_col = b_last - b_col + ic_col
            m_new = jnp.maximum(b_last + m_st, jnp.max(g_col, axis=0, keepdims=True))
            w_col = jnp.exp(g_col - m_new)
            decay = jnp.exp(b_last + m_st - m_new)
            vw = (v.astype(F32) * w_col).astype(BF16)
            c_ref[chain] = decay * c_st + lax.dot_general(vw, k, (((0,), (0,)), ((), ())), preferred_element_type=F32)
            n_ref[chain] = decay * n_st + jnp.sum(k.astype(F32) * w_col, axis=0, keepdims=True)
            m_ref[chain] = m_new
        o_ref[...] = jnp.concatenate(outs, axis=-1)


def _mlstm(za, zg, bias, lc):
    b, s, _ = za.shape
    n_all, n_ctx = s // ML_CHUNK, lc // ML_CHUNK
    fwd = lambda w: pl.BlockSpec((None, ML_CHUNK, w), lambda bb, j: (bb, j, 0))
    bwd = lambda w: pl.BlockSpec((None, ML_CHUNK, w), lambda bb, j: (bb, _mirror(j, n_ctx, n_all), 0))
    return pl.pallas_call(
        _mlstm_kernel,
        grid=(b, n_all),
        in_specs=[fwd(4 * ML_W), bwd(4 * ML_W), fwd(LANE), bwd(LANE), _whole((1, LANE))],
        out_specs=[fwd(ML_W), bwd(ML_W)],
        out_shape=[jax.ShapeDtypeStruct((b, s, ML_W), F32)] * 2,
        scratch_shapes=[pltpu.VMEM((2 * ML_H, ML_DH, ML_DH), F32),
                        pltpu.VMEM((2 * ML_H, 1, ML_DH), F32),
                        pltpu.VMEM((2 * ML_H, 1, 1), F32)],
        compiler_params=_cp(("parallel", "arbitrary"), VMEM_LIMIT),
        name="mlstm_scan",
    )(za, za, zg, zg, bias)


LRU_T = 256
LRU_HALO = SUB


def _lru_kernel(uf_ref, pf_ref, nf_ref, ub_ref, pb_ref, nb_ref, cw_ref, cb_ref, wa_ref, ba_ref, wx_ref, bx_ref,
                lam_ref, hf_ref, hb_ref, carry_ref, *, n_ctx, n_all):
    T = LRU_T
    j = pl.program_id(1)

    @pl.when(j == 0)
    def _():
        carry_ref[...] = jnp.zeros_like(carry_ref)

    cw = cw_ref[...]
    row = lax.broadcasted_iota(jnp.int32, (T, LRU_W), 0)
    for direction, (u_ref, p_ref, n_ref, o_ref) in enumerate(((uf_ref, pf_ref, nf_ref, hf_ref),
                                                              (ub_ref, pb_ref, nb_ref, hb_ref))):
        c = j if direction == 0 else _mirror(j, n_ctx, n_all)
        has_prev = ((c != 0) & (c != n_ctx)).astype(F32)
        has_next = ((c != n_ctx - 1) & (c != n_all - 1)).astype(F32)
        ext = jnp.concatenate([p_ref[...] * has_prev, u_ref[...], n_ref[...] * has_next], axis=0)
        n_ext = T + 2 * LRU_HALO
        u = cb_ref[...] + cw[2:3] * ext[LRU_HALO:LRU_HALO + T]
        for tap, sh in ((0, 2), (1, 1), (3, n_ext - 1)):
            u = u + cw[tap:tap + 1] * pltpu.roll(ext, sh, 0)[LRU_HALO:LRU_HALO + T]
        ub = u.astype(BF16)
        r = jax.nn.sigmoid(jnp.dot(ub, wa_ref[direction], preferred_element_type=F32) + ba_ref[direction])
        i = jax.nn.sigmoid(jnp.dot(ub, wx_ref[direction], preferred_element_type=F32) + bx_ref[direction])
        lam = lam_ref[direction]
        log_a = (-LRU_C) * r * jnp.log1p(jnp.exp(-lam))
        a = jnp.exp(log_a)
        bb = jnp.sqrt(1.0 - a * a) * (i * u)
        sh = 1
        while sh < T:
            if direction == 0:
                ok = row >= sh
                a_s, b_s = pltpu.roll(a, sh, 0), pltpu.roll(bb, sh, 0)
            else:
                ok = row < T - sh
                a_s, b_s = pltpu.roll(a, T - sh, 0), pltpu.roll(bb, T - sh, 0)
            bb = jnp.where(ok, a * b_s + bb, bb)
            a = jnp.where(ok, a * a_s, a)
            sh *= 2
        hcur = bb + a * carry_ref[direction]
        o_ref[...] = hcur
        last = T - 1 if direction == 0 else 0
        carry_ref[direction] = hcur[last:last + 1, :]


def _lru(zd, p, lc):
    b, s, _ = zd.shape
    n_all, n_ctx = s // LRU_T, lc // LRU_T
    per = LRU_T // LRU_HALO
    n_halo = s // LRU_HALO
    ident = lambda j: j
    mirr = lambda j: _mirror(j, n_ctx, n_all)

    def specs(cf):
        return [pl.BlockSpec((None, LRU_T, LRU_W), lambda bb, j: (bb, cf(j), 0)),
                pl.BlockSpec((None, LRU_HALO, LRU_W), lambda bb, j: (bb, jnp.maximum(cf(j) * per - 1, 0), 0)),
                pl.BlockSpec((None, LRU_HALO, LRU_W), lambda bb, j: (bb, jnp.minimum((cf(j) + 1) * per, n_halo - 1), 0))]

    vec = _whole((2, 1, LRU_W))
    mat = _whole((2, LRU_W, LRU_W))
    return pl.pallas_call(
        functools.partial(_lru_kernel, n_ctx=n_ctx, n_all=n_all),
        grid=(b, n_all),
        in_specs=specs(ident) + specs(mirr) + [_whole((4, LRU_W)), _whole((1, LRU_W)), mat, vec, mat, vec, vec],
        out_specs=[pl.BlockSpec((None, LRU_T, LRU_W), lambda bb, j: (bb, j, 0)),
                   pl.BlockSpec((None, LRU_T, LRU_W), lambda bb, j: (bb, mirr(j), 0))],
        out_shape=[jax.ShapeDtypeStruct((b, s, LRU_W), F32)] * 2,
        scratch_shapes=[pltpu.VMEM((2, 1, LRU_W), F32)],
        compiler_params=_cp(("parallel", "arbitrary"), VMEM_LIMIT),
        name="rglru_scan",
    )(zd, zd, zd, zd, zd, zd, p["conv_w"], p["conv_b"], p["wa"], p["ba"], p["wx"], p["bx"], p["lam"])


def _out_kernel(x_ref, gate_ref, hfa_ref, hba_ref, o_ref, gn_ref, yb_ref, yc_ref, hfd_ref, hbd_ref, gd_ref, w_ref, out_ref):
    pair = [(0, ML_DH), (ML_DH, 2 * ML_DH)]
    ha = hfa_ref[...] + hba_ref[...]
    ha = jnp.concatenate([_group_rms(ha[:, :LANE], pair), _group_rms(ha[:, LANE:], pair)], axis=-1) * gn_ref[...]
    ya = (jax.nn.sigmoid(o_ref[...].astype(F32)) * ha).astype(BF16)
    yd = (jax.nn.gelu(gd_ref[...]) * (hfd_ref[...] + hbd_ref[...])).astype(BF16)
    acc = jnp.dot(ya, w_ref[0:ML_W, :], preferred_element_type=F32)
    for h in range(MLA_H):
        lo = ML_W + h * MLA_V
        acc = acc + jnp.dot(yb_ref[h], w_ref[lo:lo + MLA_V, :], preferred_element_type=F32)
    acc = acc + jnp.dot(yc_ref[...], w_ref[2 * ML_W:3 * ML_W, :], preferred_element_type=F32)
    acc = acc + jnp.dot(yd, w_ref[3 * ML_W:4 * ML_W, :], preferred_element_type=F32)
    out_ref[...] = x_ref[...] + gate_ref[...] * acc


def _out_proj(xa, mod, hfa, hba, za, gn, yb, yc, hfd, hbd, zd, w, nb, ctx_tiles, off):
    b, s, _ = xa.shape
    nt = s // TM - off
    r = lambda w_, col=0: _rows(w_, off, col)
    return pl.pallas_call(
        _out_kernel,
        grid=(b, nt),
        in_specs=[r(D), _mod_spec(2, nb, ctx_tiles, off), r(ML_W), r(ML_W), r(ML_W, 3), _whole((1, ML_W)),
                  pl.BlockSpec((None, MLA_H, TM, MLA_V), lambda bb, j: (bb, 0, j + off, 0)),
                  r(ML_W), r(ML_W), r(ML_W), r(ML_W, 1), _whole((D, D))],
        out_specs=_rows(D),
        out_shape=jax.ShapeDtypeStruct((b, nt * TM, D), F32),
        compiler_params=_cp(("parallel", "parallel"), VMEM_LIMIT),
        name="out_proj",
    )(xa, mod, hfa, hba, za, gn, yb, yc, hfd, hbd, zd, w)


def _ffn_kernel(x_ref, g_ref, sh_ref, sc_ref, gate_ref, w13_ref, w2_ref, out_ref, *, d_ff):
    x = x_ref[...]
    h = _rms(x) * g_ref[...]
    hb = (h * (1.0 + sc_ref[...]) + sh_ref[...]).astype(BF16)
    a = jnp.dot(hb, w13_ref[:, :d_ff], preferred_element_type=F32)
    g = jnp.dot(hb, w13_ref[:, d_ff:], preferred_element_type=F32)
    u = (g * jax.nn.sigmoid(g) * a).astype(BF16)
    out_ref[...] = x + gate_ref[...] * jnp.dot(u, w2_ref[...], preferred_element_type=F32)


def _ffn(xa, gain, mod, w13, w2, nb, ctx_tiles):
    b, s, _ = xa.shape
    d_ff = w2.shape[0]
    return pl.pallas_call(
        functools.partial(_ffn_kernel, d_ff=d_ff),
        grid=(b, s // TM),
        in_specs=[_rows(D), _whole((1, D)), _mod_spec(3, nb, ctx_tiles, 0), _mod_spec(4, nb, ctx_tiles, 0),
                  _mod_spec(5, nb, ctx_tiles, 0), _whole((D, 2 * d_ff)), _whole((d_ff, D))],
        out_specs=_rows(D),
        out_shape=jax.ShapeDtypeStruct((b, s, D), F32),
        compiler_params=_cp(("parallel", "parallel"), VMEM_LIMIT),
        name="ffn",
    )(xa, gain.reshape(1, D), mod, mod, mod, w13, w2)


def _router_kernel(x_ref, g_ref, sh_ref, sc_ref, wr_ref, br_ref, hb_ref, cmb_ref):
    h = _rms(x_ref[...]) * g_ref[...]
    h = h * (1.0 + sc_ref[...]) + sh_ref[...]
    hb_ref[...] = h.astype(BF16)
    logits = jnp.dot(h, wr_ref[...], precision=HI, preferred_element_type=F32) + br_ref[...]
    lane = _lane(logits.shape)
    logits = jnp.where(lane < N_EXP, logits, NEG)
    m1 = jnp.max(logits, axis=-1, keepdims=True)
    i1 = jnp.min(jnp.where(logits == m1, lane, LANE), axis=-1, keepdims=True)
    rest = jnp.where(lane == i1, NEG, logits)
    m2 = jnp.max(rest, axis=-1, keepdims=True)
    i2 = jnp.min(jnp.where(rest == m2, lane, LANE), axis=-1, keepdims=True)
    e2 = jnp.exp(m2 - m1)
    inv = 1.0 / (1.0 + e2)
    cmb_ref[...] = jnp.where(lane == i1, inv, 0.0) + jnp.where(lane == i2, e2 * inv, 0.0)


def _router(xl, gain, mod, wr, br):
    b, t, _ = xl.shape
    spec = lambda k: pl.BlockSpec((None, 1, D), lambda bb, j: (bb, 0, k))
    return pl.pallas_call(
        _router_kernel,
        grid=(b, t // TM),
        in_specs=[_rows(D), _whole((1, D)), spec(3), spec(4), _whole((D, LANE)), _whole((1, LANE))],
        out_specs=[_rows(D), _rows(LANE)],
        out_shape=[jax.ShapeDtypeStruct((b, t, D), BF16), jax.ShapeDtypeStruct((b, t, LANE), F32)],
        compiler_params=_cp(("parallel", "parallel"), VMEM_LIMIT),
        name="router",
    )(xl, gain.reshape(1, D), mod, mod, wr, br)


def _moe_kernel(hb_ref, cmb_ref, gate_ref, acc_ref, w13_ref, w2_ref, out_ref, *, d_ff):
    e = pl.program_id(0)
    hb = hb_ref[...]
    a = jnp.dot(hb, w13_ref[:, :d_ff], preferred_element_type=F32)
    g = jnp.dot(hb, w13_ref[:, d_ff:], preferred_element_type=F32)
    u = (g * jax.nn.sigmoid(g) * a).astype(BF16)
    y = jnp.dot(u, w2_ref[...], preferred_element_type=F32)
    cmb = cmb_ref[...]
    cw = jnp.sum(jnp.where(_lane(cmb.shape) == e, cmb, 0.0), axis=-1, keepdims=True)
    out_ref[...] = acc_ref[...] + gate_ref[...] * (cw * y)


def _moe(hb, cmb, mod, xl, w13, w2):
    b, t, _ = xl.shape
    d_ff = w2.shape[1]
    rows = lambda w: pl.BlockSpec((None, TM, w), lambda e, bb, j: (bb, j, 0))
    once = dict(pipeline_mode=pl.Buffered(1))
    return pl.pallas_call(
        functools.partial(_moe_kernel, d_ff=d_ff),
        grid=(N_EXP, b, t // TM),
        in_specs=[rows(D), rows(LANE), pl.BlockSpec((None, 1, D), lambda e, bb, j: (bb, 0, 5)), rows(D),
                  pl.BlockSpec((None, D, 2 * d_ff), lambda e, bb, j: (e, 0, 0), **once),
                  pl.BlockSpec((None, d_ff, D), lambda e, bb, j: (e, 0, 0), **once)],
        out_specs=rows(D),
        out_shape=jax.ShapeDtypeStruct((b, t, D), F32),
        input_output_aliases={3: 0},
        compiler_params=_cp(("arbitrary", "arbitrary", "arbitrary"), VMEM_LIMIT),
        name="moe_dense",
    )(hb, cmb, mod, xl, w13, w2)


def _pad_cols(w, n):
    return jnp.pad(w, ((0, 0), (0, n - w.shape[1])))


def _layout_w_in(w):
    ml = 4 * ML_W
    a, gates = w[:, :ml], w[:, ml:ml + 4 * ML_H]
    o = ml + 4 * ML_H
    cq, ckv, kr = w[:, o:o + MLA_QR], w[:, o + MLA_QR:o + MLA_QR + MLA_KVR], w[:, o + MLA_QR + MLA_KVR:o + MLA_QR + MLA_KVR + MLA_ROPE]
    o += MLA_QR + MLA_KVR + MLA_ROPE
    sw = w[:, o:o + (SW_H + 2 * SW_KV) * SW_DH]
    o += (SW_H + 2 * SW_KV) * SW_DH
    lru = w[:, o:]
    z = lambda n: jnp.zeros((w.shape[0], n), w.dtype)
    cols = [a, _pad_cols(gates, LANE), _pad_cols(cq, 2 * LANE), ckv, z(MLA_NOPE), kr, z(LANE - MLA_NOPE - MLA_ROPE), sw, lru]
    return jnp.concatenate(cols, axis=1).astype(BF16)


IN_SEGS = ((0, 4 * ML_W), (4 * ML_W, LANE), (4 * ML_W + LANE, 4 * LANE), (4 * ML_W + 5 * LANE, 4 * LANE),
           (4 * ML_W + 9 * LANE, 4 * LANE))
IN_DTYPES = (BF16, F32, F32, F32, F32)


def _row(v, n=None):
    v = v.reshape(1, -1)
    return v if n is None else _pad_cols(v, n)


def _layer_params(l, ml_gate_b, ml_out_norm, mla_q_norm, mla_w_uq, mla_kv_norm, mla_w_ukv, mla_q_gain, mla_k_gain,
                  sw_q_gain, sw_k_gain, lru_conv_w, lru_conv_b, lru_wa, lru_ba, lru_wx, lru_bx, lru_lam):
    hq = MLA_NOPE + MLA_ROPE
    wq = mla_w_uq[l].reshape(MLA_QR, MLA_H, hq)
    wq = jnp.pad(wq, ((0, 2 * LANE - MLA_QR), (0, 0), (0, LANE - hq))).reshape(2 * LANE, MLA_H * LANE)
    wkv = mla_w_ukv[l].reshape(MLA_KVR, MLA_H, MLA_NOPE + MLA_V)
    wk = jnp.pad(wkv[:, :, :MLA_NOPE], ((0, 0), (0, 0), (0, LANE - MLA_NOPE))).reshape(MLA_KVR, MLA_H * LANE)
    wv = wkv[:, :, MLA_NOPE:].reshape(MLA_KVR, MLA_H * MLA_V)

    def blockdiag(wb):
        bw = LRU_W // LRU_BLOCKS
        eye = jnp.eye(LRU_BLOCKS, dtype=wb.dtype)
        return jnp.einsum('xncd,nm->xncmd', wb, eye).reshape(2, LRU_W, LRU_W)

    return dict(
        gate_b=_row(ml_gate_b[l], LANE), out_norm=_row(ml_out_norm[l]),
        q_norm=_row(mla_q_norm[l], 2 * LANE), w_uq=wq.astype(BF16), kv_norm=_row(mla_kv_norm[l]),
        w_uk=wk.astype(BF16), w_uv=wv.astype(BF16),
        q_gain=_row(mla_q_gain[l], LANE), k_gain=_row(mla_k_gain[l], LANE),
        sw_q_gain=_row(jnp.tile(sw_q_gain[l], 2)), sw_k_gain=_row(jnp.tile(sw_k_gain[l], 2)),
        conv_w=lru_conv_w[l], conv_b=_row(lru_conv_b[l]),
        wa=blockdiag(lru_wa[l]).astype(BF16), ba=lru_ba[l].reshape(2, 1, LRU_W),
        wx=blockdiag(lru_wx[l]).astype(BF16), bx=lru_bx[l].reshape(2, 1, LRU_W),
        lam=lru_lam[l].reshape(2, 1, LRU_W),
    )


def _mix_layer(xa, mod, gain, w_in_l, w_out_l, sink, p, tabs, lc, nb, ctx_out):
    ctx_tiles = lc // TM
    za, zg, zb, zc, zd = _proj(xa, gain, mod, w_in_l, IN_SEGS, IN_DTYPES, nb, ctx_tiles)
    hfa, hba = _mlstm(za, zg, p["gate_b"], lc)
    qm, km, vm, qs, ks, vs = _prep(zb, zc, tabs[0], tabs[1], p)
    yb = _mla_attn(qm, km, vm, lc)
    pad = ((0, 0), (0, 0), (0, SW_BLOCK), (0, 0))
    yc = _swa_attn(qs, jnp.pad(ks, pad), jnp.pad(vs, pad), sink, lc)
    hfd, hbd = _lru(zd, p, lc)
    off = 0 if ctx_out else ctx_tiles
    return _out_proj(xa, mod, hfa, hba, za, p["out_norm"], yb, yc, hfd, hbd, zd, w_out_l, nb, ctx_tiles, off)


def kernel(x, c, ctx, c_ctx, ada_w, ada_b, norm_mix, norm_ffn, w_in, w_out, ml_gate_b, ml_out_norm, mla_q_norm, mla_w_uq, mla_kv_norm, mla_w_ukv, mla_q_gain, mla_k_gain, sw_q_gain, sw_k_gain, sw_sink, lru_conv_w, lru_conv_b, lru_wa, lru_ba, lru_wx, lru_bx, lru_lam, ffn_w13, ffn_w2, moe_router, moe_router_b, moe_w13, moe_w2):
    nb, t_len, _ = x.shape
    lc = ctx.shape[1]
    depth = ada_w.shape[0]
    assert nb < SUB and lc % TM == 0 and t_len % TM == 0 and t_len % GRID_W == 0
    cc = jnp.zeros((SUB, D), F32).at[:nb].set(c).at[nb].set(c_ctx)
    mods = _ada_mod(cc, ada_w, ada_b)
    tabs = (_rope_tables(t_len, lc, LANE, MLA_ROPE, MLA_NOPE), _rope_tables(t_len, lc, SW_DH, SW_DH, 0))
    xa = jnp.concatenate([ctx, x], axis=1)
    for l in range(depth):
        ctx_out = l < depth - 1
        mod = mods[l].reshape(SUB, 1, 6 * D)
        p = _layer_params(l, ml_gate_b, ml_out_norm, mla_q_norm, mla_w_uq, mla_kv_norm, mla_w_ukv, mla_q_gain,
                          mla_k_gain, sw_q_gain, sw_k_gain, lru_conv_w, lru_conv_b, lru_wa, lru_ba, lru_wx, lru_bx,
                          lru_lam)
        xa = _mix_layer(xa, mod, norm_mix[l], _layout_w_in(w_in[l]), w_out[l].astype(BF16), sw_sink[l], p, tabs,
                        lc, nb, ctx_out)
        if l % 2 == 0:
            xa = _ffn(xa, norm_ffn[l], mod, ffn_w13[l // 2].astype(BF16), ffn_w2[l // 2].astype(BF16), nb,
                      lc // TM if ctx_out else 0)
        else:
            hb, cmb = _router(xa, norm_ffn[l], mod, _pad_cols(moe_router[l // 2], LANE),
                              _row(moe_router_b[l // 2], LANE))
            xa = _moe(hb, cmb, mod, xa, moe_w13[l // 2].astype(BF16), moe_w2[l // 2].astype(BF16))
    return xa
```

```python
import functools

import numpy as np
import jax
import jax.numpy as jnp
from jax import lax
from jax.experimental import pallas as pl
from jax.experimental.pallas import tpu as pltpu

F32 = jnp.float32
BF16 = jnp.bfloat16
HI = lax.Precision.HIGHEST

D = 1024
GRID_W = 64
EPS = 1e-6
ROPE_BASE = 10000.0
ML_H, ML_DH, ML_W, ML_CHUNK = 4, 64, 256, 128
MLA_H, MLA_QR, MLA_KVR, MLA_NOPE, MLA_ROPE, MLA_V = 4, 192, 128, 64, 32, 64
SW_H, SW_KV, SW_DH, SW_WINDOW, SW_BLOCK = 4, 2, 64, 128, 128
LRU_W, LRU_BLOCKS, LRU_C = 256, 4, 8.0
N_EXP = 8

LANE = 128
SUB = 8
TM = 256
NEG = -1e30
VMEM_LIMIT = 56 * 1024 * 1024


def _cp(sem, vmem=None):
    return pltpu.CompilerParams(dimension_semantics=sem, vmem_limit_bytes=vmem)


def _mod_spec(chunk, nb, ctx_tiles, off):
    return pl.BlockSpec((None, 1, D), lambda b, j: (jnp.where(j + off < ctx_tiles, nb, b), 0, chunk))


def _rows(width, off=0, col=0, tm=TM):
    return pl.BlockSpec((None, tm, width), lambda b, j: (b, j + off, col))


def _whole(shape):
    nd = len(shape)
    return pl.BlockSpec(shape, lambda *_: (0,) * nd)


def _rms(x):
    return x * lax.rsqrt(jnp.mean(x * x, axis=-1, keepdims=True) + EPS)


def _lane(shape):
    return lax.broadcasted_iota(jnp.int32, shape, len(shape) - 1)


def _group_rms(blk, bounds):
    lane = _lane(blk.shape)
    sq = blk * blk
    scale = jnp.zeros_like(blk)
    for lo, hi in bounds:
        msk = (lane >= lo) & (lane < hi)
        ms = jnp.sum(jnp.where(msk, sq, 0.0), axis=-1, keepdims=True) * (1.0 / (hi - lo))
        scale = jnp.where(msk, lax.rsqrt(ms + EPS), scale)
    return blk * scale


def _rope(blk, tab_ref, shift):
    n = blk.shape[-1]
    return (blk * tab_ref[0] + pltpu.roll(blk, n - shift, 1) * tab_ref[1]
            + pltpu.roll(blk, shift, 1) * tab_ref[2])


def _rope_tables(t_len, lc, width, dims, offset):
    half, nf = dims // 2, dims // 4
    p = np.arange(dims)
    i = p % half
    f = i % nf
    first = i < nf
    freq = ROPE_BASE ** (-f.astype(np.float64) / nf)
    t = np.arange(t_len)
    pos = np.where(p[None, :] < half, (t // GRID_W)[:, None], (t % GRID_W)[:, None]).astype(np.float64)
    ang = pos * freq[None, :]
    cos, sin = np.cos(ang), np.sin(ang)
    tab = np.zeros((3, lc + t_len, width), np.float64)
    tab[0] = 1.0
    tab[0, lc:, offset:offset + dims] = cos
    tab[1, lc:, offset:offset + dims] = np.where(first[None, :], -sin, 0.0)
    tab[2, lc:, offset:offset + dims] = np.where(first[None, :], 0.0, sin)
    reps = LANE // width
    return jnp.asarray(np.tile(tab, (1, 1, reps)), F32)


def _ada_kernel(c_ref, w_ref, b_ref, o_ref):
    c = c_ref[...]
    s = c * jax.nn.sigmoid(c)
    o_ref[...] = jnp.dot(s, w_ref[...], precision=HI, preferred_element_type=F32) + b_ref[...]


def _ada_mod(cc, ada_w, ada_b):
    depth = ada_w.shape[0]
    n = ada_w.shape[2]
    tn = 1024
    return pl.pallas_call(
        _ada_kernel,
        grid=(depth, n // tn),
        in_specs=[
            pl.BlockSpec((SUB, D), lambda l, j: (0, 0)),
            pl.BlockSpec((None, D, tn), lambda l, j: (l, 0, j)),
            pl.BlockSpec((None, 1, tn), lambda l, j: (l, 0, j)),
        ],
        out_specs=pl.BlockSpec((None, SUB, tn), lambda l, j: (l, 0, j)),
        out_shape=jax.ShapeDtypeStruct((depth, SUB, n), F32),
        compiler_params=_cp(("parallel", "parallel")),
        name="ada_mod",
    )(cc, ada_w, ada_b.reshape(depth, 1, n))


def _proj_kernel(x_ref, g_ref, sh_ref, sc_ref, w_ref, *out_refs, segs):
    h = _rms(x_ref[...]) * g_ref[...]
    h = h * (1.0 + sc_ref[...]) + sh_ref[...]
    hb = h.astype(BF16)
    for (start, width), o_ref in zip(segs, out_refs):
        o_ref[...] = jnp.dot(hb, w_ref[:, start:start + width], preferred_element_type=F32).astype(o_ref.dtype)


def _proj(xa, gain, mod, w, segs, dtypes, nb, ctx_tiles):
    b, s, _ = xa.shape
    n = w.shape[1]
    return pl.pallas_call(
        functools.partial(_proj_kernel, segs=segs),
        grid=(b, s // TM),
        in_specs=[
            _rows(D),
            _whole((1, D)),
            _mod_spec(0, nb, ctx_tiles, 0),
            _mod_spec(1, nb, ctx_tiles, 0),
            _whole((D, n)),
        ],
        out_specs=[_rows(wd) for _, wd in segs],
        out_shape=[jax.ShapeDtypeStruct((b, s, wd), dt) for (_, wd), dt in zip(segs, dtypes)],
        compiler_params=_cp(("parallel", "parallel"), VMEM_LIMIT),
        name="in_proj",
    )(xa, gain.reshape(1, D), mod, mod, w)


def _prep_kernel(zb_ref, zc_ref, tm_ref, ts_ref, qn_ref, wuq_ref, kvn_ref, wk_ref, wv_ref, qg_ref, kg_ref,
                 sqg_ref, skg_ref, qm_ref, km_ref, vm_ref, qs_ref, ks_ref, vs_ref):
    zb = zb_ref[...]
    cq = zb[:, :2 * LANE]
    cqn = cq * lax.rsqrt(jnp.sum(cq * cq, axis=-1, keepdims=True) * (1.0 / MLA_QR) + EPS) * qn_ref[...]
    qf = jnp.dot(cqn.astype(BF16), wuq_ref[...], preferred_element_type=F32)
    q_scale = (MLA_NOPE + MLA_ROPE) ** -0.5 * LOG2E
    head_bounds = [(0, MLA_NOPE), (MLA_NOPE, MLA_NOPE + MLA_ROPE)]
    for h in range(MLA_H):
        blk = _group_rms(qf[:, h * LANE:(h + 1) * LANE], head_bounds) * qg_ref[...]
        qm_ref[h] = (_rope(blk, tm_ref, MLA_ROPE // 4) * q_scale).astype(BF16)
    ckvn = _rms(zb[:, 2 * LANE:3 * LANE]) * kvn_ref[...]
    ckvb = ckvn.astype(BF16)
    kf = jnp.dot(ckvb, wk_ref[...], preferred_element_type=F32)
    vf = jnp.dot(ckvb, wv_ref[...], preferred_element_type=F32)
    kg = kg_ref[...]
    kr = _group_rms(zb[:, 3 * LANE:4 * LANE], head_bounds[1:]) * kg
    kr = _rope(kr, tm_ref, MLA_ROPE // 4)
    for h in range(MLA_H):
        kn = _group_rms(kf[:, h * LANE:(h + 1) * LANE], head_bounds[:1]) * kg
        km_ref[h] = (kn + kr).astype(BF16)
        vm_ref[h] = vf[:, h * MLA_V:(h + 1) * MLA_V].astype(BF16)
    zc = zc_ref[...]
    pair = [(0, SW_DH), (SW_DH, 2 * SW_DH)]
    sw_scale = SW_DH ** -0.5
    for half in range(2):
        blk = _group_rms(zc[:, half * LANE:(half + 1) * LANE], pair) * sqg_ref[...]
        qs_ref[:, half * LANE:(half + 1) * LANE] = (_rope(blk, ts_ref, SW_DH // 4) * sw_scale).astype(BF16)
    kb = _rope(_group_rms(zc[:, 2 * LANE:3 * LANE], pair) * skg_ref[...], ts_ref, SW_DH // 4).astype(BF16)
    vb = zc[:, 3 * LANE:4 * LANE].astype(BF16)
    for j in range(SW_KV):
        ks_ref[j] = kb[:, j * SW_DH:(j + 1) * SW_DH]
        vs_ref[j] = vb[:, j * SW_DH:(j + 1) * SW_DH]


def _prep(zb, zc, tab_mla, tab_sw, p):
    b, s, _ = zb.shape
    tab = pl.BlockSpec((3, TM, LANE), lambda bb, j: (0, j, 0))
    heads = lambda nh, w: pl.BlockSpec((None, nh, TM, w), lambda bb, j: (bb, 0, j, 0))
    return pl.pallas_call(
        _prep_kernel,
        grid=(b, s // TM),
        in_specs=[_rows(4 * LANE), _rows(4 * LANE), tab, tab,
                  _whole((1, 2 * LANE)), _whole((2 * LANE, 4 * LANE)), _whole((1, LANE)),
                  _whole((LANE, 4 * LANE)), _whole((LANE, 2 * LANE)), _whole((1, LANE)), _whole((1, LANE)),
                  _whole((1, LANE)), _whole((1, LANE))],
        out_specs=[heads(MLA_H, LANE), heads(MLA_H, LANE), heads(MLA_H, MLA_V),
                   _rows(2 * LANE), heads(SW_KV, SW_DH), heads(SW_KV, SW_DH)],
        out_shape=[jax.ShapeDtypeStruct((b, MLA_H, s, LANE), BF16),
                   jax.ShapeDtypeStruct((b, MLA_H, s, LANE), BF16),
                   jax.ShapeDtypeStruct((b, MLA_H, s, MLA_V), BF16),
                   jax.ShapeDtypeStruct((b, s, 2 * LANE), BF16),
                   jax.ShapeDtypeStruct((b, SW_KV, s, SW_DH), BF16),
                   jax.ShapeDtypeStruct((b, SW_KV, s, SW_DH), BF16)],
        compiler_params=_cp(("parallel", "parallel"), VMEM_LIMIT),
        name="head_prep",
    )(zb, zc, tab_mla, tab_sw, p["q_norm"], p["w_uq"], p["kv_norm"], p["w_uk"], p["w_uv"], p["q_gain"],
      p["k_gain"], p["sw_q_gain"], p["sw_k_gain"])


MLA_TQ = 256
MLA_KC = 512
LOG2E = 1.4426950408889634


def _mla_kernel(qt_ref, k_ref, vtc_ref, vtl_ref, o_ref, m_ref, l_ref, acc_ref, st_ref, *, lc, n_lat, ctx_tiles):
    i = pl.program_id(1)
    n_it = jnp.where(i < ctx_tiles, 0, n_lat)

    def update(h, st, vt, first):
        cm = jnp.max(st, axis=0, keepdims=True)
        m_new = cm if first else jnp.maximum(m_ref[h], cm)
        p = jnp.exp2(st - m_new)
        ps = jnp.sum(p, axis=0, keepdims=True)
        pv = jnp.dot(vt, p.astype(BF16), preferred_element_type=F32)
        if first:
            l_ref[h], acc_ref[h] = ps, pv
        else:
            alpha = jnp.exp2(m_ref[h] - m_new)
            l_ref[h] = alpha * l_ref[h] + ps
            acc_ref[h] = alpha * acc_ref[h] + pv
        m_ref[h] = m_new

    def scores(c, slot):
        off = pl.multiple_of(lc + c * MLA_KC, LANE)
        for h in range(MLA_H):
            st_ref[slot, h] = jnp.dot(k_ref[h, pl.ds(off, MLA_KC), :], qt_ref[h], preferred_element_type=F32)

    scores(0, 0)
    for h in range(MLA_H):
        update(h, jnp.dot(k_ref[h, 0:lc, :], qt_ref[h], preferred_element_type=F32), vtc_ref[h], True)

    def body(c, carry):
        slot = c & 1
        scores(jnp.minimum(c + 1, n_lat - 1), 1 - slot)
        for h in range(MLA_H):
            update(h, st_ref[slot, h], vtl_ref[h, c], False)
        return carry

    lax.fori_loop(0, n_it, body, 0)
    outs = [(acc_ref[h] / l_ref[h]).T for h in range(MLA_H)]
    o_ref[...] = jnp.concatenate(outs, axis=-1).astype(o_ref.dtype)


def _mla_attn(q, k, v, lc):
    b, h, s, _ = q.shape
    n_lat = (s - lc) // MLA_KC
    qt = jnp.swapaxes(q, 2, 3)
    vtc = jnp.swapaxes(v[:, :, :lc], 2, 3)
    vtl = jnp.swapaxes(v[:, :, lc:].reshape(b, h, n_lat, MLA_KC, MLA_V), 3, 4)
    kern = functools.partial(_mla_kernel, lc=lc, n_lat=n_lat, ctx_tiles=lc // MLA_TQ)
    return pl.pallas_call(
        kern,
        grid=(b, s // MLA_TQ),
        in_specs=[pl.BlockSpec((None, h, LANE, MLA_TQ), lambda bb, i: (bb, 0, 0, i)),
                  pl.BlockSpec((None, h, s, LANE), lambda bb, i: (bb, 0, 0, 0)),
                  pl.BlockSpec((None, h, MLA_V, lc), lambda bb, i: (bb, 0, 0, 0)),
                  pl.BlockSpec((None, h, n_lat, MLA_V, MLA_KC), lambda bb, i: (bb, 0, 0, 0, 0))],
        out_specs=pl.BlockSpec((None, MLA_TQ, h * MLA_V), lambda bb, i: (bb, i, 0)),
        out_shape=jax.ShapeDtypeStruct((b, s, h * MLA_V), BF16),
        scratch_shapes=[pltpu.VMEM((h, 1, MLA_TQ), F32), pltpu.VMEM((h, 1, MLA_TQ), F32),
                        pltpu.VMEM((h, MLA_V, MLA_TQ), F32), pltpu.VMEM((2, h, MLA_KC, MLA_TQ), F32)],
        compiler_params=_cp(("parallel", "arbitrary"), VMEM_LIMIT),
        name="mla_attn",
    )(qt, k, vtc, vtl)


def _swa_kernel(sink_ref, q_ref, k_ref, v_ref, o_ref, *, lc, t_len):
    j = pl.program_id(1)
    n = pl.program_id(2)
    blk = SW_BLOCK
    q2 = q_ref[...]
    qs = jnp.concatenate([q2[:, :SW_DH], q2[:, SW_DH:]], axis=0)
    start = pl.multiple_of(jnp.maximum(n - 1, 0) * blk, blk)
    nw = blk + 2 * SW_WINDOW
    kw = k_ref[pl.ds(start, nw), :]
    vw = v_ref[pl.ds(start, nw), :]
    kc = k_ref[pl.ds(0, lc), :]
    vc = v_ref[pl.ds(0, lc), :]
    dn = (((1,), (1,)), ((), ()))
    s_loc = lax.dot_general(qs, kw, dn, preferred_element_type=F32)
    s_ctx = lax.dot_general(qs, kc, dn, preferred_element_type=F32)
    row = lax.broadcasted_iota(jnp.int32, s_loc.shape, 0)
    col = lax.broadcasted_iota(jnp.int32, s_loc.shape, 1)
    qpos = n * blk - lc + (row & (blk - 1))
    kpos = start - lc + col
    valid = (n * blk >= lc) & (kpos >= 0) & (kpos < t_len) & (jnp.abs(qpos - kpos) <= SW_WINDOW)
    s_loc = jnp.where(valid, s_loc, NEG)
    rows1 = lax.broadcasted_iota(jnp.int32, (2 * blk, 1), 0)
    sink = jnp.where(rows1 < blk, sink_ref[2 * j], sink_ref[2 * j + 1])
    m = jnp.maximum(jnp.maximum(jnp.max(s_loc, axis=-1, keepdims=True), jnp.max(s_ctx, axis=-1, keepdims=True)), sink)
    e_loc = jnp.exp(s_loc - m)
    e_ctx = jnp.exp(s_ctx - m)
    den = jnp.sum(e_loc, axis=-1, keepdims=True) + jnp.sum(e_ctx, axis=-1, keepdims=True) + jnp.exp(sink - m)
    o = (jnp.dot(e_loc.astype(BF16), vw, preferred_element_type=F32)
         + jnp.dot(e_ctx.astype(BF16), vc, preferred_element_type=F32)) / den
    o_ref[...] = jnp.concatenate([o[:blk], o[blk:]], axis=-1).astype(o_ref.dtype)


def _swa_attn(q, k, v, sink, lc):
    b, s, _ = q.shape
    t_len = s - lc
    sp = k.shape[2]
    kern = functools.partial(_swa_kernel, lc=lc, t_len=t_len)
    return pl.pallas_call(
        kern,
        grid=(b, SW_KV, s // SW_BLOCK),
        in_specs=[pl.BlockSpec(memory_space=pltpu.SMEM),
                  pl.BlockSpec((None, SW_BLOCK, LANE), lambda bb, j, n: (bb, n, j)),
                  pl.BlockSpec((None, None, sp, SW_DH), lambda bb, j, n: (bb, j, 0, 0)),
                  pl.BlockSpec((None, None, sp, SW_DH), lambda bb, j, n: (bb, j, 0, 0))],
        out_specs=pl.BlockSpec((None, SW_BLOCK, LANE), lambda bb, j, n: (bb, n, j)),
        out_shape=jax.ShapeDtypeStruct((b, s, 2 * LANE), BF16),
        compiler_params=_cp(("parallel", "parallel", "arbitrary"), VMEM_LIMIT),
        name="swa_attn",
    )(sink, q, k, v)


def _mirror(j, n_ctx, n_all):
    return jnp.where(j < n_ctx, n_ctx - 1 - j, n_all + n_ctx - 1 - j)


def _mlstm_kernel(xf_ref, xb_ref, gf_ref, gb_ref, bias_ref, hf_ref, hb_ref, c_ref, n_ref, m_ref):
    L = ML_CHUNK

    @pl.when(pl.program_id(1) == 0)
    def _():
        c_ref[...] = jnp.zeros_like(c_ref)
        n_ref[...] = jnp.zeros_like(n_ref)
        m_ref[...] = jnp.zeros_like(m_ref)

    row = lax.broadcasted_iota(jnp.int32, (L, L), 0)
    col = lax.broadcasted_iota(jnp.int32, (L, L), 1)
    dn_t = (((1,), (1,)), ((), ()))
    for direction, (x_ref, g_ref, o_ref) in enumerate(((xf_ref, gf_ref, hf_ref), (xb_ref, gb_ref, hb_ref))):
        causal = (col <= row) if direction == 0 else (col >= row)
        tri = causal.astype(F32)
        x = x_ref[...]
        g = g_ref[...] + bias_ref[...]
        gt = g.T
        lf = jax.nn.log_sigmoid(g)
        lft = jax.nn.log_sigmoid(gt)
        bcol = jnp.dot(tri, lf, precision=HI, preferred_element_type=F32)
        brow = lax.dot_general(lft, tri, dn_t, precision=HI, preferred_element_type=F32)
        last = L - 1 if direction == 0 else 0
        outs = []
        for h in range(ML_H):
            chain = direction * ML_H + h
            gi, gf = 2 * ML_H * direction + h, 2 * ML_H * direction + ML_H + h
            ic_col, b_col = g[:, gi:gi + 1], bcol[:, gf:gf + 1]
            ic_row, b_row = gt[gi:gi + 1, :], brow[gf:gf + 1, :]
            q = x[:, h * ML_DH:(h + 1) * ML_DH]
            k = x[:, ML_W + h * ML_DH:ML_W + (h + 1) * ML_DH] * (ML_DH ** -0.5)
            v = x[:, 2 * ML_W + h * ML_DH:2 * ML_W + (h + 1) * ML_DH]
            c_st, n_st, m_st = c_ref[chain], n_ref[chain], m_ref[chain]
            d = jnp.where(causal, b_col + (ic_row - b_row), NEG)
            m_inter = b_col + m_st
            m_t = jnp.maximum(m_inter, jnp.max(d, axis=-1, keepdims=True))
            w_inter = jnp.exp(m_inter - m_t)
            s = lax.dot_general(q, k, dn_t, preferred_element_type=F32) * jnp.exp(d - m_t)
            num = (jnp.dot(s.astype(BF16), v, preferred_element_type=F32)
                   + w_inter * lax.dot_general(q, c_st.astype(BF16), dn_t, preferred_element_type=F32))
            den = (jnp.sum(s, axis=-1, keepdims=True)
                   + w_inter * jnp.sum(q.astype(F32) * n_st, axis=-1, keepdims=True))
            outs.append(num / jnp.maximum(jnp.abs(den), jnp.exp(-m_t)))
            b_last = b_col[last:last + 1, :]
            g_col = b_last - b_col + ic_col
            m_new = jnp.maximum(b_last + m_st, jnp.max(g_col, axis=0, keepdims=True))
            w_col = jnp.exp(g_col - m_new)
            decay = jnp.exp(b_last + m_st - m_new)
            vw = (v.astype(F32) * w_col).astype(BF16)
            c_ref[chain] = decay * c_st + lax.dot_general(vw, k, (((0,), (0,)), ((), ())), preferred_element_type=F32)
            n_ref[chain] = decay * n_st + jnp.sum(k.astype(F32) * w_col, axis=0, keepdims=True)
            m_ref[chain] = m_new
        o_ref[...] = jnp.concatenate(outs, axis=-1)


def _mlstm(za, zg, bias, lc):
    b, s, _ = za.shape
    n_all, n_ctx = s // ML_CHUNK, lc // ML_CHUNK
    fwd = lambda w: pl.BlockSpec((None, ML_CHUNK, w), lambda bb, j: (bb, j, 0))
    bwd = lambda w: pl.BlockSpec((None, ML_CHUNK, w), lambda bb, j: (bb, _mirror(j, n_ctx, n_all), 0))
    return pl.pallas_call(
        _mlstm_kernel,
        grid=(b, n_all),
        in_specs=[fwd(4 * ML_W), bwd(4 * ML_W), fwd(LANE), bwd(LANE), _whole((1, LANE))],
        out_specs=[fwd(ML_W), bwd(ML_W)],
        out_shape=[jax.ShapeDtypeStruct((b, s, ML_W), F32)] * 2,
        scratch_shapes=[pltpu.VMEM((2 * ML_H, ML_DH, ML_DH), F32),
                        pltpu.VMEM((2 * ML_H, 1, ML_DH), F32),
                        pltpu.VMEM((2 * ML_H, 1, 1), F32)],
        compiler_params=_cp(("parallel", "arbitrary"), VMEM_LIMIT),
        name="mlstm_scan",
    )(za, za, zg, zg, bias)


LRU_T = 256
LRU_HALO = SUB


def _lru_kernel(uf_ref, pf_ref, nf_ref, ub_ref, pb_ref, nb_ref, cw_ref, cb_ref, wa_ref, ba_ref, wx_ref, bx_ref,
                lam_ref, hf_ref, hb_ref, carry_ref, *, n_ctx, n_all):
    T = LRU_T
    j = pl.program_id(1)

    @pl.when(j == 0)
    def _():
        carry_ref[...] = jnp.zeros_like(carry_ref)

    cw = cw_ref[...]
    row = lax.broadcasted_iota(jnp.int32, (T, LRU_W), 0)
    for direction, (u_ref, p_ref, n_ref, o_ref) in enumerate(((uf_ref, pf_ref, nf_ref, hf_ref),
                                                              (ub_ref, pb_ref, nb_ref, hb_ref))):
        c = j if direction == 0 else _mirror(j, n_ctx, n_all)
        has_prev = ((c != 0) & (c != n_ctx)).astype(F32)
        has_next = ((c != n_ctx - 1) & (c != n_all - 1)).astype(F32)
        ext = jnp.concatenate([p_ref[...] * has_prev, u_ref[...], n_ref[...] * has_next], axis=0)
        n_ext = T + 2 * LRU_HALO
        u = cb_ref[...] + cw[2:3] * ext[LRU_HALO:LRU_HALO + T]
        for tap, sh in ((0, 2), (1, 1), (3, n_ext - 1)):
            u = u + cw[tap:tap + 1] * pltpu.roll(ext, sh, 0)[LRU_HALO:LRU_HALO + T]
        ub = u.astype(BF16)
        r = jax.nn.sigmoid(jnp.dot(ub, wa_ref[direction], preferred_element_type=F32) + ba_ref[direction])
        i = jax.nn.sigmoid(jnp.dot(ub, wx_ref[direction], preferred_element_type=F32) + bx_ref[direction])
        lam = lam_ref[direction]
        log_a = (-LRU_C) * r * jnp.log1p(jnp.exp(-lam))
        a = jnp.exp(log_a)
        bb = jnp.sqrt(1.0 - a * a) * (i * u)
        sh = 1
        while sh < T:
            if direction == 0:
                ok = row >= sh
                a_s, b_s = pltpu.roll(a, sh, 0), pltpu.roll(bb, sh, 0)
            else:
                ok = row < T - sh
                a_s, b_s = pltpu.roll(a, T - sh, 0), pltpu.roll(bb, T - sh, 0)
            bb = jnp.where(ok, a * b_s + bb, bb)
            a = jnp.where(ok, a * a_s, a)
            sh *= 2
        hcur = bb + a * carry_ref[direction]
        o_ref[...] = hcur
        last = T - 1 if direction == 0 else 0
        carry_ref[direction] = hcur[last:last + 1, :]


def _lru(zd, p, lc):
    b, s, _ = zd.shape
    n_all, n_ctx = s // LRU_T, lc // LRU_T
    per = LRU_T // LRU_HALO
    n_halo = s // LRU_HALO
    ident = lambda j: j
    mirr = lambda j: _mirror(j, n_ctx, n_all)

    def specs(cf):
        return [pl.BlockSpec((None, LRU_T, LRU_W), lambda bb, j: (bb, cf(j), 0)),
                pl.BlockSpec((None, LRU_HALO, LRU_W), lambda bb, j: (bb, jnp.maximum(cf(j) * per - 1, 0), 0)),
                pl.BlockSpec((None, LRU_HALO, LRU_W), lambda bb, j: (bb, jnp.minimum((cf(j) + 1) * per, n_halo - 1), 0))]

    vec = _whole((2, 1, LRU_W))
    mat = _whole((2, LRU_W, LRU_W))
    return pl.pallas_call(
        functools.partial(_lru_kernel, n_ctx=n_ctx, n_all=n_all),
        grid=(b, n_all),
        in_specs=specs(ident) + specs(mirr) + [_whole((4, LRU_W)), _whole((1, LRU_W)), mat, vec, mat, vec, vec],
        out_specs=[pl.BlockSpec((None, LRU_T, LRU_W), lambda bb, j: (bb, j, 0)),
                   pl.BlockSpec((None, LRU_T, LRU_W), lambda bb, j: (bb, mirr(j), 0))],
        out_shape=[jax.ShapeDtypeStruct((b, s, LRU_W), F32)] * 2,
        scratch_shapes=[pltpu.VMEM((2, 1, LRU_W), F32)],
        compiler_params=_cp(("parallel", "arbitrary"), VMEM_LIMIT),
        name="rglru_scan",
    )(zd, zd, zd, zd, zd, zd, p["conv_w"], p["conv_b"], p["wa"], p["ba"], p["wx"], p["bx"], p["lam"])


def _out_kernel(x_ref, gate_ref, hfa_ref, hba_ref, o_ref, gn_ref, yb_ref, yc_ref, hfd_ref, hbd_ref, gd_ref, w_ref, out_ref):
    pair = [(0, ML_DH), (ML_DH, 2 * ML_DH)]
    ha = hfa_ref[...] + hba_ref[...]
    ha = jnp.concatenate([_group_rms(ha[:, :LANE], pair), _group_rms(ha[:, LANE:], pair)], axis=-1) * gn_ref[...]
    ya = (jax.nn.sigmoid(o_ref[...].astype(F32)) * ha).astype(BF16)
    yd = (jax.nn.gelu(gd_ref[...]) * (hfd_ref[...] + hbd_ref[...])).astype(BF16)
    acc = jnp.dot(ya, w_ref[0:ML_W, :], preferred_element_type=F32)
    acc = acc + jnp.dot(yb_ref[...], w_ref[ML_W:2 * ML_W, :], preferred_element_type=F32)
    acc = acc + jnp.dot(yc_ref[...], w_ref[2 * ML_W:3 * ML_W, :], preferred_element_type=F32)
    acc = acc + jnp.dot(yd, w_ref[3 * ML_W:4 * ML_W, :], preferred_element_type=F32)
    out_ref[...] = x_ref[...] + gate_ref[...] * acc


def _out_proj(xa, mod, hfa, hba, za, gn, yb, yc, hfd, hbd, zd, w, nb, ctx_tiles, off):
    b, s, _ = xa.shape
    nt = s // TM - off
    r = lambda w_, col=0: _rows(w_, off, col)
    return pl.pallas_call(
        _out_kernel,
        grid=(b, nt),
        in_specs=[r(D), _mod_spec(2, nb, ctx_tiles, off), r(ML_W), r(ML_W), r(ML_W, 3), _whole((1, ML_W)),
                  r(ML_W), r(ML_W), r(ML_W), r(ML_W), r(ML_W, 1), _whole((D, D))],
        out_specs=_rows(D),
        out_shape=jax.ShapeDtypeStruct((b, nt * TM, D), F32),
        compiler_params=_cp(("parallel", "parallel"), VMEM_LIMIT),
        name="out_proj",
    )(xa, mod, hfa, hba, za, gn, yb, yc, hfd, hbd, zd, w)


def _ffn_kernel(x_ref, g_ref, sh_ref, sc_ref, gate_ref, w13_ref, w2_ref, out_ref, *, d_ff):
    x = x_ref[...]
    h = _rms(x) * g_ref[...]
    hb = (h * (1.0 + sc_ref[...]) + sh_ref[...]).astype(BF16)
    a = jnp.dot(hb, w13_ref[:, :d_ff], preferred_element_type=F32)
    g = jnp.dot(hb, w13_ref[:, d_ff:], preferred_element_type=F32)
    u = (g * jax.nn.sigmoid(g) * a).astype(BF16)
    out_ref[...] = x + gate_ref[...] * jnp.dot(u, w2_ref[...], preferred_element_type=F32)


def _ffn(xa, gain, mod, w13, w2, nb, ctx_tiles):
    b, s, _ = xa.shape
    d_ff = w2.shape[0]
    return pl.pallas_call(
        functools.partial(_ffn_kernel, d_ff=d_ff),
        grid=(b, s // TM),
        in_specs=[_rows(D), _whole((1, D)), _mod_spec(3, nb, ctx_tiles, 0), _mod_spec(4, nb, ctx_tiles, 0),
                  _mod_spec(5, nb, ctx_tiles, 0), _whole((D, 2 * d_ff)), _whole((d_ff, D))],
        out_specs=_rows(D),
        out_shape=jax.ShapeDtypeStruct((b, s, D), F32),
        compiler_params=_cp(("parallel", "parallel"), VMEM_LIMIT),
        name="ffn",
    )(xa, gain.reshape(1, D), mod, mod, mod, w13, w2)


def _router_kernel(x_ref, g_ref, sh_ref, sc_ref, wr_ref, br_ref, hb_ref, cmb_ref):
    h = _rms(x_ref[...]) * g_ref[...]
    h = h * (1.0 + sc_ref[...]) + sh_ref[...]
    hb_ref[...] = h.astype(BF16)
    logits = jnp.dot(h, wr_ref[...], precision=HI, preferred_element_type=F32) + br_ref[...]
    lane = _lane(logits.shape)
    logits = jnp.where(lane < N_EXP, logits, NEG)
    m1 = jnp.max(logits, axis=-1, keepdims=True)
    i1 = jnp.min(jnp.where(logits == m1, lane, LANE), axis=-1, keepdims=True)
    rest = jnp.where(lane == i1, NEG, logits)
    m2 = jnp.max(rest, axis=-1, keepdims=True)
    i2 = jnp.min(jnp.where(rest == m2, lane, LANE), axis=-1, keepdims=True)
    e2 = jnp.exp(m2 - m1)
    inv = 1.0 / (1.0 + e2)
    cmb_ref[...] = jnp.where(lane == i1, inv, 0.0) + jnp.where(lane == i2, e2 * inv, 0.0)


def _router(xl, gain, mod, wr, br):
    b, t, _ = xl.shape
    spec = lambda k: pl.BlockSpec((None, 1, D), lambda bb, j: (bb, 0, k))
    return pl.pallas_call(
        _router_kernel,
        grid=(b, t // TM),
        in_specs=[_rows(D), _whole((1, D)), spec(3), spec(4), _whole((D, LANE)), _whole((1, LANE))],
        out_specs=[_rows(D), _rows(LANE)],
        out_shape=[jax.ShapeDtypeStruct((b, t, D), BF16), jax.ShapeDtypeStruct((b, t, LANE), F32)],
        compiler_params=_cp(("parallel", "parallel"), VMEM_LIMIT),
        name="router",
    )(xl, gain.reshape(1, D), mod, mod, wr, br)


def _moe_kernel(hb_ref, cmb_ref, gate_ref, acc_ref, w13_ref, w2_ref, out_ref, *, d_ff):
    e = pl.program_id(0)
    hb = hb_ref[...]
    a = jnp.dot(hb, w13_ref[:, :d_ff], preferred_element_type=F32)
    g = jnp.dot(hb, w13_ref[:, d_ff:], preferred_element_type=F32)
    u = (g * jax.nn.sigmoid(g) * a).astype(BF16)
    y = jnp.dot(u, w2_ref[...], preferred_element_type=F32)
    cmb = cmb_ref[...]
    cw = jnp.sum(jnp.where(_lane(cmb.shape) == e, cmb, 0.0), axis=-1, keepdims=True)
    out_ref[...] = acc_ref[...] + gate_ref[...] * (cw * y)


def _moe(hb, cmb, mod, xl, w13, w2):
    b, t, _ = xl.shape
    d_ff = w2.shape[1]
    rows = lambda w: pl.BlockSpec((None, TM, w), lambda e, bb, j: (bb, j, 0))
    once = dict(pipeline_mode=pl.Buffered(1))
    return pl.pallas_call(
        functools.partial(_moe_kernel, d_ff=d_ff),
        grid=(N_EXP, b, t // TM),
        in_specs=[rows(D), rows(LANE), pl.BlockSpec((None, 1, D), lambda e, bb, j: (bb, 0, 5)), rows(D),
                  pl.BlockSpec((None, D, 2 * d_ff), lambda e, bb, j: (e, 0, 0), **once),
                  pl.BlockSpec((None, d_ff, D), lambda e, bb, j: (e, 0, 0), **once)],
        out_specs=rows(D),
        out_shape=jax.ShapeDtypeStruct((b, t, D), F32),
        input_output_aliases={3: 0},
        compiler_params=_cp(("arbitrary", "arbitrary", "arbitrary"), VMEM_LIMIT),
        name="moe_dense",
    )(hb, cmb, mod, xl, w13, w2)


def _pad_cols(w, n):
    return jnp.pad(w, ((0, 0), (0, n - w.shape[1])))


def _layout_w_in(w):
    ml = 4 * ML_W
    a, gates = w[:, :ml], w[:, ml:ml + 4 * ML_H]
    o = ml + 4 * ML_H
    cq, ckv, kr = w[:, o:o + MLA_QR], w[:, o + MLA_QR:o + MLA_QR + MLA_KVR], w[:, o + MLA_QR + MLA_KVR:o + MLA_QR + MLA_KVR + MLA_ROPE]
    o += MLA_QR + MLA_KVR + MLA_ROPE
    sw = w[:, o:o + (SW_H + 2 * SW_KV) * SW_DH]
    o += (SW_H + 2 * SW_KV) * SW_DH
    lru = w[:, o:]
    z = lambda n: jnp.zeros((w.shape[0], n), w.dtype)
    cols = [a, _pad_cols(gates, LANE), _pad_cols(cq, 2 * LANE), ckv, z(MLA_NOPE), kr, z(LANE - MLA_NOPE - MLA_ROPE), sw, lru]
    return jnp.concatenate(cols, axis=1).astype(BF16)


IN_SEGS = ((0, 4 * ML_W), (4 * ML_W, LANE), (4 * ML_W + LANE, 4 * LANE), (4 * ML_W + 5 * LANE, 4 * LANE),
           (4 * ML_W + 9 * LANE, 4 * LANE))
IN_DTYPES = (BF16, F32, F32, F32, F32)


def _row(v, n=None):
    v = v.reshape(1, -1)
    return v if n is None else _pad_cols(v, n)


def _layer_params(l, ml_gate_b, ml_out_norm, mla_q_norm, mla_w_uq, mla_kv_norm, mla_w_ukv, mla_q_gain, mla_k_gain,
                  sw_q_gain, sw_k_gain, lru_conv_w, lru_conv_b, lru_wa, lru_ba, lru_wx, lru_bx, lru_lam):
    hq = MLA_NOPE + MLA_ROPE
    wq = mla_w_uq[l].reshape(MLA_QR, MLA_H, hq)
    wq = jnp.pad(wq, ((0, 2 * LANE - MLA_QR), (0, 0), (0, LANE - hq))).reshape(2 * LANE, MLA_H * LANE)
    wkv = mla_w_ukv[l].reshape(MLA_KVR, MLA_H, MLA_NOPE + MLA_V)
    wk = jnp.pad(wkv[:, :, :MLA_NOPE], ((0, 0), (0, 0), (0, LANE - MLA_NOPE))).reshape(MLA_KVR, MLA_H * LANE)
    wv = wkv[:, :, MLA_NOPE:].reshape(MLA_KVR, MLA_H * MLA_V)

    def blockdiag(wb):
        bw = LRU_W // LRU_BLOCKS
        eye = jnp.eye(LRU_BLOCKS, dtype=wb.dtype)
        return jnp.einsum('xncd,nm->xncmd', wb, eye).reshape(2, LRU_W, LRU_W)

    return dict(
        gate_b=_row(ml_gate_b[l], LANE), out_norm=_row(ml_out_norm[l]),
        q_norm=_row(mla_q_norm[l], 2 * LANE), w_uq=wq.astype(BF16), kv_norm=_row(mla_kv_norm[l]),
        w_uk=wk.astype(BF16), w_uv=wv.astype(BF16),
        q_gain=_row(mla_q_gain[l], LANE), k_gain=_row(mla_k_gain[l], LANE),
        sw_q_gain=_row(jnp.tile(sw_q_gain[l], 2)), sw_k_gain=_row(jnp.tile(sw_k_gain[l], 2)),
        conv_w=lru_conv_w[l], conv_b=_row(lru_conv_b[l]),
        wa=blockdiag(lru_wa[l]).astype(BF16), ba=lru_ba[l].reshape(2, 1, LRU_W),
        wx=blockdiag(lru_wx[l]).astype(BF16), bx=lru_bx[l].reshape(2, 1, LRU_W),
        lam=lru_lam[l].reshape(2, 1, LRU_W),
    )


def _mix_layer(xa, mod, gain, w_in_l, w_out_l, sink, p, tabs, lc, nb, ctx_out):
    ctx_tiles = lc // TM
    za, zg, zb, zc, zd = _proj(xa, gain, mod, w_in_l, IN_SEGS, IN_DTYPES, nb, ctx_tiles)
    hfa, hba = _mlstm(za, zg, p["gate_b"], lc)
    qm, km, vm, qs, ks, vs = _prep(zb, zc, tabs[0], tabs[1], p)
    yb = _mla_attn(qm, km, vm, lc)
    pad = ((0, 0), (0, 0), (0, SW_BLOCK), (0, 0))
    yc = _swa_attn(qs, jnp.pad(ks, pad), jnp.pad(vs, pad), sink, lc)
    hfd, hbd = _lru(zd, p, lc)
    off = 0 if ctx_out else ctx_tiles
    return _out_proj(xa, mod, hfa, hba, za, p["out_norm"], yb, yc, hfd, hbd, zd, w_out_l, nb, ctx_tiles, off)


def kernel(x, c, ctx, c_ctx, ada_w, ada_b, norm_mix, norm_ffn, w_in, w_out, ml_gate_b, ml_out_norm, mla_q_norm, mla_w_uq, mla_kv_norm, mla_w_ukv, mla_q_gain, mla_k_gain, sw_q_gain, sw_k_gain, sw_sink, lru_conv_w, lru_conv_b, lru_wa, lru_ba, lru_wx, lru_bx, lru_lam, ffn_w13, ffn_w2, moe_router, moe_router_b, moe_w13, moe_w2):
    nb, t_len, _ = x.shape
    lc = ctx.shape[1]
    depth = ada_w.shape[0]
    assert nb < SUB and lc % TM == 0 and t_len % TM == 0 and t_len % GRID_W == 0
    cc = jnp.zeros((SUB, D), F32).at[:nb].set(c).at[nb].set(c_ctx)
    mods = _ada_mod(cc, ada_w, ada_b)
    tabs = (_rope_tables(t_len, lc, LANE, MLA_ROPE, MLA_NOPE), _rope_tables(t_len, lc, SW_DH, SW_DH, 0))
    xa = jnp.concatenate([ctx, x], axis=1)
    for l in range(depth):
        ctx_out = l < depth - 1
        mod = mods[l].reshape(SUB, 1, 6 * D)
        p = _layer_params(l, ml_gate_b, ml_out_norm, mla_q_norm, mla_w_uq, mla_kv_norm, mla_w_ukv, mla_q_gain,
                          mla_k_gain, sw_q_gain, sw_k_gain, lru_conv_w, lru_conv_b, lru_wa, lru_ba, lru_wx, lru_bx,
                          lru_lam)
        xa = _mix_layer(xa, mod, norm_mix[l], _layout_w_in(w_in[l]), w_out[l].astype(BF16), sw_sink[l], p, tabs,
                        lc, nb, ctx_out)
        if l % 2 == 0:
            xa = _ffn(xa, norm_ffn[l], mod, ffn_w13[l // 2].astype(BF16), ffn_w2[l // 2].astype(BF16), nb,
                      lc // TM if ctx_out else 0)
        else:
            hb, cmb = _router(xa, norm_ffn[l], mod, _pad_cols(moe_router[l // 2], LANE),
                              _row(moe_router_b[l // 2], LANE))
            xa = _moe(hb, cmb, mod, xa, moe_w13[l // 2].astype(BF16), moe_w2[l // 2].astype(BF16))
    return xa
```

```python
import functools

import numpy as np
import jax
import jax.numpy as jnp
from jax import lax
from jax.experimental import pallas as pl
from jax.experimental.pallas import tpu as pltpu

F32 = jnp.float32
BF16 = jnp.bfloat16
HI = lax.Precision.HIGHEST

D = 1024
GRID_W = 64
EPS = 1e-6
ROPE_BASE = 10000.0
ML_H, ML_DH, ML_W, ML_CHUNK = 4, 64, 256, 128
MLA_H, MLA_QR, MLA_KVR, MLA_NOPE, MLA_ROPE, MLA_V = 4, 192, 128, 64, 32, 64
SW_H, SW_KV, SW_DH, SW_WINDOW, SW_BLOCK = 4, 2, 64, 128, 128
LRU_W, LRU_BLOCKS, LRU_C = 256, 4, 8.0
N_EXP = 8

LANE = 128
SUB = 8
TM = 256
NEG = -1e30
VMEM_LIMIT = 56 * 1024 * 1024


def _cp(sem, vmem=None):
    return pltpu.CompilerParams(dimension_semantics=sem, vmem_limit_bytes=vmem)


def _mod_spec(chunk, nb, ctx_tiles, off):
    return pl.BlockSpec((None, 1, D), lambda b, j: (jnp.where(j + off < ctx_tiles, nb, b), 0, chunk))


def _rows(width, off=0, col=0, tm=TM):
    return pl.BlockSpec((None, tm, width), lambda b, j: (b, j + off, col))


def _whole(shape):
    nd = len(shape)
    return pl.BlockSpec(shape, lambda *_: (0,) * nd)


def _rms(x):
    return x * lax.rsqrt(jnp.mean(x * x, axis=-1, keepdims=True) + EPS)


def _lane(shape):
    return lax.broadcasted_iota(jnp.int32, shape, len(shape) - 1)


def _group_rms(blk, bounds):
    lane = _lane(blk.shape)
    sq = blk * blk
    scale = jnp.zeros_like(blk)
    for lo, hi in bounds:
        msk = (lane >= lo) & (lane < hi)
        ms = jnp.sum(jnp.where(msk, sq, 0.0), axis=-1, keepdims=True) * (1.0 / (hi - lo))
        scale = jnp.where(msk, lax.rsqrt(ms + EPS), scale)
    return blk * scale


def _rope(blk, tab_ref, shift):
    n = blk.shape[-1]
    return (blk * tab_ref[0] + pltpu.roll(blk, n - shift, 1) * tab_ref[1]
            + pltpu.roll(blk, shift, 1) * tab_ref[2])


def _rope_tables(t_len, lc, width, dims, offset):
    half, nf = dims // 2, dims // 4
    p = np.arange(dims)
    i = p % half
    f = i % nf
    first = i < nf
    freq = ROPE_BASE ** (-f.astype(np.float64) / nf)
    t = np.arange(t_len)
    pos = np.where(p[None, :] < half, (t // GRID_W)[:, None], (t % GRID_W)[:, None]).astype(np.float64)
    ang = pos * freq[None, :]
    cos, sin = np.cos(ang), np.sin(ang)
    tab = np.zeros((3, lc + t_len, width), np.float64)
    tab[0] = 1.0
    tab[0, lc:, offset:offset + dims] = cos
    tab[1, lc:, offset:offset + dims] = np.where(first[None, :], -sin, 0.0)
    tab[2, lc:, offset:offset + dims] = np.where(first[None, :], 0.0, sin)
    reps = LANE // width
    return jnp.asarray(np.tile(tab, (1, 1, reps)), F32)


def _ada_kernel(c_ref, w_ref, b_ref, o_ref):
    c = c_ref[...]
    s = c * jax.nn.sigmoid(c)
    o_ref[...] = jnp.dot(s, w_ref[...], precision=HI, preferred_element_type=F32) + b_ref[...]


def _ada_mod(cc, ada_w, ada_b):
    depth = ada_w.shape[0]
    n = ada_w.shape[2]
    tn = 1024
    return pl.pallas_call(
        _ada_kernel,
        grid=(depth, n // tn),
        in_specs=[
            pl.BlockSpec((SUB, D), lambda l, j: (0, 0)),
            pl.BlockSpec((None, D, tn), lambda l, j: (l, 0, j)),
            pl.BlockSpec((None, 1, tn), lambda l, j: (l, 0, j)),
        ],
        out_specs=pl.BlockSpec((None, SUB, tn), lambda l, j: (l, 0, j)),
        out_shape=jax.ShapeDtypeStruct((depth, SUB, n), F32),
        compiler_params=_cp(("parallel", "parallel")),
        name="ada_mod",
    )(cc, ada_w, ada_b.reshape(depth, 1, n))


def _proj_kernel(x_ref, g_ref, sh_ref, sc_ref, w_ref, *out_refs, segs):
    h = _rms(x_ref[...]) * g_ref[...]
    h = h * (1.0 + sc_ref[...]) + sh_ref[...]
    hb = h.astype(BF16)
    for (start, width), o_ref in zip(segs, out_refs):
        o_ref[...] = jnp.dot(hb, w_ref[:, start:start + width], preferred_element_type=F32).astype(o_ref.dtype)


def _proj(xa, gain, mod, w, segs, dtypes, nb, ctx_tiles):
    b, s, _ = xa.shape
    n = w.shape[1]
    return pl.pallas_call(
        functools.partial(_proj_kernel, segs=segs),
        grid=(b, s // TM),
        in_specs=[
            _rows(D),
            _whole((1, D)),
            _mod_spec(0, nb, ctx_tiles, 0),
            _mod_spec(1, nb, ctx_tiles, 0),
            _whole((D, n)),
        ],
        out_specs=[_rows(wd) for _, wd in segs],
        out_shape=[jax.ShapeDtypeStruct((b, s, wd), dt) for (_, wd), dt in zip(segs, dtypes)],
        compiler_params=_cp(("parallel", "parallel"), VMEM_LIMIT),
        name="in_proj",
    )(xa, gain.reshape(1, D), mod, mod, w)


def _prep_kernel(zb_ref, zc_ref, tm_ref, ts_ref, qn_ref, wuq_ref, kvn_ref, wk_ref, wv_ref, qg_ref, kg_ref,
                 sqg_ref, skg_ref, qm_ref, km_ref, vm_ref, qs_ref, ks_ref, vs_ref):
    zb = zb_ref[...]
    cq = zb[:, :2 * LANE]
    cqn = cq * lax.rsqrt(jnp.sum(cq * cq, axis=-1, keepdims=True) * (1.0 / MLA_QR) + EPS) * qn_ref[...]
    qf = jnp.dot(cqn.astype(BF16), wuq_ref[...], preferred_element_type=F32)
    q_scale = (MLA_NOPE + MLA_ROPE) ** -0.5 * LOG2E
    head_bounds = [(0, MLA_NOPE), (MLA_NOPE, MLA_NOPE + MLA_ROPE)]
    for h in range(MLA_H):
        blk = _group_rms(qf[:, h * LANE:(h + 1) * LANE], head_bounds) * qg_ref[...]
        qm_ref[h] = (_rope(blk, tm_ref, MLA_ROPE // 4) * q_scale).astype(BF16)
    ckvn = _rms(zb[:, 2 * LANE:3 * LANE]) * kvn_ref[...]
    ckvb = ckvn.astype(BF16)
    kf = jnp.dot(ckvb, wk_ref[...], preferred_element_type=F32)
    vf = jnp.dot(ckvb, wv_ref[...], preferred_element_type=F32)
    kg = kg_ref[...]
    kr = _group_rms(zb[:, 3 * LANE:4 * LANE], head_bounds[1:]) * kg
    kr = _rope(kr, tm_ref, MLA_ROPE // 4)
    for h in range(MLA_H):
        kn = _group_rms(kf[:, h * LANE:(h + 1) * LANE], head_bounds[:1]) * kg
        km_ref[h] = (kn + kr).astype(BF16)
        vm_ref[h] = vf[:, h * MLA_V:(h + 1) * MLA_V].astype(BF16)
    zc = zc_ref[...]
    pair = [(0, SW_DH), (SW_DH, 2 * SW_DH)]
    sw_scale = SW_DH ** -0.5
    for half in range(2):
        blk = _group_rms(zc[:, half * LANE:(half + 1) * LANE], pair) * sqg_ref[...]
        qs_ref[:, half * LANE:(half + 1) * LANE] = (_rope(blk, ts_ref, SW_DH // 4) * sw_scale).astype(BF16)
    kb = _rope(_group_rms(zc[:, 2 * LANE:3 * LANE], pair) * skg_ref[...], ts_ref, SW_DH // 4).astype(BF16)
    vb = zc[:, 3 * LANE:4 * LANE].astype(BF16)
    for j in range(SW_KV):
        ks_ref[j] = kb[:, j * SW_DH:(j + 1) * SW_DH]
        vs_ref[j] = vb[:, j * SW_DH:(j + 1) * SW_DH]


def _prep(zb, zc, tab_mla, tab_sw, p):
    b, s, _ = zb.shape
    tab = pl.BlockSpec((3, TM, LANE), lambda bb, j: (0, j, 0))
    heads = lambda nh, w: pl.BlockSpec((None, nh, TM, w), lambda bb, j: (bb, 0, j, 0))
    return pl.pallas_call(
        _prep_kernel,
        grid=(b, s // TM),
        in_specs=[_rows(4 * LANE), _rows(4 * LANE), tab, tab,
                  _whole((1, 2 * LANE)), _whole((2 * LANE, 4 * LANE)), _whole((1, LANE)),
                  _whole((LANE, 4 * LANE)), _whole((LANE, 2 * LANE)), _whole((1, LANE)), _whole((1, LANE)),
                  _whole((1, LANE)), _whole((1, LANE))],
        out_specs=[heads(MLA_H, LANE), heads(MLA_H, LANE), heads(MLA_H, MLA_V),
                   _rows(2 * LANE), heads(SW_KV, SW_DH), heads(SW_KV, SW_DH)],
        out_shape=[jax.ShapeDtypeStruct((b, MLA_H, s, LANE), BF16),
                   jax.ShapeDtypeStruct((b, MLA_H, s, LANE), BF16),
                   jax.ShapeDtypeStruct((b, MLA_H, s, MLA_V), BF16),
                   jax.ShapeDtypeStruct((b, s, 2 * LANE), BF16),
                   jax.ShapeDtypeStruct((b, SW_KV, s, SW_DH), BF16),
                   jax.ShapeDtypeStruct((b, SW_KV, s, SW_DH), BF16)],
        compiler_params=_cp(("parallel", "parallel"), VMEM_LIMIT),
        name="head_prep",
    )(zb, zc, tab_mla, tab_sw, p["q_norm"], p["w_uq"], p["kv_norm"], p["w_uk"], p["w_uv"], p["q_gain"],
      p["k_gain"], p["sw_q_gain"], p["sw_k_gain"])


MLA_TQ = 256
MLA_KC = 512
LOG2E = 1.4426950408889634


def _mla_kernel(qt_ref, k_ref, vtc_ref, vtl_ref, o_ref, m_ref, l_ref, acc_ref, st_ref, *, lc, n_lat, ctx_tiles):
    i = pl.program_id(1)
    n_it = jnp.where(i < ctx_tiles, 0, n_lat)

    def update(h, st, vt, first):
        cm = jnp.max(st, axis=0, keepdims=True)
        m_new = cm if first else jnp.maximum(m_ref[h], cm)
        p = jnp.exp2(st - m_new)
        ps = jnp.sum(p, axis=0, keepdims=True)
        pv = jnp.dot(vt, p.astype(BF16), preferred_element_type=F32)
        if first:
            l_ref[h], acc_ref[h] = ps, pv
        else:
            alpha = jnp.exp2(m_ref[h] - m_new)
            l_ref[h] = alpha * l_ref[h] + ps
            acc_ref[h] = alpha * acc_ref[h] + pv
        m_ref[h] = m_new

    def scores(c, slot):
        off = pl.multiple_of(lc + c * MLA_KC, LANE)
        for h in range(MLA_H):
            st_ref[slot, h] = jnp.dot(k_ref[h, pl.ds(off, MLA_KC), :], qt_ref[h], preferred_element_type=F32)

    scores(0, 0)
    for h in range(MLA_H):
        update(h, jnp.dot(k_ref[h, 0:lc, :], qt_ref[h], preferred_element_type=F32), vtc_ref[h], True)

    def body(c, carry):
        slot = c & 1
        scores(jnp.minimum(c + 1, n_lat - 1), 1 - slot)
        for h in range(MLA_H):
            update(h, st_ref[slot, h], vtl_ref[h, c], False)
        return carry

    lax.fori_loop(0, n_it, body, 0)
    outs = [(acc_ref[h] / l_ref[h]).T for h in range(MLA_H)]
    o_ref[...] = jnp.concatenate(outs, axis=-1).astype(o_ref.dtype)


def _mla_attn(q, k, v, lc):
    b, h, s, _ = q.shape
    n_lat = (s - lc) // MLA_KC
    qt = jnp.swapaxes(q, 2, 3)
    vtc = jnp.swapaxes(v[:, :, :lc], 2, 3)
    vtl = jnp.swapaxes(v[:, :, lc:].reshape(b, h, n_lat, MLA_KC, MLA_V), 3, 4)
    kern = functools.partial(_mla_kernel, lc=lc, n_lat=n_lat, ctx_tiles=lc // MLA_TQ)
    return pl.pallas_call(
        kern,
        grid=(b, s // MLA_TQ),
        in_specs=[pl.BlockSpec((None, h, LANE, MLA_TQ), lambda bb, i: (bb, 0, 0, i)),
                  pl.BlockSpec((None, h, s, LANE), lambda bb, i: (bb, 0, 0, 0)),
                  pl.BlockSpec((None, h, MLA_V, lc), lambda bb, i: (bb, 0, 0, 0)),
                  pl.BlockSpec((None, h, n_lat, MLA_V, MLA_KC), lambda bb, i: (bb, 0, 0, 0, 0))],
        out_specs=pl.BlockSpec((None, MLA_TQ, h * MLA_V), lambda bb, i: (bb, i, 0)),
        out_shape=jax.ShapeDtypeStruct((b, s, h * MLA_V), BF16),
        scratch_shapes=[pltpu.VMEM((h, 1, MLA_TQ), F32), pltpu.VMEM((h, 1, MLA_TQ), F32),
                        pltpu.VMEM((h, MLA_V, MLA_TQ), F32), pltpu.VMEM((2, h, MLA_KC, MLA_TQ), F32)],
        compiler_params=_cp(("parallel", "arbitrary"), VMEM_LIMIT),
        name="mla_attn",
    )(qt, k, vtc, vtl)


def _swa_kernel(sink_ref, q_ref, k_ref, v_ref, o_ref, *, lc, t_len):
    j = pl.program_id(1)
    n = pl.program_id(2)
    blk = SW_BLOCK
    q2 = q_ref[...]
    qs = jnp.concatenate([q2[:, :SW_DH], q2[:, SW_DH:]], axis=0)
    start = pl.multiple_of(jnp.maximum(n - 1, 0) * blk, blk)
    nw = blk + 2 * SW_WINDOW
    kw = k_ref[pl.ds(start, nw), :]
    vw = v_ref[pl.ds(start, nw), :]
    kc = k_ref[pl.ds(0, lc), :]
    vc = v_ref[pl.ds(0, lc), :]
    dn = (((1,), (1,)), ((), ()))
    s_loc = lax.dot_general(qs, kw, dn, preferred_element_type=F32)
    s_ctx = lax.dot_general(qs, kc, dn, preferred_element_type=F32)
    row = lax.broadcasted_iota(jnp.int32, s_loc.shape, 0)
    col = lax.broadcasted_iota(jnp.int32, s_loc.shape, 1)
    qpos = n * blk - lc + (row & (blk - 1))
    kpos = start - lc + col
    valid = (n * blk >= lc) & (kpos >= 0) & (kpos < t_len) & (jnp.abs(qpos - kpos) <= SW_WINDOW)
    s_loc = jnp.where(valid, s_loc, NEG)
    rows1 = lax.broadcasted_iota(jnp.int32, (2 * blk, 1), 0)
    sink = jnp.where(rows1 < blk, sink_ref[2 * j], sink_ref[2 * j + 1])
    m = jnp.maximum(jnp.maximum(jnp.max(s_loc, axis=-1, keepdims=True), jnp.max(s_ctx, axis=-1, keepdims=True)), sink)
    e_loc = jnp.exp(s_loc - m)
    e_ctx = jnp.exp(s_ctx - m)
    den = jnp.sum(e_loc, axis=-1, keepdims=True) + jnp.sum(e_ctx, axis=-1, keepdims=True) + jnp.exp(sink - m)
    o = (jnp.dot(e_loc.astype(BF16), vw, preferred_element_type=F32)
         + jnp.dot(e_ctx.astype(BF16), vc, preferred_element_type=F32)) / den
    o_ref[...] = jnp.concatenate([o[:blk], o[blk:]], axis=-1).astype(o_ref.dtype)


def _swa_attn(q, k, v, sink, lc):
    b, s, _ = q.shape
    t_len = s - lc
    sp = k.shape[2]
    kern = functools.partial(_swa_kernel, lc=lc, t_len=t_len)
    return pl.pallas_call(
        kern,
        grid=(b, SW_KV, s // SW_BLOCK),
        in_specs=[pl.BlockSpec(memory_space=pltpu.SMEM),
                  pl.BlockSpec((None, SW_BLOCK, LANE), lambda bb, j, n: (bb, n, j)),
                  pl.BlockSpec((None, None, sp, SW_DH), lambda bb, j, n: (bb, j, 0, 0)),
                  pl.BlockSpec((None, None, sp, SW_DH), lambda bb, j, n: (bb, j, 0, 0))],
        out_specs=pl.BlockSpec((None, SW_BLOCK, LANE), lambda bb, j, n: (bb, n, j)),
        out_shape=jax.ShapeDtypeStruct((b, s, 2 * LANE), BF16),
        compiler_params=_cp(("parallel", "parallel", "arbitrary"), VMEM_LIMIT),
        name="swa_attn",
    )(sink, q, k, v)


def _mirror(j, n_ctx, n_all):
    return jnp.where(j < n_ctx, n_ctx - 1 - j, n_all + n_ctx - 1 - j)


def _mlstm_kernel(xf_ref, xb_ref, gf_ref, gb_ref, bias_ref, hf_ref, hb_ref, c_ref, n_ref, m_ref):
    L = ML_CHUNK

    @pl.when(pl.program_id(1) == 0)
    def _():
        c_ref[...] = jnp.zeros_like(c_ref)
        n_ref[...] = jnp.zeros_like(n_ref)
        m_ref[...] = jnp.zeros_like(m_ref)

    row = lax.broadcasted_iota(jnp.int32, (L, L), 0)
    col = lax.broadcasted_iota(jnp.int32, (L, L), 1)
    dn_t = (((1,), (1,)), ((), ()))
    for direction, (x_ref, g_ref, o_ref) in enumerate(((xf_ref, gf_ref, hf_ref), (xb_ref, gb_ref, hb_ref))):
        causal = (col <= row) if direction == 0 else (col >= row)
        tri = causal.astype(F32)
        x = x_ref[...]
        g = g_ref[...] + bias_ref[...]
        gt = g.T
        lf = jax.nn.log_sigmoid(g)
        lft = jax.nn.log_sigmoid(gt)
        bcol = jnp.dot(tri, lf, precision=HI, preferred_element_type=F32)
        brow = lax.dot_general(lft, tri, dn_t, precision=HI, preferred_element_type=F32)
        last = L - 1 if direction == 0 else 0
        outs = []
        for h in range(ML_H):
            chain = direction * ML_H + h
            gi, gf = 2 * ML_H * direction + h, 2 * ML_H * direction + ML_H + h
            ic_col, b_col = g[:, gi:gi + 1], bcol[:, gf:gf + 1]
            ic_row, b_row = gt[gi:gi + 1, :], brow[gf:gf + 1, :]
            q = x[:, h * ML_DH:(h + 1) * ML_DH]
            k = x[:, ML_W + h * ML_DH:ML_W + (h + 1) * ML_DH] * (ML_DH ** -0.5)
            v = x[:, 2 * ML_W + h * ML_DH:2 * ML_W + (h + 1) * ML_DH]
            c_st, n_st, m_st = c_ref[chain], n_ref[chain], m_ref[chain]
            d = jnp.where(causal, b_col + (ic_row - b_row), NEG)
            m_inter = b_col + m_st
            m_t = jnp.maximum(m_inter, jnp.max(d, axis=-1, keepdims=True))
            w_inter = jnp.exp(m_inter - m_t)
            s = lax.dot_general(q, k, dn_t, preferred_element_type=F32) * jnp.exp(d - m_t)
            num = (jnp.dot(s.astype(BF16), v, preferred_element_type=F32)
                   + w_inter * lax.dot_general(q, c_st.astype(BF16), dn_t, preferred_element_type=F32))
            den = (jnp.sum(s, axis=-1, keepdims=True)
                   + w_inter * jnp.sum(q.astype(F32) * n_st, axis=-1, keepdims=True))
            outs.append(num / jnp.maximum(jnp.abs(den), jnp.exp(-m_t)))
            b_last = b_col[last:last + 1, :]
            g_col = b_last - b_col + ic_col
            m_new = jnp.maximum(b_last + m_st, jnp.max(g_col, axis=0, keepdims=True))
            w_col = jnp.exp(g_col - m_new)
            decay = jnp.exp(b_last + m_st - m_new)
            vw = (v.astype(F32) * w_col).astype(BF16)
            c_ref[chain] = decay * c_st + lax.dot_general(vw, k, (((0,), (0,)), ((), ())), preferred_element_type=F32)
            n_ref[chain] = decay * n_st + jnp.sum(k.astype(F32) * w_col, axis=0, keepdims=True)
            m_ref[chain] = m_new
        o_ref[...] = jnp.concatenate(outs, axis=-1)


def _mlstm(za, zg, bias, lc):
    b, s, _ = za.shape
    n_all, n_ctx = s // ML_CHUNK, lc // ML_CHUNK
    fwd = lambda w: pl.BlockSpec((None, ML_CHUNK, w), lambda bb, j: (bb, j, 0))
    bwd = lambda w: pl.BlockSpec((None, ML_CHUNK, w), lambda bb, j: (bb, _mirror(j, n_ctx, n_all), 0))
    return pl.pallas_call(
        _mlstm_kernel,
        grid=(b, n_all),
        in_specs=[fwd(4 * ML_W), bwd(4 * ML_W), fwd(LANE), bwd(LANE), _whole((1, LANE))],
        out_specs=[fwd(ML_W), bwd(ML_W)],
        out_shape=[jax.ShapeDtypeStruct((b, s, ML_W), F32)] * 2,
        scratch_shapes=[pltpu.VMEM((2 * ML_H, ML_DH, ML_DH), F32),
                        pltpu.VMEM((2 * ML_H, 1, ML_DH), F32),
                        pltpu.VMEM((2 * ML_H, 1, 1), F32)],
        compiler_params=_cp(("parallel", "arbitrary"), VMEM_LIMIT),
        name="mlstm_scan",
    )(za, za, zg, zg, bias)


LRU_T = 256
LRU_HALO = SUB


def _lru_kernel(uf_ref, pf_ref, nf_ref, ub_ref, pb_ref, nb_ref, cw_ref, cb_ref, wa_ref, ba_ref, wx_ref, bx_ref,
                lam_ref, hf_ref, hb_ref, carry_ref, *, n_ctx, n_all):
    T = LRU_T
    j = pl.program_id(1)

    @pl.when(j == 0)
    def _():
        carry_ref[...] = jnp.zeros_like(carry_ref)

    cw = cw_ref[...]
    row = lax.broadcasted_iota(jnp.int32, (T, LRU_W), 0)
    for direction, (u_ref, p_ref, n_ref, o_ref) in enumerate(((uf_ref, pf_ref, nf_ref, hf_ref),
                                                              (ub_ref, pb_ref, nb_ref, hb_ref))):
        c = j if direction == 0 else _mirror(j, n_ctx, n_all)
        has_prev = ((c != 0) & (c != n_ctx)).astype(F32)
        has_next = ((c != n_ctx - 1) & (c != n_all - 1)).astype(F32)
        ext = jnp.concatenate([p_ref[...] * has_prev, u_ref[...], n_ref[...] * has_next], axis=0)
        n_ext = T + 2 * LRU_HALO
        u = cb_ref[...] + cw[2:3] * ext[LRU_HALO:LRU_HALO + T]
        for tap, sh in ((0, 2), (1, 1), (3, n_ext - 1)):
            u = u + cw[tap:tap + 1] * pltpu.roll(ext, sh, 0)[LRU_HALO:LRU_HALO + T]
        ub = u.astype(BF16)
        r = jax.nn.sigmoid(jnp.dot(ub, wa_ref[direction], preferred_element_type=F32) + ba_ref[direction])
        i = jax.nn.sigmoid(jnp.dot(ub, wx_ref[direction], preferred_element_type=F32) + bx_ref[direction])
        lam = lam_ref[direction]
        log_a = (-LRU_C) * r * jnp.log1p(jnp.exp(-lam))
        a = jnp.exp(log_a)
        bb = jnp.sqrt(1.0 - a * a) * (i * u)
        sh = 1
        while sh < T:
            if direction == 0:
                ok = row >= sh
                a_s, b_s = pltpu.roll(a, sh, 0), pltpu.roll(bb, sh, 0)
            else:
                ok = row < T - sh
                a_s, b_s = pltpu.roll(a, T - sh, 0), pltpu.roll(bb, T - sh, 0)
            bb = jnp.where(ok, a * b_s + bb, bb)
            a = jnp.where(ok, a * a_s, a)
            sh *= 2
        hcur = bb + a * carry_ref[direction]
        o_ref[...] = hcur
        last = T - 1 if direction == 0 else 0
        carry_ref[direction] = hcur[last:last + 1, :]


def _lru(zd, p, lc):
    b, s, _ = zd.shape
    n_all, n_ctx = s // LRU_T, lc // LRU_T
    per = LRU_T // LRU_HALO
    n_halo = s // LRU_HALO
    ident = lambda j: j
    mirr = lambda j: _mirror(j, n_ctx, n_all)

    def specs(cf):
        return [pl.BlockSpec((None, LRU_T, LRU_W), lambda bb, j: (bb, cf(j), 0)),
                pl.BlockSpec((None, LRU_HALO, LRU_W), lambda bb, j: (bb, jnp.maximum(cf(j) * per - 1, 0), 0)),
                pl.BlockSpec((None, LRU_HALO, LRU_W), lambda bb, j: (bb, jnp.minimum((cf(j) + 1) * per, n_halo - 1), 0))]

    vec = _whole((2, 1, LRU_W))
    mat = _whole((2, LRU_W, LRU_W))
    return pl.pallas_call(
        functools.partial(_lru_kernel, n_ctx=n_ctx, n_all=n_all),
        grid=(b, n_all),
        in_specs=specs(ident) + specs(mirr) + [_whole((4, LRU_W)), _whole((1, LRU_W)), mat, vec, mat, vec, vec],
        out_specs=[pl.BlockSpec((None, LRU_T, LRU_W), lambda bb, j: (bb, j, 0)),
                   pl.BlockSpec((None, LRU_T, LRU_W), lambda bb, j: (bb, mirr(j), 0))],
        out_shape=[jax.ShapeDtypeStruct((b, s, LRU_W), F32)] * 2,
        scratch_shapes=[pltpu.VMEM((2, 1, LRU_W), F32)],
        compiler_params=_cp(("parallel", "arbitrary"), VMEM_LIMIT),
        name="rglru_scan",
    )(zd, zd, zd, zd, zd, zd, p["conv_w"], p["conv_b"], p["wa"], p["ba"], p["wx"], p["bx"], p["lam"])


def _out_kernel(x_ref, gate_ref, hfa_ref, hba_ref, o_ref, gn_ref, yb_ref, yc_ref, hfd_ref, hbd_ref, gd_ref, w_ref, out_ref):
    pair = [(0, ML_DH), (ML_DH, 2 * ML_DH)]
    ha = hfa_ref[...] + hba_ref[...]
    ha = jnp.concatenate([_group_rms(ha[:, :LANE], pair), _group_rms(ha[:, LANE:], pair)], axis=-1) * gn_ref[...]
    ya = (jax.nn.sigmoid(o_ref[...].astype(F32)) * ha).astype(BF16)
    yd = (jax.nn.gelu(gd_ref[...]) * (hfd_ref[...] + hbd_ref[...])).astype(BF16)
    acc = jnp.dot(ya, w_ref[0:ML_W, :], preferred_element_type=F32)
    acc = acc + jnp.dot(yb_ref[...], w_ref[ML_W:2 * ML_W, :], preferred_element_type=F32)
    acc = acc + jnp.dot(yc_ref[...], w_ref[2 * ML_W:3 * ML_W, :], preferred_element_type=F32)
    acc = acc + jnp.dot(yd, w_ref[3 * ML_W:4 * ML_W, :], preferred_element_type=F32)
    out_ref[...] = x_ref[...] + gate_ref[...] * acc


def _out_proj(xa, mod, hfa, hba, za, gn, yb, yc, hfd, hbd, zd, w, nb, ctx_tiles, off):
    b, s, _ = xa.shape
    nt = s // TM - off
    r = lambda w_, col=0: _rows(w_, off, col)
    return pl.pallas_call(
        _out_kernel,
        grid=(b, nt),
        in_specs=[r(D), _mod_spec(2, nb, ctx_tiles, off), r(ML_W), r(ML_W), r(ML_W, 3), _whole((1, ML_W)),
                  r(ML_W), r(ML_W), r(ML_W), r(ML_W), r(ML_W, 1), _whole((D, D))],
        out_specs=_rows(D),
        out_shape=jax.ShapeDtypeStruct((b, nt * TM, D), F32),
        compiler_params=_cp(("parallel", "parallel"), VMEM_LIMIT),
        name="out_proj",
    )(xa, mod, hfa, hba, za, gn, yb, yc, hfd, hbd, zd, w)


def _ffn_kernel(x_ref, g_ref, sh_ref, sc_ref, gate_ref, w13_ref, w2_ref, out_ref, *, d_ff):
    x = x_ref[...]
    h = _rms(x) * g_ref[...]
    hb = (h * (1.0 + sc_ref[...]) + sh_ref[...]).astype(BF16)
    a = jnp.dot(hb, w13_ref[:, :d_ff], preferred_element_type=F32)
    g = jnp.dot(hb, w13_ref[:, d_ff:], preferred_element_type=F32)
    u = (g * jax.nn.sigmoid(g) * a).astype(BF16)
    out_ref[...] = x + gate_ref[...] * jnp.dot(u, w2_ref[...], preferred_element_type=F32)


def _ffn(xa, gain, mod, w13, w2, nb, ctx_tiles):
    b, s, _ = xa.shape
    d_ff = w2.shape[0]
    return pl.pallas_call(
        functools.partial(_ffn_kernel, d_ff=d_ff),
        grid=(b, s // TM),
        in_specs=[_rows(D), _whole((1, D)), _mod_spec(3, nb, ctx_tiles, 0), _mod_spec(4, nb, ctx_tiles, 0),
                  _mod_spec(5, nb, ctx_tiles, 0), _whole((D, 2 * d_ff)), _whole((d_ff, D))],
        out_specs=_rows(D),
        out_shape=jax.ShapeDtypeStruct((b, s, D), F32),
        compiler_params=_cp(("parallel", "parallel"), VMEM_LIMIT),
        name="ffn",
    )(xa, gain.reshape(1, D), mod, mod, mod, w13, w2)


META_E, META_RANK, META_GATE = 0, 2, 4


def _router_kernel(x_ref, g_ref, sh_ref, sc_ref, wr_ref, br_ref, h_ref, meta_ref, cnt_ref):
    @pl.when((pl.program_id(0) == 0) & (pl.program_id(1) == 0))
    def _():
        cnt_ref[...] = jnp.zeros_like(cnt_ref)

    h = _rms(x_ref[...]) * g_ref[...]
    h = h * (1.0 + sc_ref[...]) + sh_ref[...]
    h_ref[...] = h
    logits = jnp.dot(h, wr_ref[...], precision=HI, preferred_element_type=F32) + br_ref[...]
    lane = _lane(logits.shape)
    logits = jnp.where(lane < N_EXP, logits, NEG)
    m1 = jnp.max(logits, axis=-1, keepdims=True)
    i1 = jnp.min(jnp.where(logits == m1, lane, LANE), axis=-1, keepdims=True)
    rest = jnp.where(lane == i1, NEG, logits)
    m2 = jnp.max(rest, axis=-1, keepdims=True)
    i2 = jnp.min(jnp.where(rest == m2, lane, LANE), axis=-1, keepdims=True)
    e2 = jnp.exp(m2 - m1)
    inv = 1.0 / (1.0 + e2)
    hit1, hit2 = lane == i1, lane == i2
    assign = (hit1 | hit2).astype(F32)
    rr = lax.broadcasted_iota(jnp.int32, (TM, TM), 0)
    cc = lax.broadcasted_iota(jnp.int32, (TM, TM), 1)
    before = jnp.dot((cc < rr).astype(BF16), assign.astype(BF16), preferred_element_type=F32) + cnt_ref[...]
    r1 = jnp.sum(jnp.where(hit1, before, 0.0), axis=-1, keepdims=True)
    r2 = jnp.sum(jnp.where(hit2, before, 0.0), axis=-1, keepdims=True)
    cnt_ref[...] = cnt_ref[...] + jnp.sum(assign, axis=0, keepdims=True)
    fields = (i1.astype(F32), i2.astype(F32), r1, r2, inv, e2 * inv)
    meta = jnp.zeros(logits.shape, F32)
    for k, val in enumerate(fields):
        meta = jnp.where(lane == k, val, meta)
    meta_ref[...] = meta


def _router(xl, gain, mod, wr, br):
    b, t, _ = xl.shape
    spec = lambda k: pl.BlockSpec((None, 1, D), lambda bb, j: (bb, 0, k))
    return pl.pallas_call(
        _router_kernel,
        grid=(b, t // TM),
        in_specs=[_rows(D), _whole((1, D)), spec(3), spec(4), _whole((D, LANE)), _whole((1, LANE))],
        out_specs=[_rows(D), _rows(LANE), _whole((1, LANE))],
        out_shape=[jax.ShapeDtypeStruct((b, t, D), F32), jax.ShapeDtypeStruct((b, t, LANE), F32),
                   jax.ShapeDtypeStruct((1, LANE), F32)],
        compiler_params=_cp(("arbitrary", "arbitrary"), VMEM_LIMIT),
        name="router",
    )(xl, gain.reshape(1, D), mod, mod, wr, br)


TG = 256
TOP_K = 2


def _row_dmas(route_ref, start_ref, make_copy, sem):
    def issue(r, c):
        for k in range(TOP_K):
            pos = start_ref[route_ref[0, (META_E + k) * TM + r]] + route_ref[0, (META_RANK + k) * TM + r]
            make_copy(r, k, pos, sem).start()
        return c

    def drain(r, c):
        for k in range(TOP_K):
            make_copy(0, k, 0, sem).wait()
        return c

    lax.fori_loop(0, TM, issue, 0)
    lax.fori_loop(0, TM, drain, 0)


def _dispatch_kernel(route_ref, start_ref, h_ref, init_ref, xs_ref, sem):
    del init_ref
    _row_dmas(route_ref, start_ref,
              lambda r, k, pos, s: pltpu.make_async_copy(h_ref.at[pl.ds(r, 1)], xs_ref.at[pl.ds(pos, 1)], s), sem)


def _dispatch(h2, route, start, n_rows):
    n = h2.shape[0]
    return pl.pallas_call(
        _dispatch_kernel,
        grid=(n // TM,),
        in_specs=[pl.BlockSpec((None, 1, 2 * TOP_K * TM), lambda i: (i, 0, 0), memory_space=pltpu.SMEM),
                  pl.BlockSpec(memory_space=pltpu.SMEM),
                  pl.BlockSpec((TM, D), lambda i: (i, 0)),
                  pl.BlockSpec(memory_space=pl.ANY)],
        out_specs=pl.BlockSpec(memory_space=pl.ANY),
        out_shape=jax.ShapeDtypeStruct((n_rows, D), F32),
        scratch_shapes=[pltpu.SemaphoreType.DMA(())],
        input_output_aliases={3: 0},
        compiler_params=_cp(("arbitrary",), VMEM_LIMIT),
        name="moe_dispatch",
    )(route, start, h2, jnp.zeros((n_rows, D), F32))


def _expert_kernel(te_ref, nu_ref, xs_ref, w13_ref, w2_ref, y_ref, *, d_ff):
    del te_ref
    t = pl.program_id(0)

    @pl.when(t < nu_ref[0])
    def _():
        xb = xs_ref[...].astype(BF16)
        a = jnp.dot(xb, w13_ref[:, :d_ff], preferred_element_type=F32)
        g = jnp.dot(xb, w13_ref[:, d_ff:], preferred_element_type=F32)
        u = (g * jax.nn.sigmoid(g) * a).astype(BF16)
        y_ref[...] = jnp.dot(u, w2_ref[...], preferred_element_type=F32)

    @pl.when(t >= nu_ref[0])
    def _():
        y_ref[...] = jnp.zeros_like(y_ref)


def _experts(xs, tile_expert, n_used, w13, w2):
    n_rows = xs.shape[0]
    d_ff = w2.shape[1]
    once = dict(pipeline_mode=pl.Buffered(1))
    grid_spec = pltpu.PrefetchScalarGridSpec(
        num_scalar_prefetch=2,
        grid=(n_rows // TG,),
        in_specs=[pl.BlockSpec((TG, D), lambda t, te, nu: (t, 0)),
                  pl.BlockSpec((None, D, 2 * d_ff), lambda t, te, nu: (te[t], 0, 0), **once),
                  pl.BlockSpec((None, d_ff, D), lambda t, te, nu: (te[t], 0, 0), **once)],
        out_specs=pl.BlockSpec((TG, D), lambda t, te, nu: (t, 0)),
    )
    return pl.pallas_call(
        functools.partial(_expert_kernel, d_ff=d_ff),
        grid_spec=grid_spec,
        out_shape=jax.ShapeDtypeStruct((n_rows, D), F32),
        compiler_params=_cp(("arbitrary",), VMEM_LIMIT),
        name="moe_experts",
    )(tile_expert, n_used, xs, w13, w2)


def _combine_kernel(route_ref, start_ref, x_ref, meta_ref, gate_ref, y_ref, out_ref, buf_ref, sem):
    _row_dmas(route_ref, start_ref,
              lambda r, k, pos, s: pltpu.make_async_copy(y_ref.at[pl.ds(pos, 1)], buf_ref.at[k, pl.ds(r, 1)], s), sem)
    meta = meta_ref[...]
    mix = meta[:, META_GATE:META_GATE + 1] * buf_ref[0] + meta[:, META_GATE + 1:META_GATE + 2] * buf_ref[1]
    out_ref[...] = x_ref[...] + gate_ref[...] * mix


def _combine(xl, meta, mod, y, route, start):
    b, t, _ = xl.shape
    tiles = t // TM
    return pl.pallas_call(
        _combine_kernel,
        grid=(b, tiles),
        in_specs=[pl.BlockSpec((None, 1, 2 * TOP_K * TM), lambda bb, j: (bb * tiles + j, 0, 0), memory_space=pltpu.SMEM),
                  pl.BlockSpec(memory_space=pltpu.SMEM),
                  _rows(D), _rows(LANE), pl.BlockSpec((None, 1, D), lambda bb, j: (bb, 0, 5)),
                  pl.BlockSpec(memory_space=pl.ANY)],
        out_specs=_rows(D),
        out_shape=jax.ShapeDtypeStruct((b, t, D), F32),
        scratch_shapes=[pltpu.VMEM((TOP_K, TM, D), F32), pltpu.SemaphoreType.DMA(())],
        compiler_params=_cp(("arbitrary", "arbitrary"), VMEM_LIMIT),
        name="moe_combine",
    )(route, start, xl, meta, mod, y)


def _moe(xl, gain, mod, wr, br, w13, w2):
    b, t, _ = xl.shape
    n = b * t
    h, meta, cnt = _router(xl, gain, mod, wr, br)
    counts = cnt[0, :N_EXP].astype(jnp.int32)
    padded = (counts + TG - 1) // TG * TG
    ends = jnp.cumsum(padded)
    start = ends - padded
    n_tiles = TOP_K * n // TG + N_EXP
    tile_lo = jnp.arange(n_tiles, dtype=jnp.int32) * TG
    tile_expert = jnp.minimum(jnp.sum((tile_lo[:, None] >= ends[None, :]).astype(jnp.int32), axis=1), N_EXP - 1)
    n_used = (ends[-1:] // TG).astype(jnp.int32)
    ids = meta.reshape(n // TM, TM, LANE)[:, :, :2 * TOP_K].astype(jnp.int32)
    route = jnp.swapaxes(ids, 1, 2).reshape(n // TM, 1, 2 * TOP_K * TM)
    xs = _dispatch(h.reshape(n, D), route, start, n_tiles * TG)
    y = _experts(xs, tile_expert, n_used, w13, w2)
    return _combine(xl, meta, mod, y, route, start)


def _pad_cols(w, n):
    return jnp.pad(w, ((0, 0), (0, n - w.shape[1])))


def _layout_w_in(w):
    ml = 4 * ML_W
    a, gates = w[:, :ml], w[:, ml:ml + 4 * ML_H]
    o = ml + 4 * ML_H
    cq, ckv, kr = w[:, o:o + MLA_QR], w[:, o + MLA_QR:o + MLA_QR + MLA_KVR], w[:, o + MLA_QR + MLA_KVR:o + MLA_QR + MLA_KVR + MLA_ROPE]
    o += MLA_QR + MLA_KVR + MLA_ROPE
    sw = w[:, o:o + (SW_H + 2 * SW_KV) * SW_DH]
    o += (SW_H + 2 * SW_KV) * SW_DH
    lru = w[:, o:]
    z = lambda n: jnp.zeros((w.shape[0], n), w.dtype)
    cols = [a, _pad_cols(gates, LANE), _pad_cols(cq, 2 * LANE), ckv, z(MLA_NOPE), kr, z(LANE - MLA_NOPE - MLA_ROPE), sw, lru]
    return jnp.concatenate(cols, axis=1).astype(BF16)


IN_SEGS = ((0, 4 * ML_W), (4 * ML_W, LANE), (4 * ML_W + LANE, 4 * LANE), (4 * ML_W + 5 * LANE, 4 * LANE),
           (4 * ML_W + 9 * LANE, 4 * LANE))
IN_DTYPES = (BF16, F32, F32, F32, F32)


def _row(v, n=None):
    v = v.reshape(1, -1)
    return v if n is None else _pad_cols(v, n)


def _layer_params(l, ml_gate_b, ml_out_norm, mla_q_norm, mla_w_uq, mla_kv_norm, mla_w_ukv, mla_q_gain, mla_k_gain,
                  sw_q_gain, sw_k_gain, lru_conv_w, lru_conv_b, lru_wa, lru_ba, lru_wx, lru_bx, lru_lam):
    hq = MLA_NOPE + MLA_ROPE
    wq = mla_w_uq[l].reshape(MLA_QR, MLA_H, hq)
    wq = jnp.pad(wq, ((0, 2 * LANE - MLA_QR), (0, 0), (0, LANE - hq))).reshape(2 * LANE, MLA_H * LANE)
    wkv = mla_w_ukv[l].reshape(MLA_KVR, MLA_H, MLA_NOPE + MLA_V)
    wk = jnp.pad(wkv[:, :, :MLA_NOPE], ((0, 0), (0, 0), (0, LANE - MLA_NOPE))).reshape(MLA_KVR, MLA_H * LANE)
    wv = wkv[:, :, MLA_NOPE:].reshape(MLA_KVR, MLA_H * MLA_V)

    def blockdiag(wb):
        bw = LRU_W // LRU_BLOCKS
        eye = jnp.eye(LRU_BLOCKS, dtype=wb.dtype)
        return jnp.einsum('xncd,nm->xncmd', wb, eye).reshape(2, LRU_W, LRU_W)

    return dict(
        gate_b=_row(ml_gate_b[l], LANE), out_norm=_row(ml_out_norm[l]),
        q_norm=_row(mla_q_norm[l], 2 * LANE), w_uq=wq.astype(BF16), kv_norm=_row(mla_kv_norm[l]),
        w_uk=wk.astype(BF16), w_uv=wv.astype(BF16),
        q_gain=_row(mla_q_gain[l], LANE), k_gain=_row(mla_k_gain[l], LANE),
        sw_q_gain=_row(jnp.tile(sw_q_gain[l], 2)), sw_k_gain=_row(jnp.tile(sw_k_gain[l], 2)),
        conv_w=lru_conv_w[l], conv_b=_row(lru_conv_b[l]),
        wa=blockdiag(lru_wa[l]).astype(BF16), ba=lru_ba[l].reshape(2, 1, LRU_W),
        wx=blockdiag(lru_wx[l]).astype(BF16), bx=lru_bx[l].reshape(2, 1, LRU_W),
        lam=lru_lam[l].reshape(2, 1, LRU_W),
    )


def _mix_layer(xa, mod, gain, w_in_l, w_out_l, sink, p, tabs, lc, nb, ctx_out):
    ctx_tiles = lc // TM
    za, zg, zb, zc, zd = _proj(xa, gain, mod, w_in_l, IN_SEGS, IN_DTYPES, nb, ctx_tiles)
    hfa, hba = _mlstm(za, zg, p["gate_b"], lc)
    qm, km, vm, qs, ks, vs = _prep(zb, zc, tabs[0], tabs[1], p)
    yb = _mla_attn(qm, km, vm, lc)
    pad = ((0, 0), (0, 0), (0, SW_BLOCK), (0, 0))
    yc = _swa_attn(qs, jnp.pad(ks, pad), jnp.pad(vs, pad), sink, lc)
    hfd, hbd = _lru(zd, p, lc)
    off = 0 if ctx_out else ctx_tiles
    return _out_proj(xa, mod, hfa, hba, za, p["out_norm"], yb, yc, hfd, hbd, zd, w_out_l, nb, ctx_tiles, off)


def kernel(x, c, ctx, c_ctx, ada_w, ada_b, norm_mix, norm_ffn, w_in, w_out, ml_gate_b, ml_out_norm, mla_q_norm, mla_w_uq, mla_kv_norm, mla_w_ukv, mla_q_gain, mla_k_gain, sw_q_gain, sw_k_gain, sw_sink, lru_conv_w, lru_conv_b, lru_wa, lru_ba, lru_wx, lru_bx, lru_lam, ffn_w13, ffn_w2, moe_router, moe_router_b, moe_w13, moe_w2):
    nb, t_len, _ = x.shape
    lc = ctx.shape[1]
    depth = ada_w.shape[0]
    assert nb < SUB and lc % TM == 0 and t_len % TM == 0 and t_len % GRID_W == 0
    cc = jnp.zeros((SUB, D), F32).at[:nb].set(c).at[nb].set(c_ctx)
    mods = _ada_mod(cc, ada_w, ada_b)
    tabs = (_rope_tables(t_len, lc, LANE, MLA_ROPE, MLA_NOPE), _rope_tables(t_len, lc, SW_DH, SW_DH, 0))
    xa = jnp.concatenate([ctx, x], axis=1)
    for l in range(depth):
        ctx_out = l < depth - 1
        mod = mods[l].reshape(SUB, 1, 6 * D)
        p = _layer_params(l, ml_gate_b, ml_out_norm, mla_q_norm, mla_w_uq, mla_kv_norm, mla_w_ukv, mla_q_gain,
                          mla_k_gain, sw_q_gain, sw_k_gain, lru_conv_w, lru_conv_b, lru_wa, lru_ba, lru_wx, lru_bx,
                          lru_lam)
        xa = _mix_layer(xa, mod, norm_mix[l], _layout_w_in(w_in[l]), w_out[l].astype(BF16), sw_sink[l], p, tabs,
                        lc, nb, ctx_out)
        if l % 2 == 0:
            xa = _ffn(xa, norm_ffn[l], mod, ffn_w13[l // 2].astype(BF16), ffn_w2[l // 2].astype(BF16), nb,
                      lc // TM if ctx_out else 0)
        else:
            xa = _moe(xa, norm_ffn[l], mod, _pad_cols(moe_router[l // 2], LANE), _row(moe_router_b[l // 2], LANE),
                      moe_w13[l // 2].astype(BF16), moe_w2[l // 2].astype(BF16))
    return xa
```

```python
import functools

import numpy as np
import jax
import jax.numpy as jnp
from jax import lax
from jax.experimental import pallas as pl
from jax.experimental.pallas import tpu as pltpu

F32 = jnp.float32
BF16 = jnp.bfloat16
HI = lax.Precision.HIGHEST

D = 1024
GRID_W = 64
EPS = 1e-6
ROPE_BASE = 10000.0
ML_H, ML_DH, ML_W, ML_CHUNK = 4, 64, 256, 128
MLA_H, MLA_QR, MLA_KVR, MLA_NOPE, MLA_ROPE, MLA_V = 4, 192, 128, 64, 32, 64
SW_H, SW_KV, SW_DH, SW_WINDOW, SW_BLOCK = 4, 2, 64, 128, 128
LRU_W, LRU_BLOCKS, LRU_C = 256, 4, 8.0
N_EXP = 8

LANE = 128
SUB = 8
TM = 256
NEG = -1e30
VMEM_LIMIT = 56 * 1024 * 1024


def _cp(sem, vmem=None):
    return pltpu.CompilerParams(dimension_semantics=sem, vmem_limit_bytes=vmem)


def _mod_spec(chunk, nb, ctx_tiles, off):
    return pl.BlockSpec((None, 1, D), lambda b, j: (jnp.where(j + off < ctx_tiles, nb, b), 0, chunk))


def _rows(width, off=0, col=0, tm=TM):
    return pl.BlockSpec((None, tm, width), lambda b, j: (b, j + off, col))


def _whole(shape):
    nd = len(shape)
    return pl.BlockSpec(shape, lambda *_: (0,) * nd)


def _rms(x):
    return x * lax.rsqrt(jnp.mean(x * x, axis=-1, keepdims=True) + EPS)


def _lane(shape):
    return lax.broadcasted_iota(jnp.int32, shape, len(shape) - 1)


def _group_rms(blk, bounds):
    lane = _lane(blk.shape)
    sq = blk * blk
    scale = jnp.zeros_like(blk)
    for lo, hi in bounds:
        msk = (lane >= lo) & (lane < hi)
        ms = jnp.sum(jnp.where(msk, sq, 0.0), axis=-1, keepdims=True) * (1.0 / (hi - lo))
        scale = jnp.where(msk, lax.rsqrt(ms + EPS), scale)
    return blk * scale


def _rope(blk, tab_ref, shift):
    n = blk.shape[-1]
    return (blk * tab_ref[0] + pltpu.roll(blk, n - shift, 1) * tab_ref[1]
            + pltpu.roll(blk, shift, 1) * tab_ref[2])


def _rope_tables(t_len, lc, width, dims, offset):
    half, nf = dims // 2, dims // 4
    p = np.arange(dims)
    i = p % half
    f = i % nf
    first = i < nf
    freq = ROPE_BASE ** (-f.astype(np.float64) / nf)
    t = np.arange(t_len)
    pos = np.where(p[None, :] < half, (t // GRID_W)[:, None], (t % GRID_W)[:, None]).astype(np.float64)
    ang = pos * freq[None, :]
    cos, sin = np.cos(ang), np.sin(ang)
    tab = np.zeros((3, lc + t_len, width), np.float64)
    tab[0] = 1.0
    tab[0, lc:, offset:offset + dims] = cos
    tab[1, lc:, offset:offset + dims] = np.where(first[None, :], -sin, 0.0)
    tab[2, lc:, offset:offset + dims] = np.where(first[None, :], 0.0, sin)
    reps = LANE // width
    return jnp.asarray(np.tile(tab, (1, 1, reps)), F32)


def _ada_kernel(c_ref, w_ref, b_ref, o_ref):
    c = c_ref[...]
    s = c * jax.nn.sigmoid(c)
    o_ref[...] = jnp.dot(s, w_ref[...], precision=HI, preferred_element_type=F32) + b_ref[...]


def _ada_mod(cc, ada_w, ada_b):
    depth = ada_w.shape[0]
    n = ada_w.shape[2]
    tn = 1024
    return pl.pallas_call(
        _ada_kernel,
        grid=(depth, n // tn),
        in_specs=[
            pl.BlockSpec((SUB, D), lambda l, j: (0, 0)),
            pl.BlockSpec((None, D, tn), lambda l, j: (l, 0, j)),
            pl.BlockSpec((None, 1, tn), lambda l, j: (l, 0, j)),
        ],
        out_specs=pl.BlockSpec((None, SUB, tn), lambda l, j: (l, 0, j)),
        out_shape=jax.ShapeDtypeStruct((depth, SUB, n), F32),
        compiler_params=_cp(("parallel", "parallel")),
        name="ada_mod",
    )(cc, ada_w, ada_b.reshape(depth, 1, n))


def _proj_kernel(x_ref, g_ref, sh_ref, sc_ref, w_ref, *out_refs, segs):
    h = _rms(x_ref[...]) * g_ref[...]
    h = h * (1.0 + sc_ref[...]) + sh_ref[...]
    hb = h.astype(BF16)
    for (start, width), o_ref in zip(segs, out_refs):
        o_ref[...] = jnp.dot(hb, w_ref[:, start:start + width], preferred_element_type=F32).astype(o_ref.dtype)


def _proj(xa, gain, mod, w, segs, dtypes, nb, ctx_tiles):
    b, s, _ = xa.shape
    n = w.shape[1]
    return pl.pallas_call(
        functools.partial(_proj_kernel, segs=segs),
        grid=(b, s // TM),
        in_specs=[
            _rows(D),
            _whole((1, D)),
            _mod_spec(0, nb, ctx_tiles, 0),
            _mod_spec(1, nb, ctx_tiles, 0),
            _whole((D, n)),
        ],
        out_specs=[_rows(wd) for _, wd in segs],
        out_shape=[jax.ShapeDtypeStruct((b, s, wd), dt) for (_, wd), dt in zip(segs, dtypes)],
        compiler_params=_cp(("parallel", "parallel"), VMEM_LIMIT),
        name="in_proj",
    )(xa, gain.reshape(1, D), mod, mod, w)


def _prep_kernel(zb_ref, zc_ref, tm_ref, ts_ref, qn_ref, wuq_ref, kvn_ref, wk_ref, wv_ref, qg_ref, kg_ref,
                 sqg_ref, skg_ref, qm_ref, km_ref, vm_ref, qs_ref, ks_ref, vs_ref):
    zb = zb_ref[...]
    cq = zb[:, :2 * LANE]
    cqn = cq * lax.rsqrt(jnp.sum(cq * cq, axis=-1, keepdims=True) * (1.0 / MLA_QR) + EPS) * qn_ref[...]
    qf = jnp.dot(cqn.astype(BF16), wuq_ref[...], preferred_element_type=F32)
    q_scale = (MLA_NOPE + MLA_ROPE) ** -0.5 * LOG2E
    head_bounds = [(0, MLA_NOPE), (MLA_NOPE, MLA_NOPE + MLA_ROPE)]
    for h in range(MLA_H):
        blk = _group_rms(qf[:, h * LANE:(h + 1) * LANE], head_bounds) * qg_ref[...]
        qm_ref[h] = (_rope(blk, tm_ref, MLA_ROPE // 4) * q_scale).astype(BF16)
    ckvn = _rms(zb[:, 2 * LANE:3 * LANE]) * kvn_ref[...]
    ckvb = ckvn.astype(BF16)
    kf = jnp.dot(ckvb, wk_ref[...], preferred_element_type=F32)
    vf = jnp.dot(ckvb, wv_ref[...], preferred_element_type=F32)
    kg = kg_ref[...]
    kr = _group_rms(zb[:, 3 * LANE:4 * LANE], head_bounds[1:]) * kg
    kr = _rope(kr, tm_ref, MLA_ROPE // 4)
    for h in range(MLA_H):
        kn = _group_rms(kf[:, h * LANE:(h + 1) * LANE], head_bounds[:1]) * kg
        km_ref[h] = (kn + kr).astype(BF16)
        vm_ref[h] = vf[:, h * MLA_V:(h + 1) * MLA_V].astype(BF16)
    zc = zc_ref[...]
    pair = [(0, SW_DH), (SW_DH, 2 * SW_DH)]
    sw_scale = SW_DH ** -0.5
    for half in range(2):
        blk = _group_rms(zc[:, half * LANE:(half + 1) * LANE], pair) * sqg_ref[...]
        qs_ref[:, half * LANE:(half + 1) * LANE] = (_rope(blk, ts_ref, SW_DH // 4) * sw_scale).astype(BF16)
    kb = _rope(_group_rms(zc[:, 2 * LANE:3 * LANE], pair) * skg_ref[...], ts_ref, SW_DH // 4).astype(BF16)
    vb = zc[:, 3 * LANE:4 * LANE].astype(BF16)
    for j in range(SW_KV):
        ks_ref[j] = kb[:, j * SW_DH:(j + 1) * SW_DH]
        vs_ref[j] = vb[:, j * SW_DH:(j + 1) * SW_DH]


def _prep(zb, zc, tab_mla, tab_sw, p):
    b, s, _ = zb.shape
    tab = pl.BlockSpec((3, TM, LANE), lambda bb, j: (0, j, 0))
    heads = lambda nh, w: pl.BlockSpec((None, nh, TM, w), lambda bb, j: (bb, 0, j, 0))
    return pl.pallas_call(
        _prep_kernel,
        grid=(b, s // TM),
        in_specs=[_rows(4 * LANE), _rows(4 * LANE), tab, tab,
                  _whole((1, 2 * LANE)), _whole((2 * LANE, 4 * LANE)), _whole((1, LANE)),
                  _whole((LANE, 4 * LANE)), _whole((LANE, 2 * LANE)), _whole((1, LANE)), _whole((1, LANE)),
                  _whole((1, LANE)), _whole((1, LANE))],
        out_specs=[heads(MLA_H, LANE), heads(MLA_H, LANE), heads(MLA_H, MLA_V),
                   _rows(2 * LANE), heads(SW_KV, SW_DH), heads(SW_KV, SW_DH)],
        out_shape=[jax.ShapeDtypeStruct((b, MLA_H, s, LANE), BF16),
                   jax.ShapeDtypeStruct((b, MLA_H, s, LANE), BF16),
                   jax.ShapeDtypeStruct((b, MLA_H, s, MLA_V), BF16),
                   jax.ShapeDtypeStruct((b, s, 2 * LANE), BF16),
                   jax.ShapeDtypeStruct((b, SW_KV, s, SW_DH), BF16),
                   jax.ShapeDtypeStruct((b, SW_KV, s, SW_DH), BF16)],
        compiler_params=_cp(("parallel", "parallel"), VMEM_LIMIT),
        name="head_prep",
    )(zb, zc, tab_mla, tab_sw, p["q_norm"], p["w_uq"], p["kv_norm"], p["w_uk"], p["w_uv"], p["q_gain"],
      p["k_gain"], p["sw_q_gain"], p["sw_k_gain"])


MLA_TQ = 256
MLA_KC = 256
LOG2E = 1.4426950408889634


def _mla_kernel(qt_ref, k_ref, vt_ref, o_ref, m_ref, l_ref, acc_ref, st_ref, p_ref, al_ref, *, n_all, n_ctx,
                ctx_tiles):
    i = pl.program_id(1)
    n_kv = jnp.where(i < ctx_tiles, n_ctx, n_all)
    last = n_kv - 1

    def scores(c, slot):
        off = pl.multiple_of(jnp.minimum(c, last) * MLA_KC, MLA_KC)
        for h in range(MLA_H):
            st_ref[slot, h] = jnp.dot(k_ref[h, pl.ds(off, MLA_KC), :], qt_ref[h], preferred_element_type=F32)

    def softmax(slot, first):
        for h in range(MLA_H):
            st = st_ref[slot, h]
            cm = jnp.max(st, axis=0, keepdims=True)
            m_new = cm if first else jnp.maximum(m_ref[h], cm)
            p = jnp.exp2(st - m_new)
            p_ref[slot, h] = p.astype(BF16)
            ps = jnp.sum(p, axis=0, keepdims=True)
            if first:
                l_ref[h] = ps
                al_ref[slot, h] = jnp.ones_like(ps)
            else:
                alpha = jnp.exp2(m_ref[h] - m_new)
                l_ref[h] = alpha * l_ref[h] + ps
                al_ref[slot, h] = alpha
            m_ref[h] = m_new

    def weighted_values(c, slot):
        pv = [jnp.dot(vt_ref[h, c], p_ref[slot, h], preferred_element_type=F32) for h in range(MLA_H)]
        return pv

    def accumulate(pv, slot):
        for h in range(MLA_H):
            acc_ref[h] = al_ref[slot, h] * acc_ref[h] + pv[h]

    acc_ref[...] = jnp.zeros_like(acc_ref)
    scores(0, 0)
    softmax(0, True)
    scores(1, 1)

    def body(j, carry):
        for slot in range(2):
            c = 2 * j + slot
            pv = weighted_values(c, slot)
            scores(c + 2, slot)
            softmax(1 - slot, False)
            accumulate(pv, slot)
        return carry

    lax.fori_loop(0, last // 2, body, 0)
    accumulate(weighted_values(last, 0), 0)
    outs = [(acc_ref[h] / l_ref[h]).T for h in range(MLA_H)]
    o_ref[...] = jnp.concatenate(outs, axis=-1).astype(o_ref.dtype)


def _mla_attn(q, k, v, lc):
    b, h, s, _ = q.shape
    nc = s // MLA_KC
    assert nc % 2 == 1 and (lc // MLA_KC) % 2 == 1
    qt = jnp.swapaxes(q, 2, 3)
    vt = jnp.swapaxes(v.reshape(b, h, nc, MLA_KC, MLA_V), 3, 4)
    kern = functools.partial(_mla_kernel, n_all=nc, n_ctx=lc // MLA_KC, ctx_tiles=lc // MLA_TQ)
    return pl.pallas_call(
        kern,
        grid=(b, s // MLA_TQ),
        in_specs=[pl.BlockSpec((None, h, LANE, MLA_TQ), lambda bb, i: (bb, 0, 0, i)),
                  pl.BlockSpec((None, h, s, LANE), lambda bb, i: (bb, 0, 0, 0)),
                  pl.BlockSpec((None, h, nc, MLA_V, MLA_KC), lambda bb, i: (bb, 0, 0, 0, 0))],
        out_specs=pl.BlockSpec((None, MLA_TQ, h * MLA_V), lambda bb, i: (bb, i, 0)),
        out_shape=jax.ShapeDtypeStruct((b, s, h * MLA_V), BF16),
        scratch_shapes=[pltpu.VMEM((h, 1, MLA_TQ), F32), pltpu.VMEM((h, 1, MLA_TQ), F32),
                        pltpu.VMEM((h, MLA_V, MLA_TQ), F32), pltpu.VMEM((2, h, MLA_KC, MLA_TQ), F32),
                        pltpu.VMEM((2, h, MLA_KC, MLA_TQ), BF16), pltpu.VMEM((2, h, 1, MLA_TQ), F32)],
        compiler_params=_cp(("parallel", "arbitrary"), VMEM_LIMIT),
        name="mla_attn",
    )(qt, k, vt)


def _swa_kernel(sink_ref, q_ref, k_ref, v_ref, o_ref, *, lc, t_len):
    j = pl.program_id(1)
    n = pl.program_id(2)
    blk = SW_BLOCK
    q2 = q_ref[...]
    qs = jnp.concatenate([q2[:, :SW_DH], q2[:, SW_DH:]], axis=0)
    start = pl.multiple_of(jnp.maximum(n - 1, 0) * blk, blk)
    nw = blk + 2 * SW_WINDOW
    kw = k_ref[pl.ds(start, nw), :]
    vw = v_ref[pl.ds(start, nw), :]
    kc = k_ref[pl.ds(0, lc), :]
    vc = v_ref[pl.ds(0, lc), :]
    dn = (((1,), (1,)), ((), ()))
    s_loc = lax.dot_general(qs, kw, dn, preferred_element_type=F32)
    s_ctx = lax.dot_general(qs, kc, dn, preferred_element_type=F32)
    row = lax.broadcasted_iota(jnp.int32, s_loc.shape, 0)
    col = lax.broadcasted_iota(jnp.int32, s_loc.shape, 1)
    qpos = n * blk - lc + (row & (blk - 1))
    kpos = start - lc + col
    valid = (n * blk >= lc) & (kpos >= 0) & (kpos < t_len) & (jnp.abs(qpos - kpos) <= SW_WINDOW)
    s_loc = jnp.where(valid, s_loc, NEG)
    rows1 = lax.broadcasted_iota(jnp.int32, (2 * blk, 1), 0)
    sink = jnp.where(rows1 < blk, sink_ref[2 * j], sink_ref[2 * j + 1])
    m = jnp.maximum(jnp.maximum(jnp.max(s_loc, axis=-1, keepdims=True), jnp.max(s_ctx, axis=-1, keepdims=True)), sink)
    e_loc = jnp.exp(s_loc - m)
    e_ctx = jnp.exp(s_ctx - m)
    den = jnp.sum(e_loc, axis=-1, keepdims=True) + jnp.sum(e_ctx, axis=-1, keepdims=True) + jnp.exp(sink - m)
    o = (jnp.dot(e_loc.astype(BF16), vw, preferred_element_type=F32)
         + jnp.dot(e_ctx.astype(BF16), vc, preferred_element_type=F32)) / den
    o_ref[...] = jnp.concatenate([o[:blk], o[blk:]], axis=-1).astype(o_ref.dtype)


def _swa_attn(q, k, v, sink, lc):
    b, s, _ = q.shape
    t_len = s - lc
    sp = k.shape[2]
    kern = functools.partial(_swa_kernel, lc=lc, t_len=t_len)
    return pl.pallas_call(
        kern,
        grid=(b, SW_KV, s // SW_BLOCK),
        in_specs=[pl.BlockSpec(memory_space=pltpu.SMEM),
                  pl.BlockSpec((None, SW_BLOCK, LANE), lambda bb, j, n: (bb, n, j)),
                  pl.BlockSpec((None, None, sp, SW_DH), lambda bb, j, n: (bb, j, 0, 0)),
                  pl.BlockSpec((None, None, sp, SW_DH), lambda bb, j, n: (bb, j, 0, 0))],
        out_specs=pl.BlockSpec((None, SW_BLOCK, LANE), lambda bb, j, n: (bb, n, j)),
        out_shape=jax.ShapeDtypeStruct((b, s, 2 * LANE), BF16),
        compiler_params=_cp(("parallel", "parallel", "arbitrary"), VMEM_LIMIT),
        name="swa_attn",
    )(sink, q, k, v)


def _mirror(j, n_ctx, n_all):
    return jnp.where(j < n_ctx, n_ctx - 1 - j, n_all + n_ctx - 1 - j)


ML_AUG = ML_DH + SUB


def _mlstm_kernel(xf_ref, qtf_ref, vtf_ref, gf_ref, xb_ref, qtb_ref, vtb_ref, gb_ref, bias_ref, hf_ref, hb_ref,
                  c_ref, m_ref):
    L = ML_CHUNK

    @pl.when(pl.program_id(1) == 0)
    def _():
        c_ref[...] = jnp.zeros_like(c_ref)
        m_ref[...] = jnp.zeros_like(m_ref)

    row = lax.broadcasted_iota(jnp.int32, (L, L), 0)
    col = lax.broadcasted_iota(jnp.int32, (L, L), 1)
    aug_row = lax.broadcasted_iota(jnp.int32, (SUB, L), 0)
    dirs = ((xf_ref, qtf_ref, vtf_ref, gf_ref, hf_ref), (xb_ref, qtb_ref, vtb_ref, gb_ref, hb_ref))
    for direction, (x_ref, qt_ref, vt_ref, g_ref, o_ref) in enumerate(dirs):
        reach = (row <= col) if direction == 0 else (row >= col)
        gt = (g_ref[...] + bias_ref[...]).T
        brow = jnp.dot(jax.nn.log_sigmoid(gt), reach.astype(F32), precision=HI, preferred_element_type=F32)
        ib = gt - pltpu.roll(brow, L - ML_H, 0)
        ib_t = ib.T
        last = L - 1 if direction == 0 else 0
        outs = []
        for h in range(ML_H):
            chain = direction * ML_H + h
            gi, gf = 2 * ML_H * direction + h, 2 * ML_H * direction + ML_H + h
            b_row, ib_row, ib_col = brow[gf:gf + 1, :], ib[gi:gi + 1, :], ib_t[:, gi:gi + 1]
            k = x_ref[:, h * ML_DH:(h + 1) * ML_DH] * (ML_DH ** -0.5)
            qt = qt_ref[h * ML_DH:(h + 1) * ML_DH, :]
            vt = vt_ref[h * ML_DH:(h + 1) * ML_DH, :]
            c_st, m_st = c_ref[chain], m_ref[chain]
            d = jnp.where(reach, b_row + ib_col, NEG)
            m_inter = b_row + m_st
            m_t = jnp.maximum(m_inter, jnp.max(d, axis=0, keepdims=True))
            w_inter = jnp.exp(m_inter - m_t)
            st = jnp.dot(k, qt, preferred_element_type=F32) * jnp.exp(d - m_t)
            cq = jnp.dot(c_st.astype(BF16), qt, preferred_element_type=F32)
            num = jnp.dot(vt, st.astype(BF16), preferred_element_type=F32) + w_inter * cq[:ML_DH]
            den = jnp.sum(st, axis=0, keepdims=True) + w_inter * cq[ML_DH:ML_DH + 1]
            outs.append(num / jnp.maximum(jnp.abs(den), jnp.exp(-m_t)))
            b_last = b_row[:, last:last + 1]
            g_row = b_last + ib_row
            m_new = jnp.maximum(b_last + m_st, jnp.max(g_row, axis=-1, keepdims=True))
            w_row = jnp.exp(g_row - m_new)
            decay = jnp.exp(b_last + m_st - m_new)
            upd = jnp.concatenate([vt.astype(F32) * w_row, jnp.where(aug_row == 0, w_row, 0.0)], axis=0)
            c_ref[chain] = decay * c_st + jnp.dot(upd.astype(BF16), k, preferred_element_type=F32)
            m_ref[chain] = m_new
        o_ref[...] = jnp.concatenate(outs, axis=0).T


def _mlstm(za, zg, bias, lc):
    b, s, _ = za.shape
    n_all, n_ctx = s // ML_CHUNK, lc // ML_CHUNK
    zat = jnp.swapaxes(za, 1, 2)
    ident = lambda j: j
    mirr = lambda j: _mirror(j, n_ctx, n_all)

    def specs(cf):
        rows = lambda w, col: pl.BlockSpec((None, ML_CHUNK, w), lambda bb, j: (bb, cf(j), col))
        feats = lambda blk: pl.BlockSpec((None, ML_W, ML_CHUNK), lambda bb, j: (bb, blk, cf(j)))
        return [rows(ML_W, 1), feats(0), feats(2), rows(LANE, 0)]

    out = lambda cf: pl.BlockSpec((None, ML_CHUNK, ML_W), lambda bb, j: (bb, cf(j), 0))
    return pl.pallas_call(
        _mlstm_kernel,
        grid=(b, n_all),
        in_specs=specs(ident) + specs(mirr) + [_whole((1, LANE))],
        out_specs=[out(ident), out(mirr)],
        out_shape=[jax.ShapeDtypeStruct((b, s, ML_W), F32)] * 2,
        scratch_shapes=[pltpu.VMEM((2 * ML_H, ML_AUG, ML_DH), F32), pltpu.VMEM((2 * ML_H, 1, 1), F32)],
        compiler_params=_cp(("parallel", "arbitrary"), VMEM_LIMIT),
        name="mlstm_scan",
    )(za, zat, zat, zg, za, zat, zat, zg, bias)


LRU_T = 256
LRU_HALO = SUB


def _lru_kernel(uf_ref, pf_ref, nf_ref, ub_ref, pb_ref, nb_ref, cw_ref, cb_ref, wa_ref, ba_ref, wx_ref, bx_ref,
                lam_ref, hf_ref, hb_ref, carry_ref, *, n_ctx, n_all):
    T = LRU_T
    j = pl.program_id(1)

    @pl.when(j == 0)
    def _():
        carry_ref[...] = jnp.zeros_like(carry_ref)

    cw = cw_ref[...]
    row = lax.broadcasted_iota(jnp.int32, (T, LRU_W), 0)
    for direction, (u_ref, p_ref, n_ref, o_ref) in enumerate(((uf_ref, pf_ref, nf_ref, hf_ref),
                                                              (ub_ref, pb_ref, nb_ref, hb_ref))):
        c = j if direction == 0 else _mirror(j, n_ctx, n_all)
        has_prev = ((c != 0) & (c != n_ctx)).astype(F32)
        has_next = ((c != n_ctx - 1) & (c != n_all - 1)).astype(F32)
        ext = jnp.concatenate([p_ref[...] * has_prev, u_ref[...], n_ref[...] * has_next], axis=0)
        n_ext = T + 2 * LRU_HALO
        u = cb_ref[...] + cw[2:3] * ext[LRU_HALO:LRU_HALO + T]
        for tap, sh in ((0, 2), (1, 1), (3, n_ext - 1)):
            u = u + cw[tap:tap + 1] * pltpu.roll(ext, sh, 0)[LRU_HALO:LRU_HALO + T]
        ub = u.astype(BF16)
        r = jax.nn.sigmoid(jnp.dot(ub, wa_ref[direction], preferred_element_type=F32) + ba_ref[direction])
        i = jax.nn.sigmoid(jnp.dot(ub, wx_ref[direction], preferred_element_type=F32) + bx_ref[direction])
        lam = lam_ref[direction]
        log_a = (-LRU_C) * r * jnp.log1p(jnp.exp(-lam))
        a = jnp.exp(log_a)
        bb = jnp.sqrt(1.0 - a * a) * (i * u)
        sh = 1
        while sh < T:
            if direction == 0:
                ok = row >= sh
                a_s, b_s = pltpu.roll(a, sh, 0), pltpu.roll(bb, sh, 0)
            else:
                ok = row < T - sh
                a_s, b_s = pltpu.roll(a, T - sh, 0), pltpu.roll(bb, T - sh, 0)
            bb = jnp.where(ok, a * b_s + bb, bb)
            a = jnp.where(ok, a * a_s, a)
            sh *= 2
        hcur = bb + a * carry_ref[direction]
        o_ref[...] = hcur
        last = T - 1 if direction == 0 else 0
        carry_ref[direction] = hcur[last:last + 1, :]


def _lru(zd, p, lc):
    b, s, _ = zd.shape
    n_all, n_ctx = s // LRU_T, lc // LRU_T
    per = LRU_T // LRU_HALO
    n_halo = s // LRU_HALO
    ident = lambda j: j
    mirr = lambda j: _mirror(j, n_ctx, n_all)

    def specs(cf):
        return [pl.BlockSpec((None, LRU_T, LRU_W), lambda bb, j: (bb, cf(j), 0)),
                pl.BlockSpec((None, LRU_HALO, LRU_W), lambda bb, j: (bb, jnp.maximum(cf(j) * per - 1, 0), 0)),
                pl.BlockSpec((None, LRU_HALO, LRU_W), lambda bb, j: (bb, jnp.minimum((cf(j) + 1) * per, n_halo - 1), 0))]

    vec = _whole((2, 1, LRU_W))
    mat = _whole((2, LRU_W, LRU_W))
    return pl.pallas_call(
        functools.partial(_lru_kernel, n_ctx=n_ctx, n_all=n_all),
        grid=(b, n_all),
        in_specs=specs(ident) + specs(mirr) + [_whole((4, LRU_W)), _whole((1, LRU_W)), mat, vec, mat, vec, vec],
        out_specs=[pl.BlockSpec((None, LRU_T, LRU_W), lambda bb, j: (bb, j, 0)),
                   pl.BlockSpec((None, LRU_T, LRU_W), lambda bb, j: (bb, mirr(j), 0))],
        out_shape=[jax.ShapeDtypeStruct((b, s, LRU_W), F32)] * 2,
        scratch_shapes=[pltpu.VMEM((2, 1, LRU_W), F32)],
        compiler_params=_cp(("parallel", "arbitrary"), VMEM_LIMIT),
        name="rglru_scan",
    )(zd, zd, zd, zd, zd, zd, p["conv_w"], p["conv_b"], p["wa"], p["ba"], p["wx"], p["bx"], p["lam"])


def _out_kernel(x_ref, gate_ref, hfa_ref, hba_ref, o_ref, gn_ref, yb_ref, yc_ref, hfd_ref, hbd_ref, gd_ref, w_ref, out_ref):
    pair = [(0, ML_DH), (ML_DH, 2 * ML_DH)]
    ha = hfa_ref[...] + hba_ref[...]
    ha = jnp.concatenate([_group_rms(ha[:, :LANE], pair), _group_rms(ha[:, LANE:], pair)], axis=-1) * gn_ref[...]
    ya = (jax.nn.sigmoid(o_ref[...].astype(F32)) * ha).astype(BF16)
    yd = (jax.nn.gelu(gd_ref[...]) * (hfd_ref[...] + hbd_ref[...])).astype(BF16)
    acc = jnp.dot(ya, w_ref[0:ML_W, :], preferred_element_type=F32)
    acc = acc + jnp.dot(yb_ref[...], w_ref[ML_W:2 * ML_W, :], preferred_element_type=F32)
    acc = acc + jnp.dot(yc_ref[...], w_ref[2 * ML_W:3 * ML_W, :], preferred_element_type=F32)
    acc = acc + jnp.dot(yd, w_ref[3 * ML_W:4 * ML_W, :], preferred_element_type=F32)
    out_ref[...] = x_ref[...] + gate_ref[...] * acc


def _out_proj(xa, mod, hfa, hba, za, gn, yb, yc, hfd, hbd, zd, w, nb, ctx_tiles, off):
    b, s, _ = xa.shape
    nt = s // TM - off
    r = lambda w_, col=0: _rows(w_, off, col)
    return pl.pallas_call(
        _out_kernel,
        grid=(b, nt),
        in_specs=[r(D), _mod_spec(2, nb, ctx_tiles, off), r(ML_W), r(ML_W), r(ML_W, 3), _whole((1, ML_W)),
                  r(ML_W), r(ML_W), r(ML_W), r(ML_W), r(ML_W, 1), _whole((D, D))],
        out_specs=_rows(D),
        out_shape=jax.ShapeDtypeStruct((b, nt * TM, D), F32),
        compiler_params=_cp(("parallel", "parallel"), VMEM_LIMIT),
        name="out_proj",
    )(xa, mod, hfa, hba, za, gn, yb, yc, hfd, hbd, zd, w)


def _ffn_kernel(x_ref, g_ref, sh_ref, sc_ref, gate_ref, w13_ref, w2_ref, out_ref, *, d_ff):
    x = x_ref[...]
    h = _rms(x) * g_ref[...]
    hb = (h * (1.0 + sc_ref[...]) + sh_ref[...]).astype(BF16)
    a = jnp.dot(hb, w13_ref[:, :d_ff], preferred_element_type=F32)
    g = jnp.dot(hb, w13_ref[:, d_ff:], preferred_element_type=F32)
    u = (g * jax.nn.sigmoid(g) * a).astype(BF16)
    out_ref[...] = x + gate_ref[...] * jnp.dot(u, w2_ref[...], preferred_element_type=F32)


def _ffn(xa, gain, mod, w13, w2, nb, ctx_tiles):
    b, s, _ = xa.shape
    d_ff = w2.shape[0]
    return pl.pallas_call(
        functools.partial(_ffn_kernel, d_ff=d_ff),
        grid=(b, s // TM),
        in_specs=[_rows(D), _whole((1, D)), _mod_spec(3, nb, ctx_tiles, 0), _mod_spec(4, nb, ctx_tiles, 0),
                  _mod_spec(5, nb, ctx_tiles, 0), _whole((D, 2 * d_ff)), _whole((d_ff, D))],
        out_specs=_rows(D),
        out_shape=jax.ShapeDtypeStruct((b, s, D), F32),
        compiler_params=_cp(("parallel", "parallel"), VMEM_LIMIT),
        name="ffn",
    )(xa, gain.reshape(1, D), mod, mod, mod, w13, w2)


META_E, META_RANK, META_GATE = 0, 2, 4


def _router_kernel(x_ref, g_ref, sh_ref, sc_ref, wr_ref, br_ref, h_ref, meta_ref, cnt_ref):
    @pl.when((pl.program_id(0) == 0) & (pl.program_id(1) == 0))
    def _():
        cnt_ref[...] = jnp.zeros_like(cnt_ref)

    h = _rms(x_ref[...]) * g_ref[...]
    h = h * (1.0 + sc_ref[...]) + sh_ref[...]
    h_ref[...] = h
    logits = jnp.dot(h, wr_ref[...], precision=HI, preferred_element_type=F32) + br_ref[...]
    lane = _lane(logits.shape)
    logits = jnp.where(lane < N_EXP, logits, NEG)
    m1 = jnp.max(logits, axis=-1, keepdims=True)
    i1 = jnp.min(jnp.where(logits == m1, lane, LANE), axis=-1, keepdims=True)
    rest = jnp.where(lane == i1, NEG, logits)
    m2 = jnp.max(rest, axis=-1, keepdims=True)
    i2 = jnp.min(jnp.where(rest == m2, lane, LANE), axis=-1, keepdims=True)
    e2 = jnp.exp(m2 - m1)
    inv = 1.0 / (1.0 + e2)
    hit1, hit2 = lane == i1, lane == i2
    assign = (hit1 | hit2).astype(F32)
    rr = lax.broadcasted_iota(jnp.int32, (TM, TM), 0)
    cc = lax.broadcasted_iota(jnp.int32, (TM, TM), 1)
    before = jnp.dot((cc < rr).astype(BF16), assign.astype(BF16), preferred_element_type=F32) + cnt_ref[...]
    r1 = jnp.sum(jnp.where(hit1, before, 0.0), axis=-1, keepdims=True)
    r2 = jnp.sum(jnp.where(hit2, before, 0.0), axis=-1, keepdims=True)
    cnt_ref[...] = cnt_ref[...] + jnp.sum(assign, axis=0, keepdims=True)
    fields = (i1.astype(F32), i2.astype(F32), r1, r2, inv, e2 * inv)
    meta = jnp.zeros(logits.shape, F32)
    for k, val in enumerate(fields):
        meta = jnp.where(lane == k, val, meta)
    meta_ref[...] = meta


def _router(xl, gain, mod, wr, br):
    b, t, _ = xl.shape
    spec = lambda k: pl.BlockSpec((None, 1, D), lambda bb, j: (bb, 0, k))
    return pl.pallas_call(
        _router_kernel,
        grid=(b, t // TM),
        in_specs=[_rows(D), _whole((1, D)), spec(3), spec(4), _whole((D, LANE)), _whole((1, LANE))],
        out_specs=[_rows(D), _rows(LANE), _whole((1, LANE))],
        out_shape=[jax.ShapeDtypeStruct((b, t, D), F32), jax.ShapeDtypeStruct((b, t, LANE), F32),
                   jax.ShapeDtypeStruct((1, LANE), F32)],
        compiler_params=_cp(("arbitrary", "arbitrary"), VMEM_LIMIT),
        name="router",
    )(xl, gain.reshape(1, D), mod, mod, wr, br)


TG = 256
TOP_K = 2


ROW_DMA_UNROLL = 8


def _row_dmas(route_ref, start_ref, make_copy, tile_copies, sem):
    def issue(r, c):
        for k in range(TOP_K):
            pos = start_ref[route_ref[0, (META_E + k) * TM + r]] + route_ref[0, (META_RANK + k) * TM + r]
            make_copy(r, k, pos, sem).start()
        return c

    lax.fori_loop(0, TM, issue, 0, unroll=ROW_DMA_UNROLL)
    for cp in tile_copies:
        cp.wait()


def _dispatch_kernel(route_ref, start_ref, h_ref, init_ref, xs_ref, sem):
    del init_ref
    whole = pltpu.make_async_copy(h_ref, xs_ref.at[pl.ds(0, TM)], sem)
    _row_dmas(route_ref, start_ref,
              lambda r, k, pos, s: pltpu.make_async_copy(h_ref.at[pl.ds(r, 1)], xs_ref.at[pl.ds(pos, 1)], s),
              [whole] * TOP_K, sem)


def _dispatch(h2, route, start, n_rows):
    n = h2.shape[0]
    return pl.pallas_call(
        _dispatch_kernel,
        grid=(n // TM,),
        in_specs=[pl.BlockSpec((None, 1, 2 * TOP_K * TM), lambda i: (i, 0, 0), memory_space=pltpu.SMEM),
                  pl.BlockSpec(memory_space=pltpu.SMEM),
                  pl.BlockSpec((TM, D), lambda i: (i, 0)),
                  pl.BlockSpec(memory_space=pl.ANY)],
        out_specs=pl.BlockSpec(memory_space=pl.ANY),
        out_shape=jax.ShapeDtypeStruct((n_rows, D), F32),
        scratch_shapes=[pltpu.SemaphoreType.DMA(())],
        input_output_aliases={3: 0},
        compiler_params=_cp(("arbitrary",), VMEM_LIMIT),
        name="moe_dispatch",
    )(route, start, h2, jnp.zeros((n_rows, D), F32))


def _expert_kernel(te_ref, nu_ref, xs_ref, w13_ref, w2_ref, y_ref, *, d_ff):
    del te_ref
    t = pl.program_id(0)

    @pl.when(t < nu_ref[0])
    def _():
        xb = xs_ref[...].astype(BF16)
        a = jnp.dot(xb, w13_ref[:, :d_ff], preferred_element_type=F32)
        g = jnp.dot(xb, w13_ref[:, d_ff:], preferred_element_type=F32)
        u = (g * jax.nn.sigmoid(g) * a).astype(BF16)
        y_ref[...] = jnp.dot(u, w2_ref[...], preferred_element_type=F32)

    @pl.when(t >= nu_ref[0])
    def _():
        y_ref[...] = jnp.zeros_like(y_ref)


def _experts(xs, tile_expert, n_used, w13, w2):
    n_rows = xs.shape[0]
    d_ff = w2.shape[1]
    once = dict(pipeline_mode=pl.Buffered(1))
    grid_spec = pltpu.PrefetchScalarGridSpec(
        num_scalar_prefetch=2,
        grid=(n_rows // TG,),
        in_specs=[pl.BlockSpec((TG, D), lambda t, te, nu: (t, 0)),
                  pl.BlockSpec((None, D, 2 * d_ff), lambda t, te, nu: (te[t], 0, 0), **once),
                  pl.BlockSpec((None, d_ff, D), lambda t, te, nu: (te[t], 0, 0), **once)],
        out_specs=pl.BlockSpec((TG, D), lambda t, te, nu: (t, 0)),
    )
    return pl.pallas_call(
        functools.partial(_expert_kernel, d_ff=d_ff),
        grid_spec=grid_spec,
        out_shape=jax.ShapeDtypeStruct((n_rows, D), F32),
        compiler_params=_cp(("arbitrary",), VMEM_LIMIT),
        name="moe_experts",
    )(tile_expert, n_used, xs, w13, w2)


def _combine_kernel(route_ref, start_ref, x_ref, meta_ref, gate_ref, y_ref, out_ref, buf_ref, sem):
    _row_dmas(route_ref, start_ref,
              lambda r, k, pos, s: pltpu.make_async_copy(y_ref.at[pl.ds(pos, 1)], buf_ref.at[k, pl.ds(r, 1)], s),
              [pltpu.make_async_copy(y_ref.at[pl.ds(0, TM)], buf_ref.at[k], sem) for k in range(TOP_K)], sem)
    meta = meta_ref[...]
    mix = meta[:, META_GATE:META_GATE + 1] * buf_ref[0] + meta[:, META_GATE + 1:META_GATE + 2] * buf_ref[1]
    out_ref[...] = x_ref[...] + gate_ref[...] * mix


def _combine(xl, meta, mod, y, route, start):
    b, t, _ = xl.shape
    tiles = t // TM
    return pl.pallas_call(
        _combine_kernel,
        grid=(b, tiles),
        in_specs=[pl.BlockSpec((None, 1, 2 * TOP_K * TM), lambda bb, j: (bb * tiles + j, 0, 0), memory_space=pltpu.SMEM),
                  pl.BlockSpec(memory_space=pltpu.SMEM),
                  _rows(D), _rows(LANE), pl.BlockSpec((None, 1, D), lambda bb, j: (bb, 0, 5)),
                  pl.BlockSpec(memory_space=pl.ANY)],
        out_specs=_rows(D),
        out_shape=jax.ShapeDtypeStruct((b, t, D), F32),
        scratch_shapes=[pltpu.VMEM((TOP_K, TM, D), F32), pltpu.SemaphoreType.DMA(())],
        compiler_params=_cp(("arbitrary", "arbitrary"), VMEM_LIMIT),
        name="moe_combine",
    )(route, start, xl, meta, mod, y)


def _moe(xl, gain, mod, wr, br, w13, w2):
    b, t, _ = xl.shape
    n = b * t
    h, meta, cnt = _router(xl, gain, mod, wr, br)
    counts = cnt[0, :N_EXP].astype(jnp.int32)
    padded = (counts + TG - 1) // TG * TG
    ends = jnp.cumsum(padded)
    start = ends - padded
    n_tiles = TOP_K * n // TG + N_EXP
    tile_lo = jnp.arange(n_tiles, dtype=jnp.int32) * TG
    tile_expert = jnp.minimum(jnp.sum((tile_lo[:, None] >= ends[None, :]).astype(jnp.int32), axis=1), N_EXP - 1)
    n_used = (ends[-1:] // TG).astype(jnp.int32)
    ids = meta.reshape(n // TM, TM, LANE)[:, :, :2 * TOP_K].astype(jnp.int32)
    route = jnp.swapaxes(ids, 1, 2).reshape(n // TM, 1, 2 * TOP_K * TM)
    xs = _dispatch(h.reshape(n, D), route, start, n_tiles * TG)
    y = _experts(xs, tile_expert, n_used, w13, w2)
    return _combine(xl, meta, mod, y, route, start)


def _pad_cols(w, n):
    return jnp.pad(w, ((0, 0), (0, n - w.shape[1])))


def _layout_w_in(w):
    ml = 4 * ML_W
    a, gates = w[:, :ml], w[:, ml:ml + 4 * ML_H]
    o = ml + 4 * ML_H
    cq, ckv, kr = w[:, o:o + MLA_QR], w[:, o + MLA_QR:o + MLA_QR + MLA_KVR], w[:, o + MLA_QR + MLA_KVR:o + MLA_QR + MLA_KVR + MLA_ROPE]
    o += MLA_QR + MLA_KVR + MLA_ROPE
    sw = w[:, o:o + (SW_H + 2 * SW_KV) * SW_DH]
    o += (SW_H + 2 * SW_KV) * SW_DH
    lru = w[:, o:]
    z = lambda n: jnp.zeros((w.shape[0], n), w.dtype)
    cols = [a, _pad_cols(gates, LANE), _pad_cols(cq, 2 * LANE), ckv, z(MLA_NOPE), kr, z(LANE - MLA_NOPE - MLA_ROPE), sw, lru]
    return jnp.concatenate(cols, axis=1).astype(BF16)


IN_SEGS = ((0, 4 * ML_W), (4 * ML_W, LANE), (4 * ML_W + LANE, 4 * LANE), (4 * ML_W + 5 * LANE, 4 * LANE),
           (4 * ML_W + 9 * LANE, 4 * LANE))
IN_DTYPES = (BF16, F32, F32, F32, F32)


def _row(v, n=None):
    v = v.reshape(1, -1)
    return v if n is None else _pad_cols(v, n)


def _layer_params(l, ml_gate_b, ml_out_norm, mla_q_norm, mla_w_uq, mla_kv_norm, mla_w_ukv, mla_q_gain, mla_k_gain,
                  sw_q_gain, sw_k_gain, lru_conv_w, lru_conv_b, lru_wa, lru_ba, lru_wx, lru_bx, lru_lam):
    hq = MLA_NOPE + MLA_ROPE
    wq = mla_w_uq[l].reshape(MLA_QR, MLA_H, hq)
    wq = jnp.pad(wq, ((0, 2 * LANE - MLA_QR), (0, 0), (0, LANE - hq))).reshape(2 * LANE, MLA_H * LANE)
    wkv = mla_w_ukv[l].reshape(MLA_KVR, MLA_H, MLA_NOPE + MLA_V)
    wk = jnp.pad(wkv[:, :, :MLA_NOPE], ((0, 0), (0, 0), (0, LANE - MLA_NOPE))).reshape(MLA_KVR, MLA_H * LANE)
    wv = wkv[:, :, MLA_NOPE:].reshape(MLA_KVR, MLA_H * MLA_V)

    def blockdiag(wb):
        bw = LRU_W // LRU_BLOCKS
        eye = jnp.eye(LRU_BLOCKS, dtype=wb.dtype)
        return jnp.einsum('xncd,nm->xncmd', wb, eye).reshape(2, LRU_W, LRU_W)

    return dict(
        gate_b=_row(ml_gate_b[l], LANE), out_norm=_row(ml_out_norm[l]),
        q_norm=_row(mla_q_norm[l], 2 * LANE), w_uq=wq.astype(BF16), kv_norm=_row(mla_kv_norm[l]),
        w_uk=wk.astype(BF16), w_uv=wv.astype(BF16),
        q_gain=_row(mla_q_gain[l], LANE), k_gain=_row(mla_k_gain[l], LANE),
        sw_q_gain=_row(jnp.tile(sw_q_gain[l], 2)), sw_k_gain=_row(jnp.tile(sw_k_gain[l], 2)),
        conv_w=lru_conv_w[l], conv_b=_row(lru_conv_b[l]),
        wa=blockdiag(lru_wa[l]).astype(BF16), ba=lru_ba[l].reshape(2, 1, LRU_W),
        wx=blockdiag(lru_wx[l]).astype(BF16), bx=lru_bx[l].reshape(2, 1, LRU_W),
        lam=lru_lam[l].reshape(2, 1, LRU_W),
    )


def _mix_layer(xa, mod, gain, w_in_l, w_out_l, sink, p, tabs, lc, nb, ctx_out):
    ctx_tiles = lc // TM
    za, zg, zb, zc, zd = _proj(xa, gain, mod, w_in_l, IN_SEGS, IN_DTYPES, nb, ctx_tiles)
    hfa, hba = _mlstm(za, zg, p["gate_b"], lc)
    qm, km, vm, qs, ks, vs = _prep(zb, zc, tabs[0], tabs[1], p)
    yb = _mla_attn(qm, km, vm, lc)
    pad = ((0, 0), (0, 0), (0, SW_BLOCK), (0, 0))
    yc = _swa_attn(qs, jnp.pad(ks, pad), jnp.pad(vs, pad), sink, lc)
    hfd, hbd = _lru(zd, p, lc)
    off = 0 if ctx_out else ctx_tiles
    return _out_proj(xa, mod, hfa, hba, za, p["out_norm"], yb, yc, hfd, hbd, zd, w_out_l, nb, ctx_tiles, off)


def kernel(x, c, ctx, c_ctx, ada_w, ada_b, norm_mix, norm_ffn, w_in, w_out, ml_gate_b, ml_out_norm, mla_q_norm, mla_w_uq, mla_kv_norm, mla_w_ukv, mla_q_gain, mla_k_gain, sw_q_gain, sw_k_gain, sw_sink, lru_conv_w, lru_conv_b, lru_wa, lru_ba, lru_wx, lru_bx, lru_lam, ffn_w13, ffn_w2, moe_router, moe_router_b, moe_w13, moe_w2):
    nb, t_len, _ = x.shape
    lc = ctx.shape[1]
    depth = ada_w.shape[0]
    assert nb < SUB and lc % TM == 0 and t_len % TM == 0 and t_len % GRID_W == 0
    cc = jnp.zeros((SUB, D), F32).at[:nb].set(c).at[nb].set(c_ctx)
    mods = _ada_mod(cc, ada_w, ada_b)
    tabs = (_rope_tables(t_len, lc, LANE, MLA_ROPE, MLA_NOPE), _rope_tables(t_len, lc, SW_DH, SW_DH, 0))
    xa = jnp.concatenate([ctx, x], axis=1)
    for l in range(depth):
        ctx_out = l < depth - 1
        mod = mods[l].reshape(SUB, 1, 6 * D)
        p = _layer_params(l, ml_gate_b, ml_out_norm, mla_q_norm, mla_w_uq, mla_kv_norm, mla_w_ukv, mla_q_gain,
                          mla_k_gain, sw_q_gain, sw_k_gain, lru_conv_w, lru_conv_b, lru_wa, lru_ba, lru_wx, lru_bx,
                          lru_lam)
        xa = _mix_layer(xa, mod, norm_mix[l], _layout_w_in(w_in[l]), w_out[l].astype(BF16), sw_sink[l], p, tabs,
                        lc, nb, ctx_out)
        if l % 2 == 0:
            xa = _ffn(xa, norm_ffn[l], mod, ffn_w13[l // 2].astype(BF16), ffn_w2[l // 2].astype(BF16), nb,
                      lc // TM if ctx_out else 0)
        else:
            xa = _moe(xa, norm_ffn[l], mod, _pad_cols(moe_router[l // 2], LANE), _row(moe_router_b[l // 2], LANE),
                      moe_w13[l // 2].astype(BF16), moe_w2[l // 2].astype(BF16))
    return xa
```

```python
import functools

import numpy as np
import jax
import jax.numpy as jnp
from jax import lax
from jax.experimental import pallas as pl
from jax.experimental.pallas import tpu as pltpu

F32 = jnp.float32
BF16 = jnp.bfloat16
HI = lax.Precision.HIGHEST

D = 1024
GRID_W = 64
EPS = 1e-6
ROPE_BASE = 10000.0
ML_H, ML_DH, ML_W, ML_CHUNK = 4, 64, 256, 128
MLA_H, MLA_QR, MLA_KVR, MLA_NOPE, MLA_ROPE, MLA_V = 4, 192, 128, 64, 32, 64
SW_H, SW_KV, SW_DH, SW_WINDOW, SW_BLOCK = 4, 2, 64, 128, 128
LRU_W, LRU_BLOCKS, LRU_C = 256, 4, 8.0
N_EXP = 8

LANE = 128
SUB = 8
TM = 256
NEG = -1e30
VMEM_LIMIT = 56 * 1024 * 1024


def _cp(sem, vmem=None):
    return pltpu.CompilerParams(dimension_semantics=sem, vmem_limit_bytes=vmem)


def _mod_spec(chunk, nb, ctx_tiles, off):
    return pl.BlockSpec((None, 1, D), lambda b, j: (jnp.where(j + off < ctx_tiles, nb, b), 0, chunk))


def _rows(width, off=0, col=0, tm=TM):
    return pl.BlockSpec((None, tm, width), lambda b, j: (b, j + off, col))


def _whole(shape):
    nd = len(shape)
    return pl.BlockSpec(shape, lambda *_: (0,) * nd)


def _rms(x):
    return x * lax.rsqrt(jnp.mean(x * x, axis=-1, keepdims=True) + EPS)


def _lane(shape):
    return lax.broadcasted_iota(jnp.int32, shape, len(shape) - 1)


def _group_mean_matrix(n, bounds):
    i = lax.broadcasted_iota(jnp.int32, (n, n), 0)
    j = lax.broadcasted_iota(jnp.int32, (n, n), 1)
    g = jnp.zeros((n, n), F32)
    for lo, hi in bounds:
        g = jnp.where((i >= lo) & (i < hi) & (j >= lo) & (j < hi), 1.0 / (hi - lo), g)
    return g.astype(BF16)


def _group_rms(blk, bounds, gm):
    lane = _lane(blk.shape)
    sq = blk * blk
    hi_part = sq.astype(BF16)
    lo_part = (sq - hi_part.astype(F32)).astype(BF16)
    ms = (jnp.dot(hi_part, gm, preferred_element_type=F32) + jnp.dot(lo_part, gm, preferred_element_type=F32))
    inside = functools.reduce(jnp.logical_or, [(lane >= lo) & (lane < hi) for lo, hi in bounds])
    return blk * jnp.where(inside, lax.rsqrt(ms + EPS), 0.0)


def _rope(blk, tab_ref, shift):
    n = blk.shape[-1]
    return (blk * tab_ref[0] + pltpu.roll(blk, n - shift, 1) * tab_ref[1]
            + pltpu.roll(blk, shift, 1) * tab_ref[2])


def _rope_tables(t_len, lc, width, dims, offset):
    half, nf = dims // 2, dims // 4
    p = np.arange(dims)
    i = p % half
    f = i % nf
    first = i < nf
    freq = ROPE_BASE ** (-f.astype(np.float64) / nf)
    t = np.arange(t_len)
    pos = np.where(p[None, :] < half, (t // GRID_W)[:, None], (t % GRID_W)[:, None]).astype(np.float64)
    ang = pos * freq[None, :]
    cos, sin = np.cos(ang), np.sin(ang)
    tab = np.zeros((3, lc + t_len, width), np.float64)
    tab[0] = 1.0
    tab[0, lc:, offset:offset + dims] = cos
    tab[1, lc:, offset:offset + dims] = np.where(first[None, :], -sin, 0.0)
    tab[2, lc:, offset:offset + dims] = np.where(first[None, :], 0.0, sin)
    reps = LANE // width
    return jnp.asarray(np.tile(tab, (1, 1, reps)), F32)


def _ada_kernel(c_ref, w_ref, b_ref, o_ref):
    c = c_ref[...]
    s = c * jax.nn.sigmoid(c)
    o_ref[...] = jnp.dot(s, w_ref[...], precision=HI, preferred_element_type=F32) + b_ref[...]


def _ada_mod(cc, ada_w, ada_b):
    depth = ada_w.shape[0]
    n = ada_w.shape[2]
    tn = 1024
    return pl.pallas_call(
        _ada_kernel,
        grid=(depth, n // tn),
        in_specs=[
            pl.BlockSpec((SUB, D), lambda l, j: (0, 0)),
            pl.BlockSpec((None, D, tn), lambda l, j: (l, 0, j)),
            pl.BlockSpec((None, 1, tn), lambda l, j: (l, 0, j)),
        ],
        out_specs=pl.BlockSpec((None, SUB, tn), lambda l, j: (l, 0, j)),
        out_shape=jax.ShapeDtypeStruct((depth, SUB, n), F32),
        compiler_params=_cp(("parallel", "parallel")),
        name="ada_mod",
    )(cc, ada_w, ada_b.reshape(depth, 1, n))


def _proj_kernel(x_ref, g_ref, sh_ref, sc_ref, w_ref, *out_refs, segs):
    h = _rms(x_ref[...]) * g_ref[...]
    h = h * (1.0 + sc_ref[...]) + sh_ref[...]
    hb = h.astype(BF16)
    for (start, width), o_ref in zip(segs, out_refs):
        o_ref[...] = jnp.dot(hb, w_ref[:, start:start + width], preferred_element_type=F32).astype(o_ref.dtype)


def _proj(xa, gain, mod, w, segs, dtypes, nb, ctx_tiles):
    b, s, _ = xa.shape
    n = w.shape[1]
    return pl.pallas_call(
        functools.partial(_proj_kernel, segs=segs),
        grid=(b, s // TM),
        in_specs=[
            _rows(D),
            _whole((1, D)),
            _mod_spec(0, nb, ctx_tiles, 0),
            _mod_spec(1, nb, ctx_tiles, 0),
            _whole((D, n)),
        ],
        out_specs=[_rows(wd) for _, wd in segs],
        out_shape=[jax.ShapeDtypeStruct((b, s, wd), dt) for (_, wd), dt in zip(segs, dtypes)],
        compiler_params=_cp(("parallel", "parallel"), VMEM_LIMIT),
        name="in_proj",
    )(xa, gain.reshape(1, D), mod, mod, w)


def _prep_kernel(zb_ref, zc_ref, tm_ref, ts_ref, qn_ref, wuq_ref, kvn_ref, wk_ref, wv_ref, qg_ref, kg_ref,
                 sqg_ref, skg_ref, qm_ref, km_ref, vm_ref, qs_ref, ks_ref, vs_ref):
    zb = zb_ref[...]
    cq = zb[:, :2 * LANE]
    cqn = cq * lax.rsqrt(jnp.sum(cq * cq, axis=-1, keepdims=True) * (1.0 / MLA_QR) + EPS) * qn_ref[...]
    qf = jnp.dot(cqn.astype(BF16), wuq_ref[...], preferred_element_type=F32)
    q_scale = (MLA_NOPE + MLA_ROPE) ** -0.5 * LOG2E
    head_bounds = [(0, MLA_NOPE), (MLA_NOPE, MLA_NOPE + MLA_ROPE)]
    gm_head = _group_mean_matrix(LANE, head_bounds)
    for h in range(MLA_H):
        blk = _group_rms(qf[:, h * LANE:(h + 1) * LANE], head_bounds, gm_head) * qg_ref[...]
        qm_ref[h] = (_rope(blk, tm_ref, MLA_ROPE // 4) * q_scale).astype(BF16)
    ckvn = _rms(zb[:, 2 * LANE:3 * LANE]) * kvn_ref[...]
    ckvb = ckvn.astype(BF16)
    kf = jnp.dot(ckvb, wk_ref[...], preferred_element_type=F32)
    vf = jnp.dot(ckvb, wv_ref[...], preferred_element_type=F32)
    kg = kg_ref[...]
    kr = _group_rms(zb[:, 3 * LANE:4 * LANE], head_bounds[1:], gm_head) * kg
    kr = _rope(kr, tm_ref, MLA_ROPE // 4)
    for h in range(MLA_H):
        kn = _group_rms(kf[:, h * LANE:(h + 1) * LANE], head_bounds[:1], gm_head) * kg
        km_ref[h] = (kn + kr).astype(BF16)
        vm_ref[h] = vf[:, h * MLA_V:(h + 1) * MLA_V].astype(BF16)
    zc = zc_ref[...]
    pair = [(0, SW_DH), (SW_DH, 2 * SW_DH)]
    gm_pair = _group_mean_matrix(LANE, pair)
    sw_scale = SW_DH ** -0.5
    for half in range(2):
        blk = _group_rms(zc[:, half * LANE:(half + 1) * LANE], pair, gm_pair) * sqg_ref[...]
        qs_ref[:, half * LANE:(half + 1) * LANE] = (_rope(blk, ts_ref, SW_DH // 4) * sw_scale).astype(BF16)
    kb = _rope(_group_rms(zc[:, 2 * LANE:3 * LANE], pair, gm_pair) * skg_ref[...], ts_ref, SW_DH // 4).astype(BF16)
    vb = zc[:, 3 * LANE:4 * LANE].astype(BF16)
    for j in range(SW_KV):
        ks_ref[j] = kb[:, j * SW_DH:(j + 1) * SW_DH]
        vs_ref[j] = vb[:, j * SW_DH:(j + 1) * SW_DH]


def _prep(zb, zc, tab_mla, tab_sw, p):
    b, s, _ = zb.shape
    tab = pl.BlockSpec((3, TM, LANE), lambda bb, j: (0, j, 0))
    heads = lambda nh, w: pl.BlockSpec((None, nh, TM, w), lambda bb, j: (bb, 0, j, 0))
    return pl.pallas_call(
        _prep_kernel,
        grid=(b, s // TM),
        in_specs=[_rows(4 * LANE), _rows(4 * LANE), tab, tab,
                  _whole((1, 2 * LANE)), _whole((2 * LANE, 4 * LANE)), _whole((1, LANE)),
                  _whole((LANE, 4 * LANE)), _whole((LANE, 2 * LANE)), _whole((1, LANE)), _whole((1, LANE)),
                  _whole((1, LANE)), _whole((1, LANE))],
        out_specs=[heads(MLA_H, LANE), heads(MLA_H, LANE), heads(MLA_H, MLA_V),
                   _rows(2 * LANE), heads(SW_KV, SW_DH), heads(SW_KV, SW_DH)],
        out_shape=[jax.ShapeDtypeStruct((b, MLA_H, s, LANE), BF16),
                   jax.ShapeDtypeStruct((b, MLA_H, s, LANE), BF16),
                   jax.ShapeDtypeStruct((b, MLA_H, s, MLA_V), BF16),
                   jax.ShapeDtypeStruct((b, s, 2 * LANE), BF16),
                   jax.ShapeDtypeStruct((b, SW_KV, s, SW_DH), BF16),
                   jax.ShapeDtypeStruct((b, SW_KV, s, SW_DH), BF16)],
        compiler_params=_cp(("parallel", "parallel"), VMEM_LIMIT),
        name="head_prep",
    )(zb, zc, tab_mla, tab_sw, p["q_norm"], p["w_uq"], p["kv_norm"], p["w_uk"], p["w_uv"], p["q_gain"],
      p["k_gain"], p["sw_q_gain"], p["sw_k_gain"])


MLA_TQ = 256
MLA_KC = 256
LOG2E = 1.4426950408889634
MLA_VA = MLA_V + 16


def _mla_kernel(qt_ref, k_ref, vt_ref, o_ref, m_ref, acc_ref, st_ref, p_ref, al_ref, *, n_all, n_ctx, ctx_tiles):
    i = pl.program_id(1)
    n_kv = jnp.where(i < ctx_tiles, n_ctx, n_all)
    last = n_kv - 1

    def scores(c, slot):
        off = pl.multiple_of(jnp.minimum(c, last) * MLA_KC, MLA_KC)
        for h in range(MLA_H):
            st_ref[slot, h] = jnp.dot(k_ref[h, pl.ds(off, MLA_KC), :], qt_ref[h], preferred_element_type=F32)

    def softmax(slot, first):
        for h in range(MLA_H):
            st = st_ref[slot, h]
            cm = jnp.max(st, axis=0, keepdims=True)
            m_new = cm if first else jnp.maximum(m_ref[h], cm)
            p_ref[slot, h] = jnp.exp2(st - m_new).astype(BF16)
            al_ref[slot, h] = jnp.ones_like(cm) if first else jnp.exp2(m_ref[h] - m_new)
            m_ref[h] = m_new

    def weighted_values(c, slot):
        pv = [jnp.dot(vt_ref[h, c], p_ref[slot, h], preferred_element_type=F32) for h in range(MLA_H)]
        return pv

    def accumulate(pv, slot):
        for h in range(MLA_H):
            acc_ref[h] = al_ref[slot, h] * acc_ref[h] + pv[h]

    acc_ref[...] = jnp.zeros_like(acc_ref)
    scores(0, 0)
    softmax(0, True)
    scores(1, 1)

    def body(j, carry):
        for slot in range(2):
            c = 2 * j + slot
            pv = weighted_values(c, slot)
            scores(c + 2, slot)
            softmax(1 - slot, False)
            accumulate(pv, slot)
        return carry

    lax.fori_loop(0, last // 2, body, 0)
    accumulate(weighted_values(last, 0), 0)
    outs = [(acc_ref[h, :MLA_V] / acc_ref[h, MLA_V:MLA_V + 1]).T for h in range(MLA_H)]
    o_ref[...] = jnp.concatenate(outs, axis=-1).astype(o_ref.dtype)


def _mla_attn(q, k, v, lc):
    b, h, s, _ = q.shape
    nc = s // MLA_KC
    assert nc % 2 == 1 and (lc // MLA_KC) % 2 == 1
    qt = jnp.swapaxes(q, 2, 3)
    ones = jnp.zeros((b, h, s, MLA_VA - MLA_V), v.dtype).at[..., 0].set(1.0)
    va = jnp.concatenate([v, ones], axis=-1)
    vt = jnp.swapaxes(va.reshape(b, h, nc, MLA_KC, MLA_VA), 3, 4)
    kern = functools.partial(_mla_kernel, n_all=nc, n_ctx=lc // MLA_KC, ctx_tiles=lc // MLA_TQ)
    return pl.pallas_call(
        kern,
        grid=(b, s // MLA_TQ),
        in_specs=[pl.BlockSpec((None, h, LANE, MLA_TQ), lambda bb, i: (bb, 0, 0, i)),
                  pl.BlockSpec((None, h, s, LANE), lambda bb, i: (bb, 0, 0, 0)),
                  pl.BlockSpec((None, h, nc, MLA_VA, MLA_KC), lambda bb, i: (bb, 0, 0, 0, 0))],
        out_specs=pl.BlockSpec((None, MLA_TQ, h * MLA_V), lambda bb, i: (bb, i, 0)),
        out_shape=jax.ShapeDtypeStruct((b, s, h * MLA_V), BF16),
        scratch_shapes=[pltpu.VMEM((h, 1, MLA_TQ), F32),
                        pltpu.VMEM((h, MLA_VA, MLA_TQ), F32), pltpu.VMEM((2, h, MLA_KC, MLA_TQ), F32),
                        pltpu.VMEM((2, h, MLA_KC, MLA_TQ), BF16), pltpu.VMEM((2, h, 1, MLA_TQ), F32)],
        compiler_params=_cp(("parallel", "arbitrary"), VMEM_LIMIT),
        name="mla_attn",
    )(qt, k, vt)


SW_STEP_BLOCKS = 2


def _swa_kernel(sink_ref, q_ref, k_ref, v_ref, o_ref, *, lc, t_len):
    blk = SW_BLOCK
    nw = blk + 2 * SW_WINDOW
    dn = (((1,), (1,)), ((), ()))
    row = lax.broadcasted_iota(jnp.int32, (2 * blk, nw), 0)
    col = lax.broadcasted_iota(jnp.int32, (2 * blk, nw), 1)
    rows1 = lax.broadcasted_iota(jnp.int32, (2 * blk, 1), 0)
    for part in range(SW_STEP_BLOCKS):
        n = pl.program_id(1) * SW_STEP_BLOCKS + part
        start = pl.multiple_of(jnp.maximum(n - 1, 0) * blk, blk)
        qpos = n * blk - lc + (row & (blk - 1))
        kpos = start - lc + col
        valid = (n * blk >= lc) & (kpos >= 0) & (kpos < t_len) & (jnp.abs(qpos - kpos) <= SW_WINDOW)
        outs = []
        for j in range(SW_KV):
            q2 = q_ref[part * blk:(part + 1) * blk, j * LANE:(j + 1) * LANE]
            qs = jnp.concatenate([q2[:, :SW_DH], q2[:, SW_DH:]], axis=0)
            kw = k_ref[j, pl.ds(start, nw), :]
            vw = v_ref[j, pl.ds(start, nw), :]
            kc = k_ref[j, pl.ds(0, lc), :]
            vc = v_ref[j, pl.ds(0, lc), :]
            s_loc = jnp.where(valid, lax.dot_general(qs, kw, dn, preferred_element_type=F32), NEG)
            s_ctx = lax.dot_general(qs, kc, dn, preferred_element_type=F32)
            sink = jnp.where(rows1 < blk, sink_ref[2 * j], sink_ref[2 * j + 1])
            m = jnp.maximum(jnp.maximum(jnp.max(s_loc, axis=-1, keepdims=True),
                                        jnp.max(s_ctx, axis=-1, keepdims=True)), sink)
            e_loc = jnp.exp(s_loc - m)
            e_ctx = jnp.exp(s_ctx - m)
            den = (jnp.sum(e_loc, axis=-1, keepdims=True) + jnp.sum(e_ctx, axis=-1, keepdims=True)
                   + jnp.exp(sink - m))
            o = (jnp.dot(e_loc.astype(BF16), vw, preferred_element_type=F32)
                 + jnp.dot(e_ctx.astype(BF16), vc, preferred_element_type=F32)) / den
            outs += [o[:blk], o[blk:]]
        o_ref[part * blk:(part + 1) * blk, :] = jnp.concatenate(outs, axis=-1).astype(o_ref.dtype)


def _swa_attn(q, k, v, sink, lc):
    b, s, _ = q.shape
    t_len = s - lc
    sp = k.shape[2]
    rows = SW_STEP_BLOCKS * SW_BLOCK
    kern = functools.partial(_swa_kernel, lc=lc, t_len=t_len)
    return pl.pallas_call(
        kern,
        grid=(b, s // rows),
        in_specs=[pl.BlockSpec(memory_space=pltpu.SMEM),
                  pl.BlockSpec((None, rows, SW_KV * LANE), lambda bb, n: (bb, n, 0)),
                  pl.BlockSpec((None, SW_KV, sp, SW_DH), lambda bb, n: (bb, 0, 0, 0)),
                  pl.BlockSpec((None, SW_KV, sp, SW_DH), lambda bb, n: (bb, 0, 0, 0))],
        out_specs=pl.BlockSpec((None, rows, SW_KV * LANE), lambda bb, n: (bb, n, 0)),
        out_shape=jax.ShapeDtypeStruct((b, s, SW_KV * LANE), BF16),
        compiler_params=_cp(("parallel", "arbitrary"), VMEM_LIMIT),
        name="swa_attn",
    )(sink, q, k, v)


def _mirror(j, n_ctx, n_all):
    return jnp.where(j < n_ctx, n_ctx - 1 - j, n_all + n_ctx - 1 - j)


ML_AUG = ML_DH + SUB


def _mlstm_kernel(xf_ref, qtf_ref, vtf_ref, gf_ref, xb_ref, qtb_ref, vtb_ref, gb_ref, bias_ref, hf_ref, hb_ref,
                  c_ref, m_ref):
    L = ML_CHUNK

    @pl.when(pl.program_id(1) == 0)
    def _():
        c_ref[...] = jnp.zeros_like(c_ref)
        m_ref[...] = jnp.zeros_like(m_ref)

    row = lax.broadcasted_iota(jnp.int32, (L, L), 0)
    col = lax.broadcasted_iota(jnp.int32, (L, L), 1)
    aug_row = lax.broadcasted_iota(jnp.int32, (SUB, L), 0)
    dirs = ((xf_ref, qtf_ref, vtf_ref, gf_ref, hf_ref), (xb_ref, qtb_ref, vtb_ref, gb_ref, hb_ref))
    for direction, (x_ref, qt_ref, vt_ref, g_ref, o_ref) in enumerate(dirs):
        reach = (row <= col) if direction == 0 else (row >= col)
        gt = (g_ref[...] + bias_ref[...]).T
        brow = jnp.dot(jax.nn.log_sigmoid(gt), reach.astype(F32), precision=HI, preferred_element_type=F32)
        ib = gt - pltpu.roll(brow, L - ML_H, 0)
        ib_t = ib.T
        last = L - 1 if direction == 0 else 0
        outs = []
        for h in range(ML_H):
            chain = direction * ML_H + h
            gi, gf = 2 * ML_H * direction + h, 2 * ML_H * direction + ML_H + h
            b_row, ib_row, ib_col = brow[gf:gf + 1, :], ib[gi:gi + 1, :], ib_t[:, gi:gi + 1]
            k = x_ref[:, h * ML_DH:(h + 1) * ML_DH] * (ML_DH ** -0.5)
            qt = qt_ref[h * ML_DH:(h + 1) * ML_DH, :]
            vt = vt_ref[h * ML_DH:(h + 1) * ML_DH, :]
            c_st, m_st = c_ref[chain], m_ref[chain]
            d = jnp.where(reach, b_row + ib_col, NEG)
            m_inter = b_row + m_st
            m_t = jnp.maximum(m_inter, jnp.max(d, axis=0, keepdims=True))
            w_inter = jnp.exp(m_inter - m_t)
            st = jnp.dot(k, qt, preferred_element_type=F32) * jnp.exp(d - m_t)
            cq = jnp.dot(c_st.astype(BF16), qt, preferred_element_type=F32)
            num = jnp.dot(vt, st.astype(BF16), preferred_element_type=F32) + w_inter * cq[:ML_DH]
            den = jnp.sum(st, axis=0, keepdims=True) + w_inter * cq[ML_DH:ML_DH + 1]
            outs.append(num / jnp.maximum(jnp.abs(den), jnp.exp(-m_t)))
            b_last = b_row[:, last:last + 1]
            g_row = b_last + ib_row
            m_new = jnp.maximum(b_last + m_st, jnp.max(g_row, axis=-1, keepdims=True))
            w_row = jnp.exp(g_row - m_new)
            decay = jnp.exp(b_last + m_st - m_new)
            upd = jnp.concatenate([vt.astype(F32) * w_row, jnp.where(aug_row == 0, w_row, 0.0)], axis=0)
            c_ref[chain] = decay * c_st + jnp.dot(upd.astype(BF16), k, preferred_element_type=F32)
            m_ref[chain] = m_new
        o_ref[...] = jnp.concatenate(outs, axis=0).T


def _mlstm(za, zg, bias, lc):
    b, s, _ = za.shape
    n_all, n_ctx = s // ML_CHUNK, lc // ML_CHUNK
    zat = jnp.swapaxes(za, 1, 2)
    ident = lambda j: j
    mirr = lambda j: _mirror(j, n_ctx, n_all)

    def specs(cf):
        rows = lambda w, col: pl.BlockSpec((None, ML_CHUNK, w), lambda bb, j: (bb, cf(j), col))
        feats = lambda blk: pl.BlockSpec((None, ML_W, ML_CHUNK), lambda bb, j: (bb, blk, cf(j)))
        return [rows(ML_W, 1), feats(0), feats(2), rows(LANE, 0)]

    out = lambda cf: pl.BlockSpec((None, ML_CHUNK, ML_W), lambda bb, j: (bb, cf(j), 0))
    return pl.pallas_call(
        _mlstm_kernel,
        grid=(b, n_all),
        in_specs=specs(ident) + specs(mirr) + [_whole((1, LANE))],
        out_specs=[out(ident), out(mirr)],
        out_shape=[jax.ShapeDtypeStruct((b, s, ML_W), F32)] * 2,
        scratch_shapes=[pltpu.VMEM((2 * ML_H, ML_AUG, ML_DH), F32), pltpu.VMEM((2 * ML_H, 1, 1), F32)],
        compiler_params=_cp(("parallel", "arbitrary"), VMEM_LIMIT),
        name="mlstm_scan",
    )(za, zat, zat, zg, za, zat, zat, zg, bias)


LRU_T = 256
LRU_HALO = SUB


def _lru_kernel(uf_ref, pf_ref, nf_ref, ub_ref, pb_ref, nb_ref, cw_ref, cb_ref, wa_ref, ba_ref, wx_ref, bx_ref,
                lam_ref, hf_ref, hb_ref, carry_ref, *, n_ctx, n_all):
    T = LRU_T
    j = pl.program_id(1)

    @pl.when(j == 0)
    def _():
        carry_ref[...] = jnp.zeros_like(carry_ref)

    cw = cw_ref[...]
    row = lax.broadcasted_iota(jnp.int32, (T, LRU_W), 0)
    for direction, (u_ref, p_ref, n_ref, o_ref) in enumerate(((uf_ref, pf_ref, nf_ref, hf_ref),
                                                              (ub_ref, pb_ref, nb_ref, hb_ref))):
        c = j if direction == 0 else _mirror(j, n_ctx, n_all)
        has_prev = ((c != 0) & (c != n_ctx)).astype(F32)
        has_next = ((c != n_ctx - 1) & (c != n_all - 1)).astype(F32)
        ext = jnp.concatenate([p_ref[...] * has_prev, u_ref[...], n_ref[...] * has_next], axis=0)
        n_ext = T + 2 * LRU_HALO
        u = cb_ref[...] + cw[2:3] * ext[LRU_HALO:LRU_HALO + T]
        for tap, sh in ((0, 2), (1, 1), (3, n_ext - 1)):
            u = u + cw[tap:tap + 1] * pltpu.roll(ext, sh, 0)[LRU_HALO:LRU_HALO + T]
        ub = u.astype(BF16)
        r = jax.nn.sigmoid(jnp.dot(ub, wa_ref[direction], preferred_element_type=F32) + ba_ref[direction])
        i = jax.nn.sigmoid(jnp.dot(ub, wx_ref[direction], preferred_element_type=F32) + bx_ref[direction])
        lam = lam_ref[direction]
        log_a = (-LRU_C) * r * jnp.log1p(jnp.exp(-lam))
        a = jnp.exp(log_a)
        bb = jnp.sqrt(1.0 - a * a) * (i * u)
        sh = 1
        while sh < T:
            if direction == 0:
                ok = row >= sh
                a_s, b_s = pltpu.roll(a, sh, 0), pltpu.roll(bb, sh, 0)
            else:
                ok = row < T - sh
                a_s, b_s = pltpu.roll(a, T - sh, 0), pltpu.roll(bb, T - sh, 0)
            bb = jnp.where(ok, a * b_s + bb, bb)
            a = jnp.where(ok, a * a_s, a)
            sh *= 2
        hcur = bb + a * carry_ref[direction]
        o_ref[...] = hcur
        last = T - 1 if direction == 0 else 0
        carry_ref[direction] = hcur[last:last + 1, :]


def _lru(zd, p, lc):
    b, s, _ = zd.shape
    n_all, n_ctx = s // LRU_T, lc // LRU_T
    per = LRU_T // LRU_HALO
    n_halo = s // LRU_HALO
    ident = lambda j: j
    mirr = lambda j: _mirror(j, n_ctx, n_all)

    def specs(cf):
        return [pl.BlockSpec((None, LRU_T, LRU_W), lambda bb, j: (bb, cf(j), 0)),
                pl.BlockSpec((None, LRU_HALO, LRU_W), lambda bb, j: (bb, jnp.maximum(cf(j) * per - 1, 0), 0)),
                pl.BlockSpec((None, LRU_HALO, LRU_W), lambda bb, j: (bb, jnp.minimum((cf(j) + 1) * per, n_halo - 1), 0))]

    vec = _whole((2, 1, LRU_W))
    mat = _whole((2, LRU_W, LRU_W))
    return pl.pallas_call(
        functools.partial(_lru_kernel, n_ctx=n_ctx, n_all=n_all),
        grid=(b, n_all),
        in_specs=specs(ident) + specs(mirr) + [_whole((4, LRU_W)), _whole((1, LRU_W)), mat, vec, mat, vec, vec],
        out_specs=[pl.BlockSpec((None, LRU_T, LRU_W), lambda bb, j: (bb, j, 0)),
                   pl.BlockSpec((None, LRU_T, LRU_W), lambda bb, j: (bb, mirr(j), 0))],
        out_shape=[jax.ShapeDtypeStruct((b, s, LRU_W), F32)] * 2,
        scratch_shapes=[pltpu.VMEM((2, 1, LRU_W), F32)],
        compiler_params=_cp(("parallel", "arbitrary"), VMEM_LIMIT),
        name="rglru_scan",
    )(zd, zd, zd, zd, zd, zd, p["conv_w"], p["conv_b"], p["wa"], p["ba"], p["wx"], p["bx"], p["lam"])


def _out_kernel(x_ref, gate_ref, hfa_ref, hba_ref, o_ref, gn_ref, yb_ref, yc_ref, hfd_ref, hbd_ref, gd_ref, w_ref, out_ref):
    pair = [(0, ML_DH), (ML_DH, 2 * ML_DH)]
    gm_pair = _group_mean_matrix(LANE, pair)
    ha = hfa_ref[...] + hba_ref[...]
    ha = jnp.concatenate([_group_rms(ha[:, :LANE], pair, gm_pair), _group_rms(ha[:, LANE:], pair, gm_pair)],
                         axis=-1) * gn_ref[...]
    ya = (jax.nn.sigmoid(o_ref[...].astype(F32)) * ha).astype(BF16)
    yd = (jax.nn.gelu(gd_ref[...]) * (hfd_ref[...] + hbd_ref[...])).astype(BF16)
    acc = jnp.dot(ya, w_ref[0:ML_W, :], preferred_element_type=F32)
    acc = acc + jnp.dot(yb_ref[...], w_ref[ML_W:2 * ML_W, :], preferred_element_type=F32)
    acc = acc + jnp.dot(yc_ref[...], w_ref[2 * ML_W:3 * ML_W, :], preferred_element_type=F32)
    acc = acc + jnp.dot(yd, w_ref[3 * ML_W:4 * ML_W, :], preferred_element_type=F32)
    out_ref[...] = x_ref[...] + gate_ref[...] * acc


def _out_proj(xa, mod, hfa, hba, za, gn, yb, yc, hfd, hbd, zd, w, nb, ctx_tiles, off):
    b, s, _ = xa.shape
    nt = s // TM - off
    r = lambda w_, col=0: _rows(w_, off, col)
    return pl.pallas_call(
        _out_kernel,
        grid=(b, nt),
        in_specs=[r(D), _mod_spec(2, nb, ctx_tiles, off), r(ML_W), r(ML_W), r(ML_W, 3), _whole((1, ML_W)),
                  r(ML_W), r(ML_W), r(ML_W), r(ML_W), r(ML_W, 1), _whole((D, D))],
        out_specs=_rows(D),
        out_shape=jax.ShapeDtypeStruct((b, nt * TM, D), F32),
        compiler_params=_cp(("parallel", "parallel"), VMEM_LIMIT),
        name="out_proj",
    )(xa, mod, hfa, hba, za, gn, yb, yc, hfd, hbd, zd, w)


def _ffn_kernel(x_ref, g_ref, sh_ref, sc_ref, gate_ref, w13_ref, w2_ref, out_ref, *, d_ff):
    x = x_ref[...]
    h = _rms(x) * g_ref[...]
    hb = (h * (1.0 + sc_ref[...]) + sh_ref[...]).astype(BF16)
    a = jnp.dot(hb, w13_ref[:, :d_ff], preferred_element_type=F32)
    g = jnp.dot(hb, w13_ref[:, d_ff:], preferred_element_type=F32)
    u = (g * jax.nn.sigmoid(g) * a).astype(BF16)
    out_ref[...] = x + gate_ref[...] * jnp.dot(u, w2_ref[...], preferred_element_type=F32)


def _ffn(xa, gain, mod, w13, w2, nb, ctx_tiles):
    b, s, _ = xa.shape
    d_ff = w2.shape[0]
    return pl.pallas_call(
        functools.partial(_ffn_kernel, d_ff=d_ff),
        grid=(b, s // TM),
        in_specs=[_rows(D), _whole((1, D)), _mod_spec(3, nb, ctx_tiles, 0), _mod_spec(4, nb, ctx_tiles, 0),
                  _mod_spec(5, nb, ctx_tiles, 0), _whole((D, 2 * d_ff)), _whole((d_ff, D))],
        out_specs=_rows(D),
        out_shape=jax.ShapeDtypeStruct((b, s, D), F32),
        compiler_params=_cp(("parallel", "parallel"), VMEM_LIMIT),
        name="ffn",
    )(xa, gain.reshape(1, D), mod, mod, mod, w13, w2)


META_E, META_RANK, META_GATE = 0, 2, 4


def _router_kernel(x_ref, g_ref, sh_ref, sc_ref, wr_ref, br_ref, h_ref, meta_ref, cnt_ref):
    @pl.when((pl.program_id(0) == 0) & (pl.program_id(1) == 0))
    def _():
        cnt_ref[...] = jnp.zeros_like(cnt_ref)

    h = _rms(x_ref[...]) * g_ref[...]
    h = h * (1.0 + sc_ref[...]) + sh_ref[...]
    h_ref[...] = h
    logits = jnp.dot(h, wr_ref[...], precision=HI, preferred_element_type=F32) + br_ref[...]
    lane = _lane(logits.shape)
    logits = jnp.where(lane < N_EXP, logits, NEG)
    m1 = jnp.max(logits, axis=-1, keepdims=True)
    i1 = jnp.min(jnp.where(logits == m1, lane, LANE), axis=-1, keepdims=True)
    rest = jnp.where(lane == i1, NEG, logits)
    m2 = jnp.max(rest, axis=-1, keepdims=True)
    i2 = jnp.min(jnp.where(rest == m2, lane, LANE), axis=-1, keepdims=True)
    e2 = jnp.exp(m2 - m1)
    inv = 1.0 / (1.0 + e2)
    hit1, hit2 = lane == i1, lane == i2
    assign = (hit1 | hit2).astype(F32)
    rr = lax.broadcasted_iota(jnp.int32, (TM, TM), 0)
    cc = lax.broadcasted_iota(jnp.int32, (TM, TM), 1)
    before = jnp.dot((cc < rr).astype(BF16), assign.astype(BF16), preferred_element_type=F32) + cnt_ref[...]
    r1 = jnp.sum(jnp.where(hit1, before, 0.0), axis=-1, keepdims=True)
    r2 = jnp.sum(jnp.where(hit2, before, 0.0), axis=-1, keepdims=True)
    cnt_ref[...] = cnt_ref[...] + jnp.sum(assign, axis=0, keepdims=True)
    fields = (i1.astype(F32), i2.astype(F32), r1, r2, inv, e2 * inv)
    meta = jnp.zeros(logits.shape, F32)
    for k, val in enumerate(fields):
        meta = jnp.where(lane == k, val, meta)
    meta_ref[...] = meta


def _router(xl, gain, mod, wr, br):
    b, t, _ = xl.shape
    spec = lambda k: pl.BlockSpec((None, 1, D), lambda bb, j: (bb, 0, k))
    return pl.pallas_call(
        _router_kernel,
        grid=(b, t // TM),
        in_specs=[_rows(D), _whole((1, D)), spec(3), spec(4), _whole((D, LANE)), _whole((1, LANE))],
        out_specs=[_rows(D), _rows(LANE), _whole((1, LANE))],
        out_shape=[jax.ShapeDtypeStruct((b, t, D), F32), jax.ShapeDtypeStruct((b, t, LANE), F32),
                   jax.ShapeDtypeStruct((1, LANE), F32)],
        compiler_params=_cp(("arbitrary", "arbitrary"), VMEM_LIMIT),
        name="router",
    )(xl, gain.reshape(1, D), mod, mod, wr, br)


TG = 256
TOP_K = 2


ROW_DMA_UNROLL = 8


def _row_dmas(route_ref, start_ref, make_copy, tile_copies, sem):
    def issue(r, c):
        for k in range(TOP_K):
            pos = start_ref[route_ref[0, (META_E + k) * TM + r]] + route_ref[0, (META_RANK + k) * TM + r]
            make_copy(r, k, pos, sem).start()
        return c

    lax.fori_loop(0, TM, issue, 0, unroll=ROW_DMA_UNROLL)
    for cp in tile_copies:
        cp.wait()


def _dispatch_kernel(route_ref, start_ref, h_ref, init_ref, xs_ref, sem):
    del init_ref
    whole = pltpu.make_async_copy(h_ref, xs_ref.at[pl.ds(0, TM)], sem)
    _row_dmas(route_ref, start_ref,
              lambda r, k, pos, s: pltpu.make_async_copy(h_ref.at[pl.ds(r, 1)], xs_ref.at[pl.ds(pos, 1)], s),
              [whole] * TOP_K, sem)


def _dispatch(h2, route, start, n_rows):
    n = h2.shape[0]
    return pl.pallas_call(
        _dispatch_kernel,
        grid=(n // TM,),
        in_specs=[pl.BlockSpec((None, 1, 2 * TOP_K * TM), lambda i: (i, 0, 0), memory_space=pltpu.SMEM),
                  pl.BlockSpec(memory_space=pltpu.SMEM),
                  pl.BlockSpec((TM, D), lambda i: (i, 0)),
                  pl.BlockSpec(memory_space=pl.ANY)],
        out_specs=pl.BlockSpec(memory_space=pl.ANY),
        out_shape=jax.ShapeDtypeStruct((n_rows, D), F32),
        scratch_shapes=[pltpu.SemaphoreType.DMA(())],
        input_output_aliases={3: 0},
        compiler_params=_cp(("arbitrary",), VMEM_LIMIT),
        name="moe_dispatch",
    )(route, start, h2, jnp.zeros((n_rows, D), F32))


def _expert_kernel(te_ref, nu_ref, xs_ref, w13_ref, w2_ref, y_ref, *, d_ff):
    del te_ref
    t = pl.program_id(0)

    @pl.when(t < nu_ref[0])
    def _():
        xb = xs_ref[...].astype(BF16)
        a = jnp.dot(xb, w13_ref[:, :d_ff], preferred_element_type=F32)
        g = jnp.dot(xb, w13_ref[:, d_ff:], preferred_element_type=F32)
        u = (g * jax.nn.sigmoid(g) * a).astype(BF16)
        y_ref[...] = jnp.dot(u, w2_ref[...], preferred_element_type=F32)

    @pl.when(t >= nu_ref[0])
    def _():
        y_ref[...] = jnp.zeros_like(y_ref)


def _experts(xs, tile_expert, n_used, w13, w2):
    n_rows = xs.shape[0]
    d_ff = w2.shape[1]
    once = dict(pipeline_mode=pl.Buffered(1))
    grid_spec = pltpu.PrefetchScalarGridSpec(
        num_scalar_prefetch=2,
        grid=(n_rows // TG,),
        in_specs=[pl.BlockSpec((TG, D), lambda t, te, nu: (t, 0)),
                  pl.BlockSpec((None, D, 2 * d_ff), lambda t, te, nu: (te[t], 0, 0), **once),
                  pl.BlockSpec((None, d_ff, D), lambda t, te, nu: (te[t], 0, 0), **once)],
        out_specs=pl.BlockSpec((TG, D), lambda t, te, nu: (t, 0)),
    )
    return pl.pallas_call(
        functools.partial(_expert_kernel, d_ff=d_ff),
        grid_spec=grid_spec,
        out_shape=jax.ShapeDtypeStruct((n_rows, D), F32),
        compiler_params=_cp(("arbitrary",), VMEM_LIMIT),
        name="moe_experts",
    )(tile_expert, n_used, xs, w13, w2)


def _combine_kernel(route_ref, start_ref, x_ref, meta_ref, gate_ref, y_ref, out_ref, buf_ref, sem):
    _row_dmas(route_ref, start_ref,
              lambda r, k, pos, s: pltpu.make_async_copy(y_ref.at[pl.ds(pos, 1)], buf_ref.at[k, pl.ds(r, 1)], s),
              [pltpu.make_async_copy(y_ref.at[pl.ds(0, TM)], buf_ref.at[k], sem) for k in range(TOP_K)], sem)
    meta = meta_ref[...]
    mix = meta[:, META_GATE:META_GATE + 1] * buf_ref[0] + meta[:, META_GATE + 1:META_GATE + 2] * buf_ref[1]
    out_ref[...] = x_ref[...] + gate_ref[...] * mix


def _combine(xl, meta, mod, y, route, start):
    b, t, _ = xl.shape
    tiles = t // TM
    return pl.pallas_call(
        _combine_kernel,
        grid=(b, tiles),
        in_specs=[pl.BlockSpec((None, 1, 2 * TOP_K * TM), lambda bb, j: (bb * tiles + j, 0, 0), memory_space=pltpu.SMEM),
                  pl.BlockSpec(memory_space=pltpu.SMEM),
                  _rows(D), _rows(LANE), pl.BlockSpec((None, 1, D), lambda bb, j: (bb, 0, 5)),
                  pl.BlockSpec(memory_space=pl.ANY)],
        out_specs=_rows(D),
        out_shape=jax.ShapeDtypeStruct((b, t, D), F32),
        scratch_shapes=[pltpu.VMEM((TOP_K, TM, D), F32), pltpu.SemaphoreType.DMA(())],
        compiler_params=_cp(("arbitrary", "arbitrary"), VMEM_LIMIT),
        name="moe_combine",
    )(route, start, xl, meta, mod, y)


def _moe(xl, gain, mod, wr, br, w13, w2):
    b, t, _ = xl.shape
    n = b * t
    h, meta, cnt = _router(xl, gain, mod, wr, br)
    counts = cnt[0, :N_EXP].astype(jnp.int32)
    padded = (counts + TG - 1) // TG * TG
    ends = jnp.cumsum(padded)
    start = ends - padded
    n_tiles = TOP_K * n // TG + N_EXP
    tile_lo = jnp.arange(n_tiles, dtype=jnp.int32) * TG
    tile_expert = jnp.minimum(jnp.sum((tile_lo[:, None] >= ends[None, :]).astype(jnp.int32), axis=1), N_EXP - 1)
    n_used = (ends[-1:] // TG).astype(jnp.int32)
    ids = meta.reshape(n // TM, TM, LANE)[:, :, :2 * TOP_K].astype(jnp.int32)
    route = jnp.swapaxes(ids, 1, 2).reshape(n // TM, 1, 2 * TOP_K * TM)
    xs = _dispatch(h.reshape(n, D), route, start, n_tiles * TG)
    y = _experts(xs, tile_expert, n_used, w13, w2)
    return _combine(xl, meta, mod, y, route, start)


def _pad_cols(w, n):
    return jnp.pad(w, ((0, 0), (0, n - w.shape[1])))


def _layout_w_in(w):
    ml = 4 * ML_W
    a, gates = w[:, :ml], w[:, ml:ml + 4 * ML_H]
    o = ml + 4 * ML_H
    cq, ckv, kr = w[:, o:o + MLA_QR], w[:, o + MLA_QR:o + MLA_QR + MLA_KVR], w[:, o + MLA_QR + MLA_KVR:o + MLA_QR + MLA_KVR + MLA_ROPE]
    o += MLA_QR + MLA_KVR + MLA_ROPE
    sw = w[:, o:o + (SW_H + 2 * SW_KV) * SW_DH]
    o += (SW_H + 2 * SW_KV) * SW_DH
    lru = w[:, o:]
    z = lambda n: jnp.zeros((w.shape[0], n), w.dtype)
    cols = [a, _pad_cols(gates, LANE), _pad_cols(cq, 2 * LANE), ckv, z(MLA_NOPE), kr, z(LANE - MLA_NOPE - MLA_ROPE), sw, lru]
    return jnp.concatenate(cols, axis=1).astype(BF16)


IN_SEGS = ((0, 4 * ML_W), (4 * ML_W, LANE), (4 * ML_W + LANE, 4 * LANE), (4 * ML_W + 5 * LANE, 4 * LANE),
           (4 * ML_W + 9 * LANE, 4 * LANE))
IN_DTYPES = (BF16, F32, F32, F32, F32)


def _row(v, n=None):
    v = v.reshape(1, -1)
    return v if n is None else _pad_cols(v, n)


def _layer_params(l, ml_gate_b, ml_out_norm, mla_q_norm, mla_w_uq, mla_kv_norm, mla_w_ukv, mla_q_gain, mla_k_gain,
                  sw_q_gain, sw_k_gain, lru_conv_w, lru_conv_b, lru_wa, lru_ba, lru_wx, lru_bx, lru_lam):
    hq = MLA_NOPE + MLA_ROPE
    wq = mla_w_uq[l].reshape(MLA_QR, MLA_H, hq)
    wq = jnp.pad(wq, ((0, 2 * LANE - MLA_QR), (0, 0), (0, LANE - hq))).reshape(2 * LANE, MLA_H * LANE)
    wkv = mla_w_ukv[l].reshape(MLA_KVR, MLA_H, MLA_NOPE + MLA_V)
    wk = jnp.pad(wkv[:, :, :MLA_NOPE], ((0, 0), (0, 0), (0, LANE - MLA_NOPE))).reshape(MLA_KVR, MLA_H * LANE)
    wv = wkv[:, :, MLA_NOPE:].reshape(MLA_KVR, MLA_H * MLA_V)

    def blockdiag(wb):
        bw = LRU_W // LRU_BLOCKS
        eye = jnp.eye(LRU_BLOCKS, dtype=wb.dtype)
        return jnp.einsum('xncd,nm->xncmd', wb, eye).reshape(2, LRU_W, LRU_W)

    return dict(
        gate_b=_row(ml_gate_b[l], LANE), out_norm=_row(ml_out_norm[l]),
        q_norm=_row(mla_q_norm[l], 2 * LANE), w_uq=wq.astype(BF16), kv_norm=_row(mla_kv_norm[l]),
        w_uk=wk.astype(BF16), w_uv=wv.astype(BF16),
        q_gain=_row(mla_q_gain[l], LANE), k_gain=_row(mla_k_gain[l], LANE),
        sw_q_gain=_row(jnp.tile(sw_q_gain[l], 2)), sw_k_gain=_row(jnp.tile(sw_k_gain[l], 2)),
        conv_w=lru_conv_w[l], conv_b=_row(lru_conv_b[l]),
        wa=blockdiag(lru_wa[l]).astype(BF16), ba=lru_ba[l].reshape(2, 1, LRU_W),
        wx=blockdiag(lru_wx[l]).astype(BF16), bx=lru_bx[l].reshape(2, 1, LRU_W),
        lam=lru_lam[l].reshape(2, 1, LRU_W),
    )


def _mix_layer(xa, mod, gain, w_in_l, w_out_l, sink, p, tabs, lc, nb, ctx_out):
    ctx_tiles = lc // TM
    za, zg, zb, zc, zd = _proj(xa, gain, mod, w_in_l, IN_SEGS, IN_DTYPES, nb, ctx_tiles)
    hfa, hba = _mlstm(za, zg, p["gate_b"], lc)
    qm, km, vm, qs, ks, vs = _prep(zb, zc, tabs[0], tabs[1], p)
    yb = _mla_attn(qm, km, vm, lc)
    pad = ((0, 0), (0, 0), (0, SW_BLOCK), (0, 0))
    yc = _swa_attn(qs, jnp.pad(ks, pad), jnp.pad(vs, pad), sink, lc)
    hfd, hbd = _lru(zd, p, lc)
    off = 0 if ctx_out else ctx_tiles
    return _out_proj(xa, mod, hfa, hba, za, p["out_norm"], yb, yc, hfd, hbd, zd, w_out_l, nb, ctx_tiles, off)


def kernel(x, c, ctx, c_ctx, ada_w, ada_b, norm_mix, norm_ffn, w_in, w_out, ml_gate_b, ml_out_norm, mla_q_norm, mla_w_uq, mla_kv_norm, mla_w_ukv, mla_q_gain, mla_k_gain, sw_q_gain, sw_k_gain, sw_sink, lru_conv_w, lru_conv_b, lru_wa, lru_ba, lru_wx, lru_bx, lru_lam, ffn_w13, ffn_w2, moe_router, moe_router_b, moe_w13, moe_w2):
    nb, t_len, _ = x.shape
    lc = ctx.shape[1]
    depth = ada_w.shape[0]
    assert nb < SUB and lc % TM == 0 and t_len % TM == 0 and t_len % GRID_W == 0
    cc = jnp.zeros((SUB, D), F32).at[:nb].set(c).at[nb].set(c_ctx)
    mods = _ada_mod(cc, ada_w, ada_b)
    tabs = (_rope_tables(t_len, lc, LANE, MLA_ROPE, MLA_NOPE), _rope_tables(t_len, lc, SW_DH, SW_DH, 0))
    xa = jnp.concatenate([ctx, x], axis=1)
    for l in range(depth):
        ctx_out = l < depth - 1
        mod = mods[l].reshape(SUB, 1, 6 * D)
        p = _layer_params(l, ml_gate_b, ml_out_norm, mla_q_norm, mla_w_uq, mla_kv_norm, mla_w_ukv, mla_q_gain,
                          mla_k_gain, sw_q_gain, sw_k_gain, lru_conv_w, lru_conv_b, lru_wa, lru_ba, lru_wx, lru_bx,
                          lru_lam)
        xa = _mix_layer(xa, mod, norm_mix[l], _layout_w_in(w_in[l]), w_out[l].astype(BF16), sw_sink[l], p, tabs,
                        lc, nb, ctx_out)
        if l % 2 == 0:
            xa = _ffn(xa, norm_ffn[l], mod, ffn_w13[l // 2].astype(BF16), ffn_w2[l // 2].astype(BF16), nb,
                      lc // TM if ctx_out else 0)
        else:
            xa = _moe(xa, norm_ffn[l], mod, _pad_cols(moe_router[l // 2], LANE), _row(moe_router_b[l // 2], LANE),
                      moe_w13[l // 2].astype(BF16), moe_w2[l // 2].astype(BF16))
    return xa
```

```python
import functools
import itertools

import numpy as np
import jax
import jax.numpy as jnp
from jax import lax
from jax.experimental import pallas as pl
from jax.experimental.pallas import tpu as pltpu

F32 = jnp.float32
BF16 = jnp.bfloat16
HI = lax.Precision.HIGHEST

D = 1024
GRID_W = 64
EPS = 1e-6
ROPE_BASE = 10000.0
ML_H, ML_DH, ML_W, ML_CHUNK = 4, 64, 256, 128
MLA_H, MLA_QR, MLA_KVR, MLA_NOPE, MLA_ROPE, MLA_V = 4, 192, 128, 64, 32, 64
SW_H, SW_KV, SW_DH, SW_WINDOW, SW_BLOCK = 4, 2, 64, 128, 128
LRU_W, LRU_BLOCKS, LRU_C = 256, 4, 8.0
N_EXP = 8

LANE = 128
SUB = 8
TM = 256
NEG = -1e30
VMEM_LIMIT = 56 * 1024 * 1024


def _cp(sem, vmem=None):
    return pltpu.CompilerParams(dimension_semantics=sem, vmem_limit_bytes=vmem)


def _mod_spec(chunk, nb, ctx_tiles, off):
    return pl.BlockSpec((None, 1, D), lambda b, j: (jnp.where(j + off < ctx_tiles, nb, b), 0, chunk))


def _rows(width, off=0, col=0, tm=TM):
    return pl.BlockSpec((None, tm, width), lambda b, j: (b, j + off, col))


def _whole(shape):
    nd = len(shape)
    return pl.BlockSpec(shape, lambda *_: (0,) * nd)


def _rms(x):
    return x * lax.rsqrt(jnp.mean(x * x, axis=-1, keepdims=True) + EPS)


def _lane(shape):
    return lax.broadcasted_iota(jnp.int32, shape, len(shape) - 1)


def _group_mean_matrix(n, bounds):
    i = lax.broadcasted_iota(jnp.int32, (n, n), 0)
    j = lax.broadcasted_iota(jnp.int32, (n, n), 1)
    g = jnp.zeros((n, n), F32)
    for lo, hi in bounds:
        g = jnp.where((i >= lo) & (i < hi) & (j >= lo) & (j < hi), 1.0 / (hi - lo), g)
    return g.astype(BF16)


def _group_rms(blk, bounds, gm):
    lane = _lane(blk.shape)
    sq = blk * blk
    hi_part = sq.astype(BF16)
    lo_part = (sq - hi_part.astype(F32)).astype(BF16)
    ms = (jnp.dot(hi_part, gm, preferred_element_type=F32) + jnp.dot(lo_part, gm, preferred_element_type=F32))
    inside = functools.reduce(jnp.logical_or, [(lane >= lo) & (lane < hi) for lo, hi in bounds])
    return blk * jnp.where(inside, lax.rsqrt(ms + EPS), 0.0)


def _rope(blk, tab_ref, shift):
    n = blk.shape[-1]
    return (blk * tab_ref[0] + pltpu.roll(blk, n - shift, 1) * tab_ref[1]
            + pltpu.roll(blk, shift, 1) * tab_ref[2])


def _rope_tables(t_len, lc, width, dims, offset):
    half, nf = dims // 2, dims // 4
    p = np.arange(dims)
    i = p % half
    f = i % nf
    first = i < nf
    freq = ROPE_BASE ** (-f.astype(np.float64) / nf)
    t = np.arange(t_len)
    pos = np.where(p[None, :] < half, (t // GRID_W)[:, None], (t % GRID_W)[:, None]).astype(np.float64)
    ang = pos * freq[None, :]
    cos, sin = np.cos(ang), np.sin(ang)
    tab = np.zeros((3, lc + t_len, width), np.float64)
    tab[0] = 1.0
    tab[0, lc:, offset:offset + dims] = cos
    tab[1, lc:, offset:offset + dims] = np.where(first[None, :], -sin, 0.0)
    tab[2, lc:, offset:offset + dims] = np.where(first[None, :], 0.0, sin)
    reps = LANE // width
    return jnp.asarray(np.tile(tab, (1, 1, reps)), F32)


def _ada_kernel(c_ref, w_ref, b_ref, o_ref):
    c = c_ref[...]
    s = c * jax.nn.sigmoid(c)
    o_ref[...] = jnp.dot(s, w_ref[...], precision=HI, preferred_element_type=F32) + b_ref[...]


def _ada_mod(cc, ada_w, ada_b):
    depth = ada_w.shape[0]
    n = ada_w.shape[2]
    tn = 1024
    return pl.pallas_call(
        _ada_kernel,
        grid=(depth, n // tn),
        in_specs=[
            pl.BlockSpec((SUB, D), lambda l, j: (0, 0)),
            pl.BlockSpec((None, D, tn), lambda l, j: (l, 0, j)),
            pl.BlockSpec((None, 1, tn), lambda l, j: (l, 0, j)),
        ],
        out_specs=pl.BlockSpec((None, SUB, tn), lambda l, j: (l, 0, j)),
        out_shape=jax.ShapeDtypeStruct((depth, SUB, n), F32),
        compiler_params=_cp(("parallel", "parallel")),
        name="ada_mod",
    )(cc, ada_w, ada_b.reshape(depth, 1, n))


def _proj_kernel(x_ref, g_ref, sh_ref, sc_ref, w_ref, *out_refs, segs):
    h = _rms(x_ref[...]) * g_ref[...]
    h = h * (1.0 + sc_ref[...]) + sh_ref[...]
    hb = h.astype(BF16)
    for (start, width), o_ref in zip(segs, out_refs):
        o_ref[...] = jnp.dot(hb, w_ref[:, start:start + width], preferred_element_type=F32).astype(o_ref.dtype)


def _proj(xa, gain, mod, w, segs, dtypes, nb, ctx_tiles):
    b, s, _ = xa.shape
    n = w.shape[1]
    return pl.pallas_call(
        functools.partial(_proj_kernel, segs=segs),
        grid=(b, s // TM),
        in_specs=[
            _rows(D),
            _whole((1, D)),
            _mod_spec(0, nb, ctx_tiles, 0),
            _mod_spec(1, nb, ctx_tiles, 0),
            _whole((D, n)),
        ],
        out_specs=[_rows(wd) for _, wd in segs],
        out_shape=[jax.ShapeDtypeStruct((b, s, wd), dt) for (_, wd), dt in zip(segs, dtypes)],
        compiler_params=_cp(("parallel", "parallel"), VMEM_LIMIT),
        name="in_proj",
    )(xa, gain.reshape(1, D), mod, mod, w)


def _prep_kernel(zb_ref, zc_ref, tm_ref, ts_ref, qn_ref, wuq_ref, kvn_ref, wk_ref, wv_ref, qg_ref, kg_ref,
                 sqg_ref, skg_ref, qm_ref, km_ref, vm_ref, qs_ref, ks_ref, vs_ref):
    zb = zb_ref[...]
    cq = zb[:, :2 * LANE]
    cqn = cq * lax.rsqrt(jnp.sum(cq * cq, axis=-1, keepdims=True) * (1.0 / MLA_QR) + EPS) * qn_ref[...]
    qf = jnp.dot(cqn.astype(BF16), wuq_ref[...], preferred_element_type=F32)
    q_scale = (MLA_NOPE + MLA_ROPE) ** -0.5 * LOG2E
    head_bounds = [(0, MLA_NOPE), (MLA_NOPE, MLA_NOPE + MLA_ROPE)]
    gm_head = _group_mean_matrix(LANE, head_bounds)
    for h in range(MLA_H):
        blk = _group_rms(qf[:, h * LANE:(h + 1) * LANE], head_bounds, gm_head) * qg_ref[...]
        qm_ref[h] = (_rope(blk, tm_ref, MLA_ROPE // 4) * q_scale).astype(BF16)
    ckvn = _rms(zb[:, 2 * LANE:3 * LANE]) * kvn_ref[...]
    ckvb = ckvn.astype(BF16)
    kf = jnp.dot(ckvb, wk_ref[...], preferred_element_type=F32)
    vf = jnp.dot(ckvb, wv_ref[...], preferred_element_type=F32)
    kg = kg_ref[...]
    kr = _group_rms(zb[:, 3 * LANE:4 * LANE], head_bounds[1:], gm_head) * kg
    kr = _rope(kr, tm_ref, MLA_ROPE // 4)
    for h in range(MLA_H):
        kn = _group_rms(kf[:, h * LANE:(h + 1) * LANE], head_bounds[:1], gm_head) * kg
        km_ref[h] = (kn + kr).astype(BF16)
        vm_ref[h] = vf[:, h * MLA_V:(h + 1) * MLA_V].astype(BF16)
    zc = zc_ref[...]
    pair = [(0, SW_DH), (SW_DH, 2 * SW_DH)]
    gm_pair = _group_mean_matrix(LANE, pair)
    sw_scale = SW_DH ** -0.5
    for half in range(2):
        blk = _group_rms(zc[:, half * LANE:(half + 1) * LANE], pair, gm_pair) * sqg_ref[...]
        qs_ref[:, half * LANE:(half + 1) * LANE] = (_rope(blk, ts_ref, SW_DH // 4) * sw_scale).astype(BF16)
    kb = _rope(_group_rms(zc[:, 2 * LANE:3 * LANE], pair, gm_pair) * skg_ref[...], ts_ref, SW_DH // 4).astype(BF16)
    vb = zc[:, 3 * LANE:4 * LANE].astype(BF16)
    for j in range(SW_KV):
        ks_ref[j] = kb[:, j * SW_DH:(j + 1) * SW_DH]
        vs_ref[j] = vb[:, j * SW_DH:(j + 1) * SW_DH]


def _prep(zb, zc, tab_mla, tab_sw, p):
    b, s, _ = zb.shape
    tab = pl.BlockSpec((3, TM, LANE), lambda bb, j: (0, j, 0))
    heads = lambda nh, w: pl.BlockSpec((None, nh, TM, w), lambda bb, j: (bb, 0, j, 0))
    return pl.pallas_call(
        _prep_kernel,
        grid=(b, s // TM),
        in_specs=[_rows(4 * LANE), _rows(4 * LANE), tab, tab,
                  _whole((1, 2 * LANE)), _whole((2 * LANE, 4 * LANE)), _whole((1, LANE)),
                  _whole((LANE, 4 * LANE)), _whole((LANE, 2 * LANE)), _whole((1, LANE)), _whole((1, LANE)),
                  _whole((1, LANE)), _whole((1, LANE))],
        out_specs=[heads(MLA_H, LANE), heads(MLA_H, LANE), heads(MLA_H, MLA_V),
                   _rows(2 * LANE), heads(SW_KV, SW_DH), heads(SW_KV, SW_DH)],
        out_shape=[jax.ShapeDtypeStruct((b, MLA_H, s, LANE), BF16),
                   jax.ShapeDtypeStruct((b, MLA_H, s, LANE), BF16),
                   jax.ShapeDtypeStruct((b, MLA_H, s, MLA_V), BF16),
                   jax.ShapeDtypeStruct((b, s, 2 * LANE), BF16),
                   jax.ShapeDtypeStruct((b, SW_KV, s, SW_DH), BF16),
                   jax.ShapeDtypeStruct((b, SW_KV, s, SW_DH), BF16)],
        compiler_params=_cp(("parallel", "parallel"), VMEM_LIMIT),
        name="head_prep",
    )(zb, zc, tab_mla, tab_sw, p["q_norm"], p["w_uq"], p["kv_norm"], p["w_uk"], p["w_uv"], p["q_gain"],
      p["k_gain"], p["sw_q_gain"], p["sw_k_gain"])


MLA_TQ = 256
MLA_KC = 256
LOG2E = 1.4426950408889634
MLA_VA = MLA_V + 16
MLA_UNROLL = 2


def _mla_kernel(qt_ref, k_ref, vt_ref, o_ref, m_ref, acc_ref, st_ref, cm_ref, p_ref, al_ref, *, n_all, n_ctx,
                ctx_tiles):
    i = pl.program_id(1)
    n_kv = jnp.where(i < ctx_tiles, n_ctx, n_all)
    last = n_kv - 1

    def scores(c, slot):
        off = pl.multiple_of(jnp.minimum(c, last) * MLA_KC, MLA_KC)
        for h in range(MLA_H):
            st = jnp.dot(k_ref[h, pl.ds(off, MLA_KC), :], qt_ref[h], preferred_element_type=F32)
            st_ref[slot, h] = st
            cm_ref[slot, h] = jnp.max(st, axis=0, keepdims=True)

    def softmax(slot, first):
        for h in range(MLA_H):
            cm = cm_ref[slot, h]
            m_new = cm if first else jnp.maximum(m_ref[h], cm)
            p_ref[slot, h] = jnp.exp2(st_ref[slot, h] - m_new).astype(BF16)
            al_ref[slot, h] = jnp.ones_like(cm) if first else jnp.exp2(m_ref[h] - m_new)
            m_ref[h] = m_new

    def weighted_values(c, slot):
        pv = [jnp.dot(vt_ref[h, c], p_ref[slot, h], preferred_element_type=F32) for h in range(MLA_H)]
        return pv

    def accumulate(pv, slot):
        for h in range(MLA_H):
            acc_ref[h] = al_ref[slot, h] * acc_ref[h] + pv[h]

    acc_ref[...] = jnp.zeros_like(acc_ref)
    scores(0, 0)
    softmax(0, True)
    scores(1, 1)

    def body(j, carry):
        for u in range(MLA_UNROLL):
            c, slot = MLA_UNROLL * j + u, u & 1
            pv = weighted_values(c, slot)
            scores(c + 2, slot)
            softmax(1 - slot, False)
            accumulate(pv, slot)
        return carry

    lax.fori_loop(0, last // MLA_UNROLL, body, 0)
    accumulate(weighted_values(last, 0), 0)
    outs = [(acc_ref[h, :MLA_V] / acc_ref[h, MLA_V:MLA_V + 1]).T for h in range(MLA_H)]
    o_ref[...] = jnp.concatenate(outs, axis=-1).astype(o_ref.dtype)


def _mla_attn(q, k, v, lc):
    b, h, s, _ = q.shape
    nc = s // MLA_KC
    assert (nc - 1) % MLA_UNROLL == 0 and lc == MLA_KC
    qt = jnp.swapaxes(q, 2, 3)
    ones = jnp.zeros((b, h, s, MLA_VA - MLA_V), v.dtype).at[..., 0].set(1.0)
    va = jnp.concatenate([v, ones], axis=-1)
    vt = jnp.swapaxes(va.reshape(b, h, nc, MLA_KC, MLA_VA), 3, 4)
    kern = functools.partial(_mla_kernel, n_all=nc, n_ctx=lc // MLA_KC, ctx_tiles=lc // MLA_TQ)
    return pl.pallas_call(
        kern,
        grid=(b, s // MLA_TQ),
        in_specs=[pl.BlockSpec((None, h, LANE, MLA_TQ), lambda bb, i: (bb, 0, 0, i)),
                  pl.BlockSpec((None, h, s, LANE), lambda bb, i: (bb, 0, 0, 0)),
                  pl.BlockSpec((None, h, nc, MLA_VA, MLA_KC), lambda bb, i: (bb, 0, 0, 0, 0))],
        out_specs=pl.BlockSpec((None, MLA_TQ, h * MLA_V), lambda bb, i: (bb, i, 0)),
        out_shape=jax.ShapeDtypeStruct((b, s, h * MLA_V), BF16),
        scratch_shapes=[pltpu.VMEM((h, 1, MLA_TQ), F32),
                        pltpu.VMEM((h, MLA_VA, MLA_TQ), F32), pltpu.VMEM((2, h, MLA_KC, MLA_TQ), F32),
                        pltpu.VMEM((2, h, 1, MLA_TQ), F32),
                        pltpu.VMEM((2, h, MLA_KC, MLA_TQ), BF16), pltpu.VMEM((2, h, 1, MLA_TQ), F32)],
        compiler_params=_cp(("parallel", "arbitrary"), VMEM_LIMIT),
        name="mla_attn",
    )(qt, k, vt)


SW_STEP_BLOCKS = 2


def _swa_kernel(sink_ref, q_ref, k_ref, v_ref, o_ref, *, lc, t_len):
    blk = SW_BLOCK
    nw = blk + 2 * SW_WINDOW
    dn = (((1,), (1,)), ((), ()))
    row = lax.broadcasted_iota(jnp.int32, (2 * blk, nw), 0)
    col = lax.broadcasted_iota(jnp.int32, (2 * blk, nw), 1)
    rows1 = lax.broadcasted_iota(jnp.int32, (2 * blk, 1), 0)
    chains = []
    for part in range(SW_STEP_BLOCKS):
        n = pl.program_id(1) * SW_STEP_BLOCKS + part
        start = pl.multiple_of(jnp.clip((n - 1) * blk, 0, lc + t_len - nw), blk)
        qpos = n * blk - lc + (row & (blk - 1))
        kpos = start - lc + col
        valid = (n * blk >= lc) & (kpos >= 0) & (kpos < t_len) & (jnp.abs(qpos - kpos) <= SW_WINDOW)
        for j in range(SW_KV):
            q2 = q_ref[part * blk:(part + 1) * blk, j * LANE:(j + 1) * LANE]
            qs = jnp.concatenate([q2[:, :SW_DH], q2[:, SW_DH:]], axis=0)
            s_loc = lax.dot_general(qs, k_ref[j, pl.ds(start, nw), :], dn, preferred_element_type=F32)
            s_ctx = lax.dot_general(qs, k_ref[j, pl.ds(0, lc), :], dn, preferred_element_type=F32)
            chains.append(dict(j=j, start=start, valid=valid, s_loc=s_loc, s_ctx=s_ctx))
    for ch in chains:
        j = ch["j"]
        s_loc, s_ctx = jnp.where(ch["valid"], ch["s_loc"], NEG), ch["s_ctx"]
        sink = jnp.where(rows1 < blk, sink_ref[2 * j], sink_ref[2 * j + 1])
        m = jnp.maximum(jnp.maximum(jnp.max(s_loc, axis=-1, keepdims=True),
                                    jnp.max(s_ctx, axis=-1, keepdims=True)), sink)
        ch["e_loc"] = jnp.exp(s_loc - m)
        ch["e_ctx"] = jnp.exp(s_ctx - m)
        ch["den"] = (jnp.sum(ch["e_loc"], axis=-1, keepdims=True) + jnp.sum(ch["e_ctx"], axis=-1, keepdims=True)
                     + jnp.exp(sink - m))
    for ch in chains:
        j = ch["j"]
        ch["o"] = (jnp.dot(ch["e_loc"].astype(BF16), v_ref[j, pl.ds(ch["start"], nw), :], preferred_element_type=F32)
                   + jnp.dot(ch["e_ctx"].astype(BF16), v_ref[j, pl.ds(0, lc), :], preferred_element_type=F32))
    for part in range(SW_STEP_BLOCKS):
        outs = []
        for ch in chains[part * SW_KV:(part + 1) * SW_KV]:
            o = ch["o"] / ch["den"]
            outs += [o[:blk], o[blk:]]
        o_ref[part * blk:(part + 1) * blk, :] = jnp.concatenate(outs, axis=-1).astype(o_ref.dtype)


def _swa_attn(q, k, v, sink, lc):
    b, s, _ = q.shape
    t_len = s - lc
    sp = k.shape[2]
    rows = SW_STEP_BLOCKS * SW_BLOCK
    kern = functools.partial(_swa_kernel, lc=lc, t_len=t_len)
    return pl.pallas_call(
        kern,
        grid=(b, s // rows),
        in_specs=[pl.BlockSpec(memory_space=pltpu.SMEM),
                  pl.BlockSpec((None, rows, SW_KV * LANE), lambda bb, n: (bb, n, 0)),
                  pl.BlockSpec((None, SW_KV, sp, SW_DH), lambda bb, n: (bb, 0, 0, 0)),
                  pl.BlockSpec((None, SW_KV, sp, SW_DH), lambda bb, n: (bb, 0, 0, 0))],
        out_specs=pl.BlockSpec((None, rows, SW_KV * LANE), lambda bb, n: (bb, n, 0)),
        out_shape=jax.ShapeDtypeStruct((b, s, SW_KV * LANE), BF16),
        compiler_params=_cp(("parallel", "arbitrary"), VMEM_LIMIT),
        name="swa_attn",
    )(sink, q, k, v)


def _mirror(j, n_ctx, n_all):
    return jnp.where(j < n_ctx, n_ctx - 1 - j, n_all + n_ctx - 1 - j)


ML_AUG = ML_DH + SUB
ML_ROWS = 4


def _mlstm_kernel(xf_ref, qtf_ref, vtf_ref, gf_ref, xb_ref, qtb_ref, vtb_ref, gb_ref, bias_ref, hf_ref, hb_ref,
                  c_ref, m_ref):
    L = ML_CHUNK

    @pl.when(pl.program_id(1) == 0)
    def _():
        c_ref[...] = jnp.zeros_like(c_ref)
        m_ref[...] = jnp.zeros_like(m_ref)

    row = lax.broadcasted_iota(jnp.int32, (L, L), 0)
    col = lax.broadcasted_iota(jnp.int32, (L, L), 1)
    aug_row = lax.broadcasted_iota(jnp.int32, (SUB, L), 0)
    dirs = ((xf_ref, qtf_ref, vtf_ref, gf_ref, hf_ref), (xb_ref, qtb_ref, vtb_ref, gb_ref, hb_ref))
    chains = []
    for r, (direction, (x_ref, qt_ref, vt_ref, g_ref, o_ref)) in itertools.product(range(ML_ROWS), enumerate(dirs)):
        reach = (row <= col) if direction == 0 else (row >= col)
        gt = (g_ref[r] + bias_ref[...]).T
        brow = jnp.dot(jax.nn.log_sigmoid(gt), reach.astype(F32), precision=HI, preferred_element_type=F32)
        ib = gt - pltpu.roll(brow, L - ML_H, 0)
        ib_t = ib.T
        for h in range(ML_H):
            ch = dict(idx=(r * 2 + direction) * ML_H + h, reach=reach, last=L - 1 if direction == 0 else 0)
            gi, gf = 2 * ML_H * direction + h, 2 * ML_H * direction + ML_H + h
            ch.update(b_row=brow[gf:gf + 1, :], ib_row=ib[gi:gi + 1, :], ib_col=ib_t[:, gi:gi + 1])
            ch["k"] = x_ref[r, :, h * ML_DH:(h + 1) * ML_DH] * (ML_DH ** -0.5)
            ch["qt"] = qt_ref[r, h * ML_DH:(h + 1) * ML_DH, :]
            ch["vt"] = vt_ref[r, h * ML_DH:(h + 1) * ML_DH, :]
            ch["c"], ch["m"] = c_ref[ch["idx"]], m_ref[ch["idx"]]
            chains.append(ch)
    for ch in chains:
        ch["kq"] = jnp.dot(ch["k"], ch["qt"], preferred_element_type=F32)
        ch["cq"] = jnp.dot(ch["c"].astype(BF16), ch["qt"], preferred_element_type=F32)
    for ch in chains:
        b_row, m_st = ch["b_row"], ch["m"]
        d = jnp.where(ch["reach"], b_row + ch["ib_col"], NEG)
        m_inter = b_row + m_st
        m_t = jnp.maximum(m_inter, jnp.max(d, axis=0, keepdims=True))
        ch["w_inter"], ch["m_t"] = jnp.exp(m_inter - m_t), m_t
        ch["st"] = ch["kq"] * jnp.exp(d - m_t)
        b_last = b_row[:, ch["last"]:ch["last"] + 1]
        g_row = b_last + ch["ib_row"]
        m_new = jnp.maximum(b_last + m_st, jnp.max(g_row, axis=-1, keepdims=True))
        w_row = jnp.exp(g_row - m_new)
        ch["decay"], ch["m_new"] = jnp.exp(b_last + m_st - m_new), m_new
        ch["upd"] = jnp.concatenate([ch["vt"].astype(F32) * w_row, jnp.where(aug_row == 0, w_row, 0.0)], axis=0)
    for ch in chains:
        ch["num"] = jnp.dot(ch["vt"], ch["st"].astype(BF16), preferred_element_type=F32)
        ch["dc"] = jnp.dot(ch["upd"].astype(BF16), ch["k"], preferred_element_type=F32)
    outs = []
    for ch in chains:
        num = ch["num"] + ch["w_inter"] * ch["cq"][:ML_DH]
        den = jnp.sum(ch["st"], axis=0, keepdims=True) + ch["w_inter"] * ch["cq"][ML_DH:ML_DH + 1]
        outs.append(num / jnp.maximum(jnp.abs(den), jnp.exp(-ch["m_t"])))
        c_ref[ch["idx"]] = ch["decay"] * ch["c"] + ch["dc"]
        m_ref[ch["idx"]] = ch["m_new"]
    for n, (r, o_ref) in enumerate(itertools.product(range(ML_ROWS), (hf_ref, hb_ref))):
        o_ref[r] = jnp.concatenate(outs[n * ML_H:(n + 1) * ML_H], axis=0).T


def _mlstm(za, zg, bias, lc):
    b, s, _ = za.shape
    n_all, n_ctx = s // ML_CHUNK, lc // ML_CHUNK
    zat = jnp.swapaxes(za, 1, 2)
    ident = lambda j: j
    mirr = lambda j: _mirror(j, n_ctx, n_all)

    def specs(cf):
        rows = lambda w, col: pl.BlockSpec((ML_ROWS, ML_CHUNK, w), lambda bb, j: (bb, cf(j), col))
        feats = lambda blk: pl.BlockSpec((ML_ROWS, ML_W, ML_CHUNK), lambda bb, j: (bb, blk, cf(j)))
        return [rows(ML_W, 1), feats(0), feats(2), rows(LANE, 0)]

    out = lambda cf: pl.BlockSpec((ML_ROWS, ML_CHUNK, ML_W), lambda bb, j: (bb, cf(j), 0))
    assert b % ML_ROWS == 0
    return pl.pallas_call(
        _mlstm_kernel,
        grid=(b // ML_ROWS, n_all),
        in_specs=specs(ident) + specs(mirr) + [_whole((1, LANE))],
        out_specs=[out(ident), out(mirr)],
        out_shape=[jax.ShapeDtypeStruct((b, s, ML_W), F32)] * 2,
        scratch_shapes=[pltpu.VMEM((ML_ROWS * 2 * ML_H, ML_AUG, ML_DH), F32),
                        pltpu.VMEM((ML_ROWS * 2 * ML_H, 1, 1), F32)],
        compiler_params=_cp(("parallel", "arbitrary"), VMEM_LIMIT),
        name="mlstm_scan",
    )(za, zat, zat, zg, za, zat, zat, zg, bias)


LRU_T = 256
LRU_HALO = SUB


def _lru_kernel(uf_ref, pf_ref, nf_ref, ub_ref, pb_ref, nb_ref, cw_ref, cb_ref, wa_ref, ba_ref, wx_ref, bx_ref,
                lam_ref, hf_ref, hb_ref, carry_ref, *, n_ctx, n_all):
    T = LRU_T
    j = pl.program_id(1)

    @pl.when(j == 0)
    def _():
        carry_ref[...] = jnp.zeros_like(carry_ref)

    cw = cw_ref[...]
    row = lax.broadcasted_iota(jnp.int32, (T, LRU_W), 0)
    for direction, (u_ref, p_ref, n_ref, o_ref) in enumerate(((uf_ref, pf_ref, nf_ref, hf_ref),
                                                              (ub_ref, pb_ref, nb_ref, hb_ref))):
        c = j if direction == 0 else _mirror(j, n_ctx, n_all)
        has_prev = ((c != 0) & (c != n_ctx)).astype(F32)
        has_next = ((c != n_ctx - 1) & (c != n_all - 1)).astype(F32)
        ext = jnp.concatenate([p_ref[...] * has_prev, u_ref[...], n_ref[...] * has_next], axis=0)
        n_ext = T + 2 * LRU_HALO
        u = cb_ref[...] + cw[2:3] * ext[LRU_HALO:LRU_HALO + T]
        for tap, sh in ((0, 2), (1, 1), (3, n_ext - 1)):
            u = u + cw[tap:tap + 1] * pltpu.roll(ext, sh, 0)[LRU_HALO:LRU_HALO + T]
        ub = u.astype(BF16)
        r = jax.nn.sigmoid(jnp.dot(ub, wa_ref[direction], preferred_element_type=F32) + ba_ref[direction])
        i = jax.nn.sigmoid(jnp.dot(ub, wx_ref[direction], preferred_element_type=F32) + bx_ref[direction])
        lam = lam_ref[direction]
        log_a = (-LRU_C) * r * jnp.log1p(jnp.exp(-lam))
        a = jnp.exp(log_a)
        bb = jnp.sqrt(1.0 - a * a) * (i * u)
        sh = 1
        while sh < T:
            if direction == 0:
                ok = row >= sh
                a_s, b_s = pltpu.roll(a, sh, 0), pltpu.roll(bb, sh, 0)
            else:
                ok = row < T - sh
                a_s, b_s = pltpu.roll(a, T - sh, 0), pltpu.roll(bb, T - sh, 0)
            bb = jnp.where(ok, a * b_s + bb, bb)
            a = jnp.where(ok, a * a_s, a)
            sh *= 2
        hcur = bb + a * carry_ref[direction]
        o_ref[...] = hcur
        last = T - 1 if direction == 0 else 0
        carry_ref[direction] = hcur[last:last + 1, :]


def _lru(zd, p, lc):
    b, s, _ = zd.shape
    n_all, n_ctx = s // LRU_T, lc // LRU_T
    per = LRU_T // LRU_HALO
    n_halo = s // LRU_HALO
    ident = lambda j: j
    mirr = lambda j: _mirror(j, n_ctx, n_all)

    def specs(cf):
        return [pl.BlockSpec((None, LRU_T, LRU_W), lambda bb, j: (bb, cf(j), 0)),
                pl.BlockSpec((None, LRU_HALO, LRU_W), lambda bb, j: (bb, jnp.maximum(cf(j) * per - 1, 0), 0)),
                pl.BlockSpec((None, LRU_HALO, LRU_W), lambda bb, j: (bb, jnp.minimum((cf(j) + 1) * per, n_halo - 1), 0))]

    vec = _whole((2, 1, LRU_W))
    mat = _whole((2, LRU_W, LRU_W))
    return pl.pallas_call(
        functools.partial(_lru_kernel, n_ctx=n_ctx, n_all=n_all),
        grid=(b, n_all),
        in_specs=specs(ident) + specs(mirr) + [_whole((4, LRU_W)), _whole((1, LRU_W)), mat, vec, mat, vec, vec],
        out_specs=[pl.BlockSpec((None, LRU_T, LRU_W), lambda bb, j: (bb, j, 0)),
                   pl.BlockSpec((None, LRU_T, LRU_W), lambda bb, j: (bb, mirr(j), 0))],
        out_shape=[jax.ShapeDtypeStruct((b, s, LRU_W), F32)] * 2,
        scratch_shapes=[pltpu.VMEM((2, 1, LRU_W), F32)],
        compiler_params=_cp(("parallel", "arbitrary"), VMEM_LIMIT),
        name="rglru_scan",
    )(zd, zd, zd, zd, zd, zd, p["conv_w"], p["conv_b"], p["wa"], p["ba"], p["wx"], p["bx"], p["lam"])


def _out_kernel(x_ref, gate_ref, hfa_ref, hba_ref, o_ref, gn_ref, yb_ref, yc_ref, hfd_ref, hbd_ref, gd_ref, w_ref, out_ref):
    pair = [(0, ML_DH), (ML_DH, 2 * ML_DH)]
    gm_pair = _group_mean_matrix(LANE, pair)
    ha = hfa_ref[...] + hba_ref[...]
    ha = jnp.concatenate([_group_rms(ha[:, :LANE], pair, gm_pair), _group_rms(ha[:, LANE:], pair, gm_pair)],
                         axis=-1) * gn_ref[...]
    ya = (jax.nn.sigmoid(o_ref[...].astype(F32)) * ha).astype(BF16)
    yd = (jax.nn.gelu(gd_ref[...]) * (hfd_ref[...] + hbd_ref[...])).astype(BF16)
    acc = jnp.dot(ya, w_ref[0:ML_W, :], preferred_element_type=F32)
    acc = acc + jnp.dot(yb_ref[...], w_ref[ML_W:2 * ML_W, :], preferred_element_type=F32)
    acc = acc + jnp.dot(yc_ref[...], w_ref[2 * ML_W:3 * ML_W, :], preferred_element_type=F32)
    acc = acc + jnp.dot(yd, w_ref[3 * ML_W:4 * ML_W, :], preferred_element_type=F32)
    out_ref[...] = x_ref[...] + gate_ref[...] * acc


def _out_proj(xa, mod, hfa, hba, za, gn, yb, yc, hfd, hbd, zd, w, nb, ctx_tiles, off):
    b, s, _ = xa.shape
    nt = s // TM - off
    r = lambda w_, col=0: _rows(w_, off, col)
    return pl.pallas_call(
        _out_kernel,
        grid=(b, nt),
        in_specs=[r(D), _mod_spec(2, nb, ctx_tiles, off), r(ML_W), r(ML_W), r(ML_W, 3), _whole((1, ML_W)),
                  r(ML_W), r(ML_W), r(ML_W), r(ML_W), r(ML_W, 1), _whole((D, D))],
        out_specs=_rows(D),
        out_shape=jax.ShapeDtypeStruct((b, nt * TM, D), F32),
        compiler_params=_cp(("parallel", "parallel"), VMEM_LIMIT),
        name="out_proj",
    )(xa, mod, hfa, hba, za, gn, yb, yc, hfd, hbd, zd, w)


def _ffn_kernel(x_ref, g_ref, sh_ref, sc_ref, gate_ref, w13_ref, w2_ref, out_ref, *, d_ff):
    x = x_ref[...]
    h = _rms(x) * g_ref[...]
    hb = (h * (1.0 + sc_ref[...]) + sh_ref[...]).astype(BF16)
    a = jnp.dot(hb, w13_ref[:, :d_ff], preferred_element_type=F32)
    g = jnp.dot(hb, w13_ref[:, d_ff:], preferred_element_type=F32)
    u = (g * jax.nn.sigmoid(g) * a).astype(BF16)
    out_ref[...] = x + gate_ref[...] * jnp.dot(u, w2_ref[...], preferred_element_type=F32)


def _ffn(xa, gain, mod, w13, w2, nb, ctx_tiles):
    b, s, _ = xa.shape
    d_ff = w2.shape[0]
    return pl.pallas_call(
        functools.partial(_ffn_kernel, d_ff=d_ff),
        grid=(b, s // TM),
        in_specs=[_rows(D), _whole((1, D)), _mod_spec(3, nb, ctx_tiles, 0), _mod_spec(4, nb, ctx_tiles, 0),
                  _mod_spec(5, nb, ctx_tiles, 0), _whole((D, 2 * d_ff)), _whole((d_ff, D))],
        out_specs=_rows(D),
        out_shape=jax.ShapeDtypeStruct((b, s, D), F32),
        compiler_params=_cp(("parallel", "parallel"), VMEM_LIMIT),
        name="ffn",
    )(xa, gain.reshape(1, D), mod, mod, mod, w13, w2)


META_E, META_RANK, META_GATE = 0, 2, 4


def _router_kernel(x_ref, g_ref, sh_ref, sc_ref, wr_ref, br_ref, h_ref, meta_ref, cnt_ref):
    @pl.when((pl.program_id(0) == 0) & (pl.program_id(1) == 0))
    def _():
        cnt_ref[...] = jnp.zeros_like(cnt_ref)

    h = _rms(x_ref[...]) * g_ref[...]
    h = h * (1.0 + sc_ref[...]) + sh_ref[...]
    h_ref[...] = h
    logits = jnp.dot(h, wr_ref[...], precision=HI, preferred_element_type=F32) + br_ref[...]
    lane = _lane(logits.shape)
    logits = jnp.where(lane < N_EXP, logits, NEG)
    m1 = jnp.max(logits, axis=-1, keepdims=True)
    i1 = jnp.min(jnp.where(logits == m1, lane, LANE), axis=-1, keepdims=True)
    rest = jnp.where(lane == i1, NEG, logits)
    m2 = jnp.max(rest, axis=-1, keepdims=True)
    i2 = jnp.min(jnp.where(rest == m2, lane, LANE), axis=-1, keepdims=True)
    e2 = jnp.exp(m2 - m1)
    inv = 1.0 / (1.0 + e2)
    hit1, hit2 = lane == i1, lane == i2
    assign = (hit1 | hit2).astype(F32)
    rr = lax.broadcasted_iota(jnp.int32, (TM, TM), 0)
    cc = lax.broadcasted_iota(jnp.int32, (TM, TM), 1)
    before = jnp.dot((cc < rr).astype(BF16), assign.astype(BF16), preferred_element_type=F32) + cnt_ref[...]
    r1 = jnp.sum(jnp.where(hit1, before, 0.0), axis=-1, keepdims=True)
    r2 = jnp.sum(jnp.where(hit2, before, 0.0), axis=-1, keepdims=True)
    cnt_ref[...] = cnt_ref[...] + jnp.sum(assign, axis=0, keepdims=True)
    fields = (i1.astype(F32), i2.astype(F32), r1, r2, inv, e2 * inv)
    meta = jnp.zeros(logits.shape, F32)
    for k, val in enumerate(fields):
        meta = jnp.where(lane == k, val, meta)
    meta_ref[...] = meta


def _router(xl, gain, mod, wr, br):
    b, t, _ = xl.shape
    spec = lambda k: pl.BlockSpec((None, 1, D), lambda bb, j: (bb, 0, k))
    return pl.pallas_call(
        _router_kernel,
        grid=(b, t // TM),
        in_specs=[_rows(D), _whole((1, D)), spec(3), spec(4), _whole((D, LANE)), _whole((1, LANE))],
        out_specs=[_rows(D), _rows(LANE), _whole((1, LANE))],
        out_shape=[jax.ShapeDtypeStruct((b, t, D), F32), jax.ShapeDtypeStruct((b, t, LANE), F32),
                   jax.ShapeDtypeStruct((1, LANE), F32)],
        compiler_params=_cp(("arbitrary", "arbitrary"), VMEM_LIMIT),
        name="router",
    )(xl, gain.reshape(1, D), mod, mod, wr, br)


TG = 256
TOP_K = 2


ROW_DMA_UNROLL = 8


def _row_dmas(route_ref, start_ref, make_copy, tile_copies, sem):
    def issue(r, c):
        for k in range(TOP_K):
            pos = start_ref[route_ref[0, (META_E + k) * TM + r]] + route_ref[0, (META_RANK + k) * TM + r]
            make_copy(r, k, pos, sem).start()
        return c

    lax.fori_loop(0, TM, issue, 0, unroll=ROW_DMA_UNROLL)
    for cp in tile_copies:
        cp.wait()


def _dispatch_kernel(route_ref, start_ref, fill_ref, h_ref, xs_ref, zero_ref, sem, fill_sem):
    @pl.when(pl.program_id(0) == 0)
    def _():
        zero_ref[...] = jnp.zeros_like(zero_ref)
        zero_row = lambda pos: pltpu.make_async_copy(zero_ref.at[pl.ds(0, 1)], xs_ref.at[pl.ds(pos, 1)], fill_sem)
        for e in range(N_EXP + 1):
            first, count = fill_ref[e], fill_ref[N_EXP + 1 + e]
            lax.fori_loop(0, count, lambda i, c: (zero_row(first + i).start(), c)[1], 0)
            lax.fori_loop(0, count, lambda i, c: (zero_row(first).wait(), c)[1], 0)

    whole = pltpu.make_async_copy(h_ref, xs_ref.at[pl.ds(0, TM)], sem)
    _row_dmas(route_ref, start_ref,
              lambda r, k, pos, s: pltpu.make_async_copy(h_ref.at[pl.ds(r, 1)], xs_ref.at[pl.ds(pos, 1)], s),
              [whole] * TOP_K, sem)


def _dispatch(h2, route, start, fill, n_rows):
    n = h2.shape[0]
    return pl.pallas_call(
        _dispatch_kernel,
        grid=(n // TM,),
        in_specs=[pl.BlockSpec((None, 1, 2 * TOP_K * TM), lambda i: (i, 0, 0), memory_space=pltpu.SMEM),
                  pl.BlockSpec(memory_space=pltpu.SMEM),
                  pl.BlockSpec(memory_space=pltpu.SMEM),
                  pl.BlockSpec((TM, D), lambda i: (i, 0))],
        out_specs=pl.BlockSpec(memory_space=pl.ANY),
        out_shape=jax.ShapeDtypeStruct((n_rows, D), F32),
        scratch_shapes=[pltpu.VMEM((SUB, D), F32), pltpu.SemaphoreType.DMA(()), pltpu.SemaphoreType.DMA(())],
        compiler_params=_cp(("arbitrary",), VMEM_LIMIT),
        name="moe_dispatch",
    )(route, start, fill, h2)


def _expert_kernel(te_ref, nu_ref, xs_ref, w13_ref, w2_ref, y_ref, *, d_ff):
    del te_ref
    t = pl.program_id(0)

    @pl.when(t < nu_ref[0])
    def _():
        xb = xs_ref[...].astype(BF16)
        a = jnp.dot(xb, w13_ref[:, :d_ff], preferred_element_type=F32)
        g = jnp.dot(xb, w13_ref[:, d_ff:], preferred_element_type=F32)
        u = (g * jax.nn.sigmoid(g) * a).astype(BF16)
        y_ref[...] = jnp.dot(u, w2_ref[...], preferred_element_type=F32)

    @pl.when(t >= nu_ref[0])
    def _():
        y_ref[...] = jnp.zeros_like(y_ref)


def _experts(xs, tile_expert, n_used, w13, w2):
    n_rows = xs.shape[0]
    d_ff = w2.shape[1]
    once = dict(pipeline_mode=pl.Buffered(1))
    grid_spec = pltpu.PrefetchScalarGridSpec(
        num_scalar_prefetch=2,
        grid=(n_rows // TG,),
        in_specs=[pl.BlockSpec((TG, D), lambda t, te, nu: (jnp.minimum(t, nu[0] - 1), 0)),
                  pl.BlockSpec((None, D, 2 * d_ff), lambda t, te, nu: (te[t], 0, 0), **once),
                  pl.BlockSpec((None, d_ff, D), lambda t, te, nu: (te[t], 0, 0), **once)],
        out_specs=pl.BlockSpec((TG, D), lambda t, te, nu: (t, 0)),
    )
    return pl.pallas_call(
        functools.partial(_expert_kernel, d_ff=d_ff),
        grid_spec=grid_spec,
        out_shape=jax.ShapeDtypeStruct((n_rows, D), F32),
        compiler_params=_cp(("arbitrary",), VMEM_LIMIT),
        name="moe_experts",
    )(tile_expert, n_used, xs, w13, w2)


def _combine_kernel(route_ref, start_ref, x_ref, meta_ref, gate_ref, y_ref, out_ref, buf_ref, sem):
    _row_dmas(route_ref, start_ref,
              lambda r, k, pos, s: pltpu.make_async_copy(y_ref.at[pl.ds(pos, 1)], buf_ref.at[k, pl.ds(r, 1)], s),
              [pltpu.make_async_copy(y_ref.at[pl.ds(0, TM)], buf_ref.at[k], sem) for k in range(TOP_K)], sem)
    meta = meta_ref[...]
    mix = meta[:, META_GATE:META_GATE + 1] * buf_ref[0] + meta[:, META_GATE + 1:META_GATE + 2] * buf_ref[1]
    out_ref[...] = x_ref[...] + gate_ref[...] * mix


def _combine(xl, meta, mod, y, route, start):
    b, t, _ = xl.shape
    tiles = t // TM
    return pl.pallas_call(
        _combine_kernel,
        grid=(b, tiles),
        in_specs=[pl.BlockSpec((None, 1, 2 * TOP_K * TM), lambda bb, j: (bb * tiles + j, 0, 0), memory_space=pltpu.SMEM),
                  pl.BlockSpec(memory_space=pltpu.SMEM),
                  _rows(D), _rows(LANE), pl.BlockSpec((None, 1, D), lambda bb, j: (bb, 0, 5)),
                  pl.BlockSpec(memory_space=pl.ANY)],
        out_specs=_rows(D),
        out_shape=jax.ShapeDtypeStruct((b, t, D), F32),
        scratch_shapes=[pltpu.VMEM((TOP_K, TM, D), F32), pltpu.SemaphoreType.DMA(())],
        compiler_params=_cp(("arbitrary", "arbitrary"), VMEM_LIMIT),
        name="moe_combine",
    )(route, start, xl, meta, mod, y)


def _moe(xl, gain, mod, wr, br, w13, w2):
    b, t, _ = xl.shape
    n = b * t
    h, meta, cnt = _router(xl, gain, mod, wr, br)
    counts = cnt[0, :N_EXP].astype(jnp.int32)
    padded = (counts + TG - 1) // TG * TG
    ends = jnp.cumsum(padded)
    start = ends - padded
    n_tiles = TOP_K * n // TG + N_EXP
    tile_lo = jnp.arange(n_tiles, dtype=jnp.int32) * TG
    tile_expert = jnp.minimum(jnp.sum((tile_lo[:, None] >= ends[None, :]).astype(jnp.int32), axis=1), N_EXP - 1)
    n_used = (ends[-1:] // TG).astype(jnp.int32)
    ids = meta.reshape(n // TM, TM, LANE)[:, :, :2 * TOP_K].astype(jnp.int32)
    route = jnp.swapaxes(ids, 1, 2).reshape(n // TM, 1, 2 * TOP_K * TM)
    n_rows = n_tiles * TG
    fill = jnp.concatenate([start + counts, ends[-1:], padded - counts, n_rows - ends[-1:]])
    xs = _dispatch(h.reshape(n, D), route, start, fill, n_rows)
    y = _experts(xs, tile_expert, n_used, w13, w2)
    return _combine(xl, meta, mod, y, route, start)


def _pad_cols(w, n):
    return jnp.pad(w, ((0, 0), (0, n - w.shape[1])))


def _layout_w_in(w):
    ml = 4 * ML_W
    a, gates = w[:, :ml], w[:, ml:ml + 4 * ML_H]
    o = ml + 4 * ML_H
    cq, ckv, kr = w[:, o:o + MLA_QR], w[:, o + MLA_QR:o + MLA_QR + MLA_KVR], w[:, o + MLA_QR + MLA_KVR:o + MLA_QR + MLA_KVR + MLA_ROPE]
    o += MLA_QR + MLA_KVR + MLA_ROPE
    sw = w[:, o:o + (SW_H + 2 * SW_KV) * SW_DH]
    o += (SW_H + 2 * SW_KV) * SW_DH
    lru = w[:, o:]
    z = lambda n: jnp.zeros((w.shape[0], n), w.dtype)
    cols = [a, _pad_cols(gates, LANE), _pad_cols(cq, 2 * LANE), ckv, z(MLA_NOPE), kr, z(LANE - MLA_NOPE - MLA_ROPE), sw, lru]
    return jnp.concatenate(cols, axis=1).astype(BF16)


IN_SEGS = ((0, 4 * ML_W), (4 * ML_W, LANE), (4 * ML_W + LANE, 4 * LANE), (4 * ML_W + 5 * LANE, 4 * LANE),
           (4 * ML_W + 9 * LANE, 4 * LANE))
IN_DTYPES = (BF16, F32, F32, F32, F32)


def _row(v, n=None):
    v = v.reshape(1, -1)
    return v if n is None else _pad_cols(v, n)


def _layer_params(l, ml_gate_b, ml_out_norm, mla_q_norm, mla_w_uq, mla_kv_norm, mla_w_ukv, mla_q_gain, mla_k_gain,
                  sw_q_gain, sw_k_gain, lru_conv_w, lru_conv_b, lru_wa, lru_ba, lru_wx, lru_bx, lru_lam):
    hq = MLA_NOPE + MLA_ROPE
    wq = mla_w_uq[l].reshape(MLA_QR, MLA_H, hq)
    wq = jnp.pad(wq, ((0, 2 * LANE - MLA_QR), (0, 0), (0, LANE - hq))).reshape(2 * LANE, MLA_H * LANE)
    wkv = mla_w_ukv[l].reshape(MLA_KVR, MLA_H, MLA_NOPE + MLA_V)
    wk = jnp.pad(wkv[:, :, :MLA_NOPE], ((0, 0), (0, 0), (0, LANE - MLA_NOPE))).reshape(MLA_KVR, MLA_H * LANE)
    wv = wkv[:, :, MLA_NOPE:].reshape(MLA_KVR, MLA_H * MLA_V)

    def blockdiag(wb):
        bw = LRU_W // LRU_BLOCKS
        eye = jnp.eye(LRU_BLOCKS, dtype=wb.dtype)
        return jnp.einsum('xncd,nm->xncmd', wb, eye).reshape(2, LRU_W, LRU_W)

    return dict(
        gate_b=_row(ml_gate_b[l], LANE), out_norm=_row(ml_out_norm[l]),
        q_norm=_row(mla_q_norm[l], 2 * LANE), w_uq=wq.astype(BF16), kv_norm=_row(mla_kv_norm[l]),
        w_uk=wk.astype(BF16), w_uv=wv.astype(BF16),
        q_gain=_row(mla_q_gain[l], LANE), k_gain=_row(mla_k_gain[l], LANE),
        sw_q_gain=_row(jnp.tile(sw_q_gain[l], 2)), sw_k_gain=_row(jnp.tile(sw_k_gain[l], 2)),
        conv_w=lru_conv_w[l], conv_b=_row(lru_conv_b[l]),
        wa=blockdiag(lru_wa[l]).astype(BF16), ba=lru_ba[l].reshape(2, 1, LRU_W),
        wx=blockdiag(lru_wx[l]).astype(BF16), bx=lru_bx[l].reshape(2, 1, LRU_W),
        lam=lru_lam[l].reshape(2, 1, LRU_W),
    )


def _mix_layer(xa, mod, gain, w_in_l, w_out_l, sink, p, tabs, lc, nb, ctx_out):
    ctx_tiles = lc // TM
    za, zg, zb, zc, zd = _proj(xa, gain, mod, w_in_l, IN_SEGS, IN_DTYPES, nb, ctx_tiles)
    hfa, hba = _mlstm(za, zg, p["gate_b"], lc)
    qm, km, vm, qs, ks, vs = _prep(zb, zc, tabs[0], tabs[1], p)
    yb = _mla_attn(qm, km, vm, lc)
    yc = _swa_attn(qs, ks, vs, sink, lc)
    hfd, hbd = _lru(zd, p, lc)
    off = 0 if ctx_out else ctx_tiles
    return _out_proj(xa, mod, hfa, hba, za, p["out_norm"], yb, yc, hfd, hbd, zd, w_out_l, nb, ctx_tiles, off)


def kernel(x, c, ctx, c_ctx, ada_w, ada_b, norm_mix, norm_ffn, w_in, w_out, ml_gate_b, ml_out_norm, mla_q_norm, mla_w_uq, mla_kv_norm, mla_w_ukv, mla_q_gain, mla_k_gain, sw_q_gain, sw_k_gain, sw_sink, lru_conv_w, lru_conv_b, lru_wa, lru_ba, lru_wx, lru_bx, lru_lam, ffn_w13, ffn_w2, moe_router, moe_router_b, moe_w13, moe_w2):
    nb, t_len, _ = x.shape
    lc = ctx.shape[1]
    depth = ada_w.shape[0]
    assert nb < SUB and lc % TM == 0 and t_len % TM == 0 and t_len % GRID_W == 0
    cc = jnp.zeros((SUB, D), F32).at[:nb].set(c).at[nb].set(c_ctx)
    mods = _ada_mod(cc, ada_w, ada_b)
    tabs = (_rope_tables(t_len, lc, LANE, MLA_ROPE, MLA_NOPE), _rope_tables(t_len, lc, SW_DH, SW_DH, 0))
    xa = jnp.concatenate([ctx, x], axis=1)
    for l in range(depth):
        ctx_out = l < depth - 1
        mod = mods[l].reshape(SUB, 1, 6 * D)
        p = _layer_params(l, ml_gate_b, ml_out_norm, mla_q_norm, mla_w_uq, mla_kv_norm, mla_w_ukv, mla_q_gain,
                          mla_k_gain, sw_q_gain, sw_k_gain, lru_conv_w, lru_conv_b, lru_wa, lru_ba, lru_wx, lru_bx,
                          lru_lam)
        xa = _mix_layer(xa, mod, norm_mix[l], _layout_w_in(w_in[l]), w_out[l].astype(BF16), sw_sink[l], p, tabs,
                        lc, nb, ctx_out)
        if l % 2 == 0:
            xa = _ffn(xa, norm_ffn[l], mod, ffn_w13[l // 2].astype(BF16), ffn_w2[l // 2].astype(BF16), nb,
                      lc // TM if ctx_out else 0)
        else:
            xa = _moe(xa, norm_ffn[l], mod, _pad_cols(moe_router[l // 2], LANE), _row(moe_router_b[l // 2], LANE),
                      moe_w13[l // 2].astype(BF16), moe_w2[l // 2].astype(BF16))
    return xa
```

```python
import functools
import itertools

import numpy as np
import jax
import jax.numpy as jnp
from jax import lax
from jax.experimental import pallas as pl
from jax.experimental.pallas import tpu as pltpu

F32 = jnp.float32
BF16 = jnp.bfloat16
HI = lax.Precision.HIGHEST

D = 1024
GRID_W = 64
EPS = 1e-6
ROPE_BASE = 10000.0
ML_H, ML_DH, ML_W, ML_CHUNK = 4, 64, 256, 128
MLA_H, MLA_QR, MLA_KVR, MLA_NOPE, MLA_ROPE, MLA_V = 4, 192, 128, 64, 32, 64
SW_H, SW_KV, SW_DH, SW_WINDOW, SW_BLOCK = 4, 2, 64, 128, 128
LRU_W, LRU_BLOCKS, LRU_C = 256, 4, 8.0
N_EXP = 8

LANE = 128
SUB = 8
TM = 256
NEG = -1e30
VMEM_LIMIT = 56 * 1024 * 1024


def _cp(sem, vmem=None):
    return pltpu.CompilerParams(dimension_semantics=sem, vmem_limit_bytes=vmem)


def _mod_spec(chunk, nb, ctx_tiles, off):
    return pl.BlockSpec((None, 1, D), lambda b, j: (jnp.where(j + off < ctx_tiles, nb, b), 0, chunk))


def _rows(width, off=0, col=0, tm=TM):
    return pl.BlockSpec((None, tm, width), lambda b, j: (b, j + off, col))


def _whole(shape):
    nd = len(shape)
    return pl.BlockSpec(shape, lambda *_: (0,) * nd)


def _rms(x):
    return x * lax.rsqrt(jnp.mean(x * x, axis=-1, keepdims=True) + EPS)


def _lane(shape):
    return lax.broadcasted_iota(jnp.int32, shape, len(shape) - 1)


def _group_mean_matrix(n, bounds):
    i = lax.broadcasted_iota(jnp.int32, (n, n), 0)
    j = lax.broadcasted_iota(jnp.int32, (n, n), 1)
    g = jnp.zeros((n, n), F32)
    for lo, hi in bounds:
        g = jnp.where((i >= lo) & (i < hi) & (j >= lo) & (j < hi), 1.0 / (hi - lo), g)
    return g.astype(BF16)


def _group_rms(blk, bounds, gm):
    lane = _lane(blk.shape)
    sq = blk * blk
    hi_part = sq.astype(BF16)
    lo_part = (sq - hi_part.astype(F32)).astype(BF16)
    ms = (jnp.dot(hi_part, gm, preferred_element_type=F32) + jnp.dot(lo_part, gm, preferred_element_type=F32))
    inside = functools.reduce(jnp.logical_or, [(lane >= lo) & (lane < hi) for lo, hi in bounds])
    return blk * jnp.where(inside, lax.rsqrt(ms + EPS), 0.0)


def _rope(blk, tab_ref, shift):
    n = blk.shape[-1]
    return (blk * tab_ref[0] + pltpu.roll(blk, n - shift, 1) * tab_ref[1]
            + pltpu.roll(blk, shift, 1) * tab_ref[2])


def _rope_tables(t_len, lc, width, dims, offset):
    half, nf = dims // 2, dims // 4
    p = np.arange(dims)
    i = p % half
    f = i % nf
    first = i < nf
    freq = ROPE_BASE ** (-f.astype(np.float64) / nf)
    t = np.arange(t_len)
    pos = np.where(p[None, :] < half, (t // GRID_W)[:, None], (t % GRID_W)[:, None]).astype(np.float64)
    ang = pos * freq[None, :]
    cos, sin = np.cos(ang), np.sin(ang)
    tab = np.zeros((3, lc + t_len, width), np.float64)
    tab[0] = 1.0
    tab[0, lc:, offset:offset + dims] = cos
    tab[1, lc:, offset:offset + dims] = np.where(first[None, :], -sin, 0.0)
    tab[2, lc:, offset:offset + dims] = np.where(first[None, :], 0.0, sin)
    reps = LANE // width
    return jnp.asarray(np.tile(tab, (1, 1, reps)), F32)


def _ada_kernel(c_ref, w_ref, b_ref, o_ref):
    c = c_ref[...]
    s = c * jax.nn.sigmoid(c)
    o_ref[...] = jnp.dot(s, w_ref[...], precision=HI, preferred_element_type=F32) + b_ref[...]


def _ada_mod(cc, ada_w, ada_b):
    depth = ada_w.shape[0]
    n = ada_w.shape[2]
    tn = 1024
    return pl.pallas_call(
        _ada_kernel,
        grid=(depth, n // tn),
        in_specs=[
            pl.BlockSpec((SUB, D), lambda l, j: (0, 0)),
            pl.BlockSpec((None, D, tn), lambda l, j: (l, 0, j)),
            pl.BlockSpec((None, 1, tn), lambda l, j: (l, 0, j)),
        ],
        out_specs=pl.BlockSpec((None, SUB, tn), lambda l, j: (l, 0, j)),
        out_shape=jax.ShapeDtypeStruct((depth, SUB, n), F32),
        compiler_params=_cp(("parallel", "parallel")),
        name="ada_mod",
    )(cc, ada_w, ada_b.reshape(depth, 1, n))


def _proj_kernel(x_ref, g_ref, sh_ref, sc_ref, w_ref, *out_refs, segs):
    h = _rms(x_ref[...]) * g_ref[...]
    h = h * (1.0 + sc_ref[...]) + sh_ref[...]
    hb = h.astype(BF16)
    for (start, width), o_ref in zip(segs, out_refs):
        o_ref[...] = jnp.dot(hb, w_ref[:, start:start + width], preferred_element_type=F32).astype(o_ref.dtype)


def _proj(xa, gain, mod, w, segs, dtypes, nb, ctx_tiles):
    b, s, _ = xa.shape
    n = w.shape[1]
    return pl.pallas_call(
        functools.partial(_proj_kernel, segs=segs),
        grid=(b, s // TM),
        in_specs=[
            _rows(D),
            _whole((1, D)),
            _mod_spec(0, nb, ctx_tiles, 0),
            _mod_spec(1, nb, ctx_tiles, 0),
            _whole((D, n)),
        ],
        out_specs=[_rows(wd) for _, wd in segs],
        out_shape=[jax.ShapeDtypeStruct((b, s, wd), dt) for (_, wd), dt in zip(segs, dtypes)],
        compiler_params=_cp(("parallel", "parallel"), VMEM_LIMIT),
        name="in_proj",
    )(xa, gain.reshape(1, D), mod, mod, w)


def _prep_kernel(zb_ref, zc_ref, tm_ref, ts_ref, qn_ref, wuq_ref, kvn_ref, wk_ref, wv_ref, qg_ref, kg_ref,
                 sqg_ref, skg_ref, qm_ref, km_ref, vm_ref, qs_ref, ks_ref, vs_ref):
    zb = zb_ref[...]
    cq = zb[:, :2 * LANE]
    cqn = cq * lax.rsqrt(jnp.sum(cq * cq, axis=-1, keepdims=True) * (1.0 / MLA_QR) + EPS) * qn_ref[...]
    qf = jnp.dot(cqn.astype(BF16), wuq_ref[...], preferred_element_type=F32)
    q_scale = (MLA_NOPE + MLA_ROPE) ** -0.5 * LOG2E
    head_bounds = [(0, MLA_NOPE), (MLA_NOPE, MLA_NOPE + MLA_ROPE)]
    gm_head = _group_mean_matrix(LANE, head_bounds)
    for h in range(MLA_H):
        blk = _group_rms(qf[:, h * LANE:(h + 1) * LANE], head_bounds, gm_head) * qg_ref[...]
        qm_ref[h] = (_rope(blk, tm_ref, MLA_ROPE // 4) * q_scale).astype(BF16)
    ckvn = _rms(zb[:, 2 * LANE:3 * LANE]) * kvn_ref[...]
    ckvb = ckvn.astype(BF16)
    kf = jnp.dot(ckvb, wk_ref[...], preferred_element_type=F32)
    vf = jnp.dot(ckvb, wv_ref[...], preferred_element_type=F32)
    kg = kg_ref[...]
    kr = _group_rms(zb[:, 3 * LANE:4 * LANE], head_bounds[1:], gm_head) * kg
    kr = _rope(kr, tm_ref, MLA_ROPE // 4)
    for h in range(MLA_H):
        kn = _group_rms(kf[:, h * LANE:(h + 1) * LANE], head_bounds[:1], gm_head) * kg
        km_ref[h] = (kn + kr).astype(BF16)
        vm_ref[h] = vf[:, h * MLA_V:(h + 1) * MLA_V].astype(BF16)
    zc = zc_ref[...]
    pair = [(0, SW_DH), (SW_DH, 2 * SW_DH)]
    gm_pair = _group_mean_matrix(LANE, pair)
    sw_scale = SW_DH ** -0.5
    for half in range(2):
        blk = _group_rms(zc[:, half * LANE:(half + 1) * LANE], pair, gm_pair) * sqg_ref[...]
        qs_ref[:, half * LANE:(half + 1) * LANE] = (_rope(blk, ts_ref, SW_DH // 4) * sw_scale).astype(BF16)
    kb = _rope(_group_rms(zc[:, 2 * LANE:3 * LANE], pair, gm_pair) * skg_ref[...], ts_ref, SW_DH // 4).astype(BF16)
    vb = zc[:, 3 * LANE:4 * LANE].astype(BF16)
    for j in range(SW_KV):
        ks_ref[j] = kb[:, j * SW_DH:(j + 1) * SW_DH]
        vs_ref[j] = vb[:, j * SW_DH:(j + 1) * SW_DH]


def _prep(zb, zc, tab_mla, tab_sw, p):
    b, s, _ = zb.shape
    tab = pl.BlockSpec((3, TM, LANE), lambda bb, j: (0, j, 0))
    heads = lambda nh, w: pl.BlockSpec((None, nh, TM, w), lambda bb, j: (bb, 0, j, 0))
    return pl.pallas_call(
        _prep_kernel,
        grid=(b, s // TM),
        in_specs=[_rows(4 * LANE), _rows(4 * LANE), tab, tab,
                  _whole((1, 2 * LANE)), _whole((2 * LANE, 4 * LANE)), _whole((1, LANE)),
                  _whole((LANE, 4 * LANE)), _whole((LANE, 2 * LANE)), _whole((1, LANE)), _whole((1, LANE)),
                  _whole((1, LANE)), _whole((1, LANE))],
        out_specs=[heads(MLA_H, LANE), heads(MLA_H, LANE), heads(MLA_H, MLA_V),
                   _rows(2 * LANE), heads(SW_KV, SW_DH), heads(SW_KV, SW_DH)],
        out_shape=[jax.ShapeDtypeStruct((b, MLA_H, s, LANE), BF16),
                   jax.ShapeDtypeStruct((b, MLA_H, s, LANE), BF16),
                   jax.ShapeDtypeStruct((b, MLA_H, s, MLA_V), BF16),
                   jax.ShapeDtypeStruct((b, s, 2 * LANE), BF16),
                   jax.ShapeDtypeStruct((b, SW_KV, s, SW_DH), BF16),
                   jax.ShapeDtypeStruct((b, SW_KV, s, SW_DH), BF16)],
        compiler_params=_cp(("parallel", "parallel"), VMEM_LIMIT),
        name="head_prep",
    )(zb, zc, tab_mla, tab_sw, p["q_norm"], p["w_uq"], p["kv_norm"], p["w_uk"], p["w_uv"], p["q_gain"],
      p["k_gain"], p["sw_q_gain"], p["sw_k_gain"])


MLA_TQ = 256
MLA_KC = 256
LOG2E = 1.4426950408889634
MLA_VA = MLA_V + 16
MLA_UNROLL = 2


def _mla_kernel(qt_ref, k_ref, vt_ref, o_ref, m_ref, acc_ref, st_ref, cm_ref, p_ref, al_ref, *, n_all, n_ctx,
                ctx_tiles):
    i = pl.program_id(1)
    n_kv = jnp.where(i < ctx_tiles, n_ctx, n_all)
    last = n_kv - 1

    def scores(c, slot):
        off = pl.multiple_of(jnp.minimum(c, last) * MLA_KC, MLA_KC)
        for h in range(MLA_H):
            st = jnp.dot(k_ref[h, pl.ds(off, MLA_KC), :], qt_ref[h], preferred_element_type=F32)
            st_ref[slot, h] = st
            cm_ref[slot, h] = jnp.max(st, axis=0, keepdims=True)

    def softmax(slot, first):
        for h in range(MLA_H):
            cm = cm_ref[slot, h]
            m_new = cm if first else jnp.maximum(m_ref[h], cm)
            p_ref[slot, h] = jnp.exp2(st_ref[slot, h] - m_new).astype(BF16)
            al_ref[slot, h] = jnp.ones_like(cm) if first else jnp.exp2(m_ref[h] - m_new)
            m_ref[h] = m_new

    def weighted_values(c, slot):
        pv = [jnp.dot(vt_ref[h, c], p_ref[slot, h], preferred_element_type=F32) for h in range(MLA_H)]
        return pv

    def accumulate(pv, slot):
        for h in range(MLA_H):
            acc_ref[h] = al_ref[slot, h] * acc_ref[h] + pv[h]

    acc_ref[...] = jnp.zeros_like(acc_ref)
    scores(0, 0)
    softmax(0, True)
    scores(1, 1)

    def body(j, carry):
        for u in range(MLA_UNROLL):
            c, slot = MLA_UNROLL * j + u, u & 1
            pv = weighted_values(c, slot)
            scores(c + 2, slot)
            softmax(1 - slot, False)
            accumulate(pv, slot)
        return carry

    lax.fori_loop(0, last // MLA_UNROLL, body, 0)
    accumulate(weighted_values(last, 0), 0)
    outs = [(acc_ref[h, :MLA_V] / acc_ref[h, MLA_V:MLA_V + 1]).T for h in range(MLA_H)]
    o_ref[...] = jnp.concatenate(outs, axis=-1).astype(o_ref.dtype)


def _mla_attn(q, k, v, lc):
    b, h, s, _ = q.shape
    nc = s // MLA_KC
    assert (nc - 1) % MLA_UNROLL == 0 and lc == MLA_KC
    qt = jnp.swapaxes(q, 2, 3)
    ones = jnp.zeros((b, h, s, MLA_VA - MLA_V), v.dtype).at[..., 0].set(1.0)
    va = jnp.concatenate([v, ones], axis=-1)
    vt = jnp.swapaxes(va.reshape(b, h, nc, MLA_KC, MLA_VA), 3, 4)
    kern = functools.partial(_mla_kernel, n_all=nc, n_ctx=lc // MLA_KC, ctx_tiles=lc // MLA_TQ)
    return pl.pallas_call(
        kern,
        grid=(b, s // MLA_TQ),
        in_specs=[pl.BlockSpec((None, h, LANE, MLA_TQ), lambda bb, i: (bb, 0, 0, i)),
                  pl.BlockSpec((None, h, s, LANE), lambda bb, i: (bb, 0, 0, 0)),
                  pl.BlockSpec((None, h, nc, MLA_VA, MLA_KC), lambda bb, i: (bb, 0, 0, 0, 0))],
        out_specs=pl.BlockSpec((None, MLA_TQ, h * MLA_V), lambda bb, i: (bb, i, 0)),
        out_shape=jax.ShapeDtypeStruct((b, s, h * MLA_V), BF16),
        scratch_shapes=[pltpu.VMEM((h, 1, MLA_TQ), F32),
                        pltpu.VMEM((h, MLA_VA, MLA_TQ), F32), pltpu.VMEM((2, h, MLA_KC, MLA_TQ), F32),
                        pltpu.VMEM((2, h, 1, MLA_TQ), F32),
                        pltpu.VMEM((2, h, MLA_KC, MLA_TQ), BF16), pltpu.VMEM((2, h, 1, MLA_TQ), F32)],
        compiler_params=_cp(("parallel", "arbitrary"), VMEM_LIMIT),
        name="mla_attn",
    )(qt, k, vt)


SW_STEP_BLOCKS = 2


def _swa_kernel(sink_ref, q_ref, k_ref, v_ref, o_ref, *, lc, t_len):
    blk = SW_BLOCK
    nw = blk + 2 * SW_WINDOW
    dn = (((1,), (1,)), ((), ()))
    row = lax.broadcasted_iota(jnp.int32, (2 * blk, nw), 0)
    col = lax.broadcasted_iota(jnp.int32, (2 * blk, nw), 1)
    rows1 = lax.broadcasted_iota(jnp.int32, (2 * blk, 1), 0)
    chains = []
    for part in range(SW_STEP_BLOCKS):
        n = pl.program_id(1) * SW_STEP_BLOCKS + part
        start = pl.multiple_of(jnp.clip((n - 1) * blk, 0, lc + t_len - nw), blk)
        qpos = n * blk - lc + (row & (blk - 1))
        kpos = start - lc + col
        valid = (n * blk >= lc) & (kpos >= 0) & (kpos < t_len) & (jnp.abs(qpos - kpos) <= SW_WINDOW)
        for j in range(SW_KV):
            q2 = q_ref[part * blk:(part + 1) * blk, j * LANE:(j + 1) * LANE]
            qs = jnp.concatenate([q2[:, :SW_DH], q2[:, SW_DH:]], axis=0)
            s_loc = lax.dot_general(qs, k_ref[j, pl.ds(start, nw), :], dn, preferred_element_type=F32)
            s_ctx = lax.dot_general(qs, k_ref[j, pl.ds(0, lc), :], dn, preferred_element_type=F32)
            chains.append(dict(j=j, start=start, valid=valid, s_loc=s_loc, s_ctx=s_ctx))
    for ch in chains:
        j = ch["j"]
        s_loc, s_ctx = jnp.where(ch["valid"], ch["s_loc"], NEG), ch["s_ctx"]
        sink = jnp.where(rows1 < blk, sink_ref[2 * j], sink_ref[2 * j + 1])
        m = jnp.maximum(jnp.maximum(jnp.max(s_loc, axis=-1, keepdims=True),
                                    jnp.max(s_ctx, axis=-1, keepdims=True)), sink)
        ch["e_loc"] = jnp.exp(s_loc - m)
        ch["e_ctx"] = jnp.exp(s_ctx - m)
        ch["den"] = (jnp.sum(ch["e_loc"], axis=-1, keepdims=True) + jnp.sum(ch["e_ctx"], axis=-1, keepdims=True)
                     + jnp.exp(sink - m))
    for ch in chains:
        j = ch["j"]
        ch["o"] = (jnp.dot(ch["e_loc"].astype(BF16), v_ref[j, pl.ds(ch["start"], nw), :], preferred_element_type=F32)
                   + jnp.dot(ch["e_ctx"].astype(BF16), v_ref[j, pl.ds(0, lc), :], preferred_element_type=F32))
    for part in range(SW_STEP_BLOCKS):
        outs = []
        for ch in chains[part * SW_KV:(part + 1) * SW_KV]:
            o = ch["o"] / ch["den"]
            outs += [o[:blk], o[blk:]]
        o_ref[part * blk:(part + 1) * blk, :] = jnp.concatenate(outs, axis=-1).astype(o_ref.dtype)


def _swa_attn(q, k, v, sink, lc):
    b, s, _ = q.shape
    t_len = s - lc
    sp = k.shape[2]
    rows = SW_STEP_BLOCKS * SW_BLOCK
    kern = functools.partial(_swa_kernel, lc=lc, t_len=t_len)
    return pl.pallas_call(
        kern,
        grid=(b, s // rows),
        in_specs=[pl.BlockSpec(memory_space=pltpu.SMEM),
                  pl.BlockSpec((None, rows, SW_KV * LANE), lambda bb, n: (bb, n, 0)),
                  pl.BlockSpec((None, SW_KV, sp, SW_DH), lambda bb, n: (bb, 0, 0, 0)),
                  pl.BlockSpec((None, SW_KV, sp, SW_DH), lambda bb, n: (bb, 0, 0, 0))],
        out_specs=pl.BlockSpec((None, rows, SW_KV * LANE), lambda bb, n: (bb, n, 0)),
        out_shape=jax.ShapeDtypeStruct((b, s, SW_KV * LANE), BF16),
        compiler_params=_cp(("parallel", "arbitrary"), VMEM_LIMIT),
        name="swa_attn",
    )(sink, q, k, v)


def _mirror(j, n_ctx, n_all):
    return jnp.where(j < n_ctx, n_ctx - 1 - j, n_all + n_ctx - 1 - j)


ML_AUG = ML_DH + SUB
ML_ROWS = 4


def _mlstm_kernel(xf_ref, qtf_ref, vtf_ref, gf_ref, xb_ref, qtb_ref, vtb_ref, gb_ref, bias_ref, hf_ref, hb_ref,
                  c_ref, m_ref):
    L = ML_CHUNK

    @pl.when(pl.program_id(1) == 0)
    def _():
        c_ref[...] = jnp.zeros_like(c_ref)
        m_ref[...] = jnp.zeros_like(m_ref)

    row = lax.broadcasted_iota(jnp.int32, (L, L), 0)
    col = lax.broadcasted_iota(jnp.int32, (L, L), 1)
    aug_row = lax.broadcasted_iota(jnp.int32, (SUB, L), 0)
    dirs = ((xf_ref, qtf_ref, vtf_ref, gf_ref, hf_ref), (xb_ref, qtb_ref, vtb_ref, gb_ref, hb_ref))
    chains = []
    for r, (direction, (x_ref, qt_ref, vt_ref, g_ref, o_ref)) in itertools.product(range(ML_ROWS), enumerate(dirs)):
        reach = (row <= col) if direction == 0 else (row >= col)
        gt = (g_ref[r] + bias_ref[...]).T
        brow = jnp.dot(jax.nn.log_sigmoid(gt), reach.astype(F32), precision=HI, preferred_element_type=F32)
        ib = gt - pltpu.roll(brow, L - ML_H, 0)
        ib_t = ib.T
        for h in range(ML_H):
            ch = dict(idx=(r * 2 + direction) * ML_H + h, reach=reach, last=L - 1 if direction == 0 else 0)
            gi, gf = 2 * ML_H * direction + h, 2 * ML_H * direction + ML_H + h
            ch.update(b_row=brow[gf:gf + 1, :], ib_row=ib[gi:gi + 1, :], ib_col=ib_t[:, gi:gi + 1])
            ch["k"] = x_ref[r, :, h * ML_DH:(h + 1) * ML_DH] * (ML_DH ** -0.5)
            ch["qt"] = qt_ref[r, h * ML_DH:(h + 1) * ML_DH, :]
            ch["vt"] = vt_ref[r, h * ML_DH:(h + 1) * ML_DH, :]
            ch["c"], ch["m"] = c_ref[ch["idx"]], m_ref[ch["idx"]]
            chains.append(ch)
    for ch in chains:
        ch["kq"] = jnp.dot(ch["k"], ch["qt"], preferred_element_type=F32)
        ch["cq"] = jnp.dot(ch["c"].astype(BF16), ch["qt"], preferred_element_type=F32)
    for ch in chains:
        b_row, m_st = ch["b_row"], ch["m"]
        d = jnp.where(ch["reach"], b_row + ch["ib_col"], NEG)
        m_inter = b_row + m_st
        m_t = jnp.maximum(m_inter, jnp.max(d, axis=0, keepdims=True))
        ch["w_inter"], ch["m_t"] = jnp.exp(m_inter - m_t), m_t
        ch["st"] = ch["kq"] * jnp.exp(d - m_t)
        b_last = b_row[:, ch["last"]:ch["last"] + 1]
        g_row = b_last + ch["ib_row"]
        m_new = jnp.maximum(b_last + m_st, jnp.max(g_row, axis=-1, keepdims=True))
        w_row = jnp.exp(g_row - m_new)
        ch["decay"], ch["m_new"] = jnp.exp(b_last + m_st - m_new), m_new
        ch["upd"] = jnp.concatenate([ch["vt"].astype(F32) * w_row, jnp.where(aug_row == 0, w_row, 0.0)], axis=0)
    for ch in chains:
        ch["num"] = jnp.dot(ch["vt"], ch["st"].astype(BF16), preferred_element_type=F32)
        ch["dc"] = jnp.dot(ch["upd"].astype(BF16), ch["k"], preferred_element_type=F32)
    outs = []
    for ch in chains:
        num = ch["num"] + ch["w_inter"] * ch["cq"][:ML_DH]
        den = jnp.sum(ch["st"], axis=0, keepdims=True) + ch["w_inter"] * ch["cq"][ML_DH:ML_DH + 1]
        outs.append(num / jnp.maximum(jnp.abs(den), jnp.exp(-ch["m_t"])))
        c_ref[ch["idx"]] = ch["decay"] * ch["c"] + ch["dc"]
        m_ref[ch["idx"]] = ch["m_new"]
    for n, (r, o_ref) in enumerate(itertools.product(range(ML_ROWS), (hf_ref, hb_ref))):
        o_ref[r] = jnp.concatenate(outs[n * ML_H:(n + 1) * ML_H], axis=0).T


def _mlstm(za, zg, bias, lc):
    b, s, _ = za.shape
    n_all, n_ctx = s // ML_CHUNK, lc // ML_CHUNK
    zat = jnp.swapaxes(za, 1, 2)
    ident = lambda j: j
    mirr = lambda j: _mirror(j, n_ctx, n_all)

    def specs(cf):
        rows = lambda w, col: pl.BlockSpec((ML_ROWS, ML_CHUNK, w), lambda bb, j: (bb, cf(j), col))
        feats = lambda blk: pl.BlockSpec((ML_ROWS, ML_W, ML_CHUNK), lambda bb, j: (bb, blk, cf(j)))
        return [rows(ML_W, 1), feats(0), feats(2), rows(LANE, 0)]

    out = lambda cf: pl.BlockSpec((ML_ROWS, ML_CHUNK, ML_W), lambda bb, j: (bb, cf(j), 0))
    assert b % ML_ROWS == 0
    return pl.pallas_call(
        _mlstm_kernel,
        grid=(b // ML_ROWS, n_all),
        in_specs=specs(ident) + specs(mirr) + [_whole((1, LANE))],
        out_specs=[out(ident), out(mirr)],
        out_shape=[jax.ShapeDtypeStruct((b, s, ML_W), F32)] * 2,
        scratch_shapes=[pltpu.VMEM((ML_ROWS * 2 * ML_H, ML_AUG, ML_DH), F32),
                        pltpu.VMEM((ML_ROWS * 2 * ML_H, 1, 1), F32)],
        compiler_params=_cp(("parallel", "arbitrary"), VMEM_LIMIT),
        name="mlstm_scan",
    )(za, zat, zat, zg, za, zat, zat, zg, bias)


LRU_T = 256
LRU_HALO = SUB
LRU_TAPS = 4


def _scan_rows(a, b, reverse):
    axis = a.ndim - 2
    n, sh = a.shape[axis], 1
    idx = lax.broadcasted_iota(jnp.int32, a.shape, axis)
    while sh < n:
        if reverse:
            ok, a_s, b_s = idx < n - sh, pltpu.roll(a, n - sh, axis), pltpu.roll(b, n - sh, axis)
        else:
            ok, a_s, b_s = idx >= sh, pltpu.roll(a, sh, axis), pltpu.roll(b, sh, axis)
        b = jnp.where(ok, a * b_s + b, b)
        a = jnp.where(ok, a * a_s, a)
        sh *= 2
    return a, b


def _lru_kernel(uf_ref, pf_ref, nf_ref, ub_ref, pb_ref, nb_ref, cw_ref, cb_ref, wa_ref, ba_ref, wx_ref, bx_ref,
                lam_ref, hf_ref, hb_ref, carry_ref, sa_ref, sb_ref, ext_ref, *, n_ctx, n_all):
    T = LRU_T
    groups = T // SUB
    j = pl.program_id(1)

    @pl.when(j == 0)
    def _():
        carry_ref[...] = jnp.zeros_like(carry_ref)

    cw = cw_ref[...]
    grow = lax.broadcasted_iota(jnp.int32, (groups, LRU_W), 0)
    for direction, (u_ref, p_ref, n_ref, o_ref) in enumerate(((uf_ref, pf_ref, nf_ref, hf_ref),
                                                              (ub_ref, pb_ref, nb_ref, hb_ref))):
        c = j if direction == 0 else _mirror(j, n_ctx, n_all)
        has_prev = ((c != 0) & (c != n_ctx)).astype(F32)
        has_next = ((c != n_ctx - 1) & (c != n_all - 1)).astype(F32)
        ext_ref[direction, 0:LRU_HALO] = p_ref[...] * has_prev
        ext_ref[direction, LRU_HALO:LRU_HALO + T] = u_ref[...]
        ext_ref[direction, LRU_HALO + T:] = n_ref[...] * has_next
        u = cb_ref[...]
        for tap in range(LRU_TAPS):
            u = u + cw[tap:tap + 1] * ext_ref[direction, pl.ds(LRU_HALO - LRU_TAPS // 2 + tap, T), :]
        ub = u.astype(BF16)
        r = jax.nn.sigmoid(jnp.dot(ub, wa_ref[direction], preferred_element_type=F32) + ba_ref[direction])
        i = jax.nn.sigmoid(jnp.dot(ub, wx_ref[direction], preferred_element_type=F32) + bx_ref[direction])
        lam = lam_ref[direction]
        log_a = (-LRU_C) * r * jnp.log1p(jnp.exp(-lam))
        a = jnp.exp(log_a)
        bb = jnp.sqrt(1.0 - a * a) * (i * u)
        rev = direction == 1
        a, bb = _scan_rows(a.reshape(groups, SUB, LRU_W), bb.reshape(groups, SUB, LRU_W), rev)
        a, bb = a.reshape(T, LRU_W), bb.reshape(T, LRU_W)
        for half in range(LRU_W // LANE):
            sa_ref[direction, half] = a[:, half * LANE:(half + 1) * LANE]
            sb_ref[direction, half] = bb[:, half * LANE:(half + 1) * LANE]
        edge = pl.ds(0 if rev else SUB - 1, groups, stride=SUB)
        ga = jnp.concatenate([sa_ref[direction, half, edge, :] for half in range(LRU_W // LANE)], axis=-1)
        gb = jnp.concatenate([sb_ref[direction, half, edge, :] for half in range(LRU_W // LANE)], axis=-1)
        ga, gb = _scan_rows(ga, gb, rev)
        if rev:
            inner, ga, gb = grow < groups - 1, pltpu.roll(ga, groups - 1, 0), pltpu.roll(gb, groups - 1, 0)
        else:
            inner, ga, gb = grow >= 1, pltpu.roll(ga, 1, 0), pltpu.roll(gb, 1, 0)
        carry = carry_ref[direction]
        h_in = jnp.where(inner, gb + ga * carry, carry)
        h_in = jnp.broadcast_to(h_in[:, None, :], (groups, SUB, LRU_W)).reshape(T, LRU_W)
        hcur = bb + a * h_in
        o_ref[...] = hcur
        last = T - 1 if direction == 0 else 0
        carry_ref[direction] = hcur[last:last + 1, :]


def _lru(zd, p, lc):
    b, s, _ = zd.shape
    n_all, n_ctx = s // LRU_T, lc // LRU_T
    per = LRU_T // LRU_HALO
    n_halo = s // LRU_HALO
    ident = lambda j: j
    mirr = lambda j: _mirror(j, n_ctx, n_all)

    def specs(cf):
        return [pl.BlockSpec((None, LRU_T, LRU_W), lambda bb, j: (bb, cf(j), 0)),
                pl.BlockSpec((None, LRU_HALO, LRU_W), lambda bb, j: (bb, jnp.maximum(cf(j) * per - 1, 0), 0)),
                pl.BlockSpec((None, LRU_HALO, LRU_W), lambda bb, j: (bb, jnp.minimum((cf(j) + 1) * per, n_halo - 1), 0))]

    vec = _whole((2, 1, LRU_W))
    mat = _whole((2, LRU_W, LRU_W))
    return pl.pallas_call(
        functools.partial(_lru_kernel, n_ctx=n_ctx, n_all=n_all),
        grid=(b, n_all),
        in_specs=specs(ident) + specs(mirr) + [_whole((4, LRU_W)), _whole((1, LRU_W)), mat, vec, mat, vec, vec],
        out_specs=[pl.BlockSpec((None, LRU_T, LRU_W), lambda bb, j: (bb, j, 0)),
                   pl.BlockSpec((None, LRU_T, LRU_W), lambda bb, j: (bb, mirr(j), 0))],
        out_shape=[jax.ShapeDtypeStruct((b, s, LRU_W), F32)] * 2,
        scratch_shapes=[pltpu.VMEM((2, 1, LRU_W), F32),
                        pltpu.VMEM((2, LRU_W // LANE, LRU_T, LANE), F32),
                        pltpu.VMEM((2, LRU_W // LANE, LRU_T, LANE), F32),
                        pltpu.VMEM((2, LRU_T + 2 * LRU_HALO, LRU_W), F32)],
        compiler_params=_cp(("parallel", "arbitrary"), VMEM_LIMIT),
        name="rglru_scan",
    )(zd, zd, zd, zd, zd, zd, p["conv_w"], p["conv_b"], p["wa"], p["ba"], p["wx"], p["bx"], p["lam"])


def _out_kernel(x_ref, gate_ref, hfa_ref, hba_ref, o_ref, gn_ref, yb_ref, yc_ref, hfd_ref, hbd_ref, gd_ref, w_ref, out_ref):
    pair = [(0, ML_DH), (ML_DH, 2 * ML_DH)]
    gm_pair = _group_mean_matrix(LANE, pair)
    ha = hfa_ref[...] + hba_ref[...]
    ha = jnp.concatenate([_group_rms(ha[:, :LANE], pair, gm_pair), _group_rms(ha[:, LANE:], pair, gm_pair)],
                         axis=-1) * gn_ref[...]
    ya = (jax.nn.sigmoid(o_ref[...].astype(F32)) * ha).astype(BF16)
    yd = (jax.nn.gelu(gd_ref[...]) * (hfd_ref[...] + hbd_ref[...])).astype(BF16)
    acc = jnp.dot(ya, w_ref[0:ML_W, :], preferred_element_type=F32)
    acc = acc + jnp.dot(yb_ref[...], w_ref[ML_W:2 * ML_W, :], preferred_element_type=F32)
    acc = acc + jnp.dot(yc_ref[...], w_ref[2 * ML_W:3 * ML_W, :], preferred_element_type=F32)
    acc = acc + jnp.dot(yd, w_ref[3 * ML_W:4 * ML_W, :], preferred_element_type=F32)
    out_ref[...] = x_ref[...] + gate_ref[...] * acc


def _out_proj(xa, mod, hfa, hba, za, gn, yb, yc, hfd, hbd, zd, w, nb, ctx_tiles, off):
    b, s, _ = xa.shape
    nt = s // TM - off
    r = lambda w_, col=0: _rows(w_, off, col)
    return pl.pallas_call(
        _out_kernel,
        grid=(b, nt),
        in_specs=[r(D), _mod_spec(2, nb, ctx_tiles, off), r(ML_W), r(ML_W), r(ML_W, 3), _whole((1, ML_W)),
                  r(ML_W), r(ML_W), r(ML_W), r(ML_W), r(ML_W, 1), _whole((D, D))],
        out_specs=_rows(D),
        out_shape=jax.ShapeDtypeStruct((b, nt * TM, D), F32),
        compiler_params=_cp(("parallel", "parallel"), VMEM_LIMIT),
        name="out_proj",
    )(xa, mod, hfa, hba, za, gn, yb, yc, hfd, hbd, zd, w)


def _ffn_kernel(x_ref, g_ref, sh_ref, sc_ref, gate_ref, w13_ref, w2_ref, out_ref, *, d_ff):
    x = x_ref[...]
    h = _rms(x) * g_ref[...]
    hb = (h * (1.0 + sc_ref[...]) + sh_ref[...]).astype(BF16)
    a = jnp.dot(hb, w13_ref[:, :d_ff], preferred_element_type=F32)
    g = jnp.dot(hb, w13_ref[:, d_ff:], preferred_element_type=F32)
    u = (g * jax.nn.sigmoid(g) * a).astype(BF16)
    out_ref[...] = x + gate_ref[...] * jnp.dot(u, w2_ref[...], preferred_element_type=F32)


def _ffn(xa, gain, mod, w13, w2, nb, ctx_tiles):
    b, s, _ = xa.shape
    d_ff = w2.shape[0]
    return pl.pallas_call(
        functools.partial(_ffn_kernel, d_ff=d_ff),
        grid=(b, s // TM),
        in_specs=[_rows(D), _whole((1, D)), _mod_spec(3, nb, ctx_tiles, 0), _mod_spec(4, nb, ctx_tiles, 0),
                  _mod_spec(5, nb, ctx_tiles, 0), _whole((D, 2 * d_ff)), _whole((d_ff, D))],
        out_specs=_rows(D),
        out_shape=jax.ShapeDtypeStruct((b, s, D), F32),
        compiler_params=_cp(("parallel", "parallel"), VMEM_LIMIT),
        name="ffn",
    )(xa, gain.reshape(1, D), mod, mod, mod, w13, w2)


META_E, META_RANK, META_GATE = 0, 2, 4


def _router_kernel(x_ref, g_ref, sh_ref, sc_ref, wr_ref, br_ref, h_ref, meta_ref, cnt_ref):
    @pl.when((pl.program_id(0) == 0) & (pl.program_id(1) == 0))
    def _():
        cnt_ref[...] = jnp.zeros_like(cnt_ref)

    h = _rms(x_ref[...]) * g_ref[...]
    h = h * (1.0 + sc_ref[...]) + sh_ref[...]
    h_ref[...] = h
    logits = jnp.dot(h, wr_ref[...], precision=HI, preferred_element_type=F32) + br_ref[...]
    lane = _lane(logits.shape)
    logits = jnp.where(lane < N_EXP, logits, NEG)
    m1 = jnp.max(logits, axis=-1, keepdims=True)
    i1 = jnp.min(jnp.where(logits == m1, lane, LANE), axis=-1, keepdims=True)
    rest = jnp.where(lane == i1, NEG, logits)
    m2 = jnp.max(rest, axis=-1, keepdims=True)
    i2 = jnp.min(jnp.where(rest == m2, lane, LANE), axis=-1, keepdims=True)
    e2 = jnp.exp(m2 - m1)
    inv = 1.0 / (1.0 + e2)
    hit1, hit2 = lane == i1, lane == i2
    assign = (hit1 | hit2).astype(F32)
    rr = lax.broadcasted_iota(jnp.int32, (TM, TM), 0)
    cc = lax.broadcasted_iota(jnp.int32, (TM, TM), 1)
    before = jnp.dot((cc < rr).astype(BF16), assign.astype(BF16), preferred_element_type=F32) + cnt_ref[...]
    r1 = jnp.sum(jnp.where(hit1, before, 0.0), axis=-1, keepdims=True)
    r2 = jnp.sum(jnp.where(hit2, before, 0.0), axis=-1, keepdims=True)
    cnt_ref[...] = cnt_ref[...] + jnp.sum(assign, axis=0, keepdims=True)
    fields = (i1.astype(F32), i2.astype(F32), r1, r2, inv, e2 * inv)
    meta = jnp.zeros(logits.shape, F32)
    for k, val in enumerate(fields):
        meta = jnp.where(lane == k, val, meta)
    meta_ref[...] = meta


def _router(xl, gain, mod, wr, br):
    b, t, _ = xl.shape
    spec = lambda k: pl.BlockSpec((None, 1, D), lambda bb, j: (bb, 0, k))
    return pl.pallas_call(
        _router_kernel,
        grid=(b, t // TM),
        in_specs=[_rows(D), _whole((1, D)), spec(3), spec(4), _whole((D, LANE)), _whole((1, LANE))],
        out_specs=[_rows(D), _rows(LANE), _whole((1, LANE))],
        out_shape=[jax.ShapeDtypeStruct((b, t, D), F32), jax.ShapeDtypeStruct((b, t, LANE), F32),
                   jax.ShapeDtypeStruct((1, LANE), F32)],
        compiler_params=_cp(("arbitrary", "arbitrary"), VMEM_LIMIT),
        name="router",
    )(xl, gain.reshape(1, D), mod, mod, wr, br)


TG = 256
TOP_K = 2


ROW_DMA_UNROLL = 8


def _row_dmas(route_ref, make_copy, tile_copies, sem):
    def issue(r, c):
        for k in range(TOP_K):
            make_copy(r, k, route_ref[0, k * TM + r], sem).start()
        return c

    lax.fori_loop(0, TM, issue, 0, unroll=ROW_DMA_UNROLL)
    for cp in tile_copies:
        cp.wait()


def _dispatch_kernel(route_ref, fill_ref, h_ref, xs_ref, zero_ref, sem, fill_sem):
    @pl.when(pl.program_id(0) == 0)
    def _():
        zero_ref[...] = jnp.zeros_like(zero_ref)
        zero_row = lambda pos: pltpu.make_async_copy(zero_ref.at[pl.ds(0, 1)], xs_ref.at[pl.ds(pos, 1)], fill_sem)
        for e in range(N_EXP + 1):
            first, count = fill_ref[e], fill_ref[N_EXP + 1 + e]
            lax.fori_loop(0, count, lambda i, c: (zero_row(first + i).start(), c)[1], 0)
            lax.fori_loop(0, count, lambda i, c: (zero_row(first).wait(), c)[1], 0)

    whole = pltpu.make_async_copy(h_ref, xs_ref.at[pl.ds(0, TM)], sem)
    _row_dmas(route_ref,
              lambda r, k, pos, s: pltpu.make_async_copy(h_ref.at[pl.ds(r, 1)], xs_ref.at[pl.ds(pos, 1)], s),
              [whole] * TOP_K, sem)


def _dispatch(h2, route, fill, n_rows):
    n = h2.shape[0]
    return pl.pallas_call(
        _dispatch_kernel,
        grid=(n // TM,),
        in_specs=[pl.BlockSpec((None, 1, TOP_K * TM), lambda i: (i, 0, 0), memory_space=pltpu.SMEM),
                  pl.BlockSpec(memory_space=pltpu.SMEM),
                  pl.BlockSpec((TM, D), lambda i: (i, 0))],
        out_specs=pl.BlockSpec(memory_space=pl.ANY),
        out_shape=jax.ShapeDtypeStruct((n_rows, D), F32),
        scratch_shapes=[pltpu.VMEM((SUB, D), F32), pltpu.SemaphoreType.DMA(()), pltpu.SemaphoreType.DMA(())],
        compiler_params=_cp(("arbitrary",), VMEM_LIMIT),
        name="moe_dispatch",
    )(route, fill, h2)


def _expert_kernel(te_ref, nu_ref, xs_ref, w13_ref, w2_ref, y_ref, *, d_ff):
    del te_ref
    t = pl.program_id(0)

    @pl.when(t < nu_ref[0])
    def _():
        xb = xs_ref[...].astype(BF16)
        a = jnp.dot(xb, w13_ref[:, :d_ff], preferred_element_type=F32)
        g = jnp.dot(xb, w13_ref[:, d_ff:], preferred_element_type=F32)
        u = (g * jax.nn.sigmoid(g) * a).astype(BF16)
        y_ref[...] = jnp.dot(u, w2_ref[...], preferred_element_type=F32)

    @pl.when(t >= nu_ref[0])
    def _():
        y_ref[...] = jnp.zeros_like(y_ref)


def _experts(xs, tile_expert, n_used, w13, w2):
    n_rows = xs.shape[0]
    d_ff = w2.shape[1]
    once = dict(pipeline_mode=pl.Buffered(1))
    grid_spec = pltpu.PrefetchScalarGridSpec(
        num_scalar_prefetch=2,
        grid=(n_rows // TG,),
        in_specs=[pl.BlockSpec((TG, D), lambda t, te, nu: (jnp.minimum(t, nu[0] - 1), 0)),
                  pl.BlockSpec((None, D, 2 * d_ff), lambda t, te, nu: (te[t], 0, 0), **once),
                  pl.BlockSpec((None, d_ff, D), lambda t, te, nu: (te[t], 0, 0), **once)],
        out_specs=pl.BlockSpec((TG, D), lambda t, te, nu: (t, 0)),
    )
    return pl.pallas_call(
        functools.partial(_expert_kernel, d_ff=d_ff),
        grid_spec=grid_spec,
        out_shape=jax.ShapeDtypeStruct((n_rows, D), F32),
        compiler_params=_cp(("arbitrary",), VMEM_LIMIT),
        name="moe_experts",
    )(tile_expert, n_used, xs, w13, w2)


def _combine_kernel(route_ref, x_ref, meta_ref, gate_ref, y_ref, out_ref, buf_ref, sem):
    _row_dmas(route_ref,
              lambda r, k, pos, s: pltpu.make_async_copy(y_ref.at[pl.ds(pos, 1)], buf_ref.at[k, pl.ds(r, 1)], s),
              [pltpu.make_async_copy(y_ref.at[pl.ds(0, TM)], buf_ref.at[k], sem) for k in range(TOP_K)], sem)
    meta = meta_ref[...]
    mix = meta[:, META_GATE:META_GATE + 1] * buf_ref[0] + meta[:, META_GATE + 1:META_GATE + 2] * buf_ref[1]
    out_ref[...] = x_ref[...] + gate_ref[...] * mix


def _combine(xl, meta, mod, y, route):
    b, t, _ = xl.shape
    tiles = t // TM
    return pl.pallas_call(
        _combine_kernel,
        grid=(b, tiles),
        in_specs=[pl.BlockSpec((None, 1, TOP_K * TM), lambda bb, j: (bb * tiles + j, 0, 0), memory_space=pltpu.SMEM),
                  _rows(D), _rows(LANE), pl.BlockSpec((None, 1, D), lambda bb, j: (bb, 0, 5)),
                  pl.BlockSpec(memory_space=pl.ANY)],
        out_specs=_rows(D),
        out_shape=jax.ShapeDtypeStruct((b, t, D), F32),
        scratch_shapes=[pltpu.VMEM((TOP_K, TM, D), F32), pltpu.SemaphoreType.DMA(())],
        compiler_params=_cp(("arbitrary", "arbitrary"), VMEM_LIMIT),
        name="moe_combine",
    )(route, xl, meta, mod, y)


def _moe(xl, gain, mod, wr, br, w13, w2):
    b, t, _ = xl.shape
    n = b * t
    h, meta, cnt = _router(xl, gain, mod, wr, br)
    counts = cnt[0, :N_EXP].astype(jnp.int32)
    padded = (counts + TG - 1) // TG * TG
    ends = jnp.cumsum(padded)
    start = ends - padded
    n_tiles = TOP_K * n // TG + N_EXP
    tile_lo = jnp.arange(n_tiles, dtype=jnp.int32) * TG
    tile_expert = jnp.minimum(jnp.sum((tile_lo[:, None] >= ends[None, :]).astype(jnp.int32), axis=1), N_EXP - 1)
    n_used = (ends[-1:] // TG).astype(jnp.int32)
    ids = meta.reshape(n // TM, TM, LANE)[:, :, :2 * TOP_K].astype(jnp.int32)
    expert, rank = ids[:, :, META_E:META_E + TOP_K], ids[:, :, META_RANK:META_RANK + TOP_K]
    pos = rank + sum(jnp.where(expert == e, start[e], 0) for e in range(N_EXP))
    route = jnp.swapaxes(pos, 1, 2).reshape(n // TM, 1, TOP_K * TM)
    n_rows = n_tiles * TG
    fill = jnp.concatenate([start + counts, ends[-1:], padded - counts, n_rows - ends[-1:]])
    xs = _dispatch(h.reshape(n, D), route, fill, n_rows)
    y = _experts(xs, tile_expert, n_used, w13, w2)
    return _combine(xl, meta, mod, y, route)


def _pad_cols(w, n):
    return jnp.pad(w, ((0, 0), (0, n - w.shape[1])))


def _layout_w_in(w):
    ml = 4 * ML_W
    a, gates = w[:, :ml], w[:, ml:ml + 4 * ML_H]
    o = ml + 4 * ML_H
    cq, ckv, kr = w[:, o:o + MLA_QR], w[:, o + MLA_QR:o + MLA_QR + MLA_KVR], w[:, o + MLA_QR + MLA_KVR:o + MLA_QR + MLA_KVR + MLA_ROPE]
    o += MLA_QR + MLA_KVR + MLA_ROPE
    sw = w[:, o:o + (SW_H + 2 * SW_KV) * SW_DH]
    o += (SW_H + 2 * SW_KV) * SW_DH
    lru = w[:, o:]
    z = lambda n: jnp.zeros((w.shape[0], n), w.dtype)
    cols = [a, _pad_cols(gates, LANE), _pad_cols(cq, 2 * LANE), ckv, z(MLA_NOPE), kr, z(LANE - MLA_NOPE - MLA_ROPE), sw, lru]
    return jnp.concatenate(cols, axis=1).astype(BF16)


IN_SEGS = ((0, 4 * ML_W), (4 * ML_W, LANE), (4 * ML_W + LANE, 4 * LANE), (4 * ML_W + 5 * LANE, 4 * LANE),
           (4 * ML_W + 9 * LANE, 4 * LANE))
IN_DTYPES = (BF16, F32, F32, F32, F32)


def _row(v, n=None):
    v = v.reshape(1, -1)
    return v if n is None else _pad_cols(v, n)


def _layer_params(l, ml_gate_b, ml_out_norm, mla_q_norm, mla_w_uq, mla_kv_norm, mla_w_ukv, mla_q_gain, mla_k_gain,
                  sw_q_gain, sw_k_gain, lru_conv_w, lru_conv_b, lru_wa, lru_ba, lru_wx, lru_bx, lru_lam):
    hq = MLA_NOPE + MLA_ROPE
    wq = mla_w_uq[l].reshape(MLA_QR, MLA_H, hq)
    wq = jnp.pad(wq, ((0, 2 * LANE - MLA_QR), (0, 0), (0, LANE - hq))).reshape(2 * LANE, MLA_H * LANE)
    wkv = mla_w_ukv[l].reshape(MLA_KVR, MLA_H, MLA_NOPE + MLA_V)
    wk = jnp.pad(wkv[:, :, :MLA_NOPE], ((0, 0), (0, 0), (0, LANE - MLA_NOPE))).reshape(MLA_KVR, MLA_H * LANE)
    wv = wkv[:, :, MLA_NOPE:].reshape(MLA_KVR, MLA_H * MLA_V)

    def blockdiag(wb):
        bw = LRU_W // LRU_BLOCKS
        eye = jnp.eye(LRU_BLOCKS, dtype=wb.dtype)
        return jnp.einsum('xncd,nm->xncmd', wb, eye).reshape(2, LRU_W, LRU_W)

    return dict(
        gate_b=_row(ml_gate_b[l], LANE), out_norm=_row(ml_out_norm[l]),
        q_norm=_row(mla_q_norm[l], 2 * LANE), w_uq=wq.astype(BF16), kv_norm=_row(mla_kv_norm[l]),
        w_uk=wk.astype(BF16), w_uv=wv.astype(BF16),
        q_gain=_row(mla_q_gain[l], LANE), k_gain=_row(mla_k_gain[l], LANE),
        sw_q_gain=_row(jnp.tile(sw_q_gain[l], 2)), sw_k_gain=_row(jnp.tile(sw_k_gain[l], 2)),
        conv_w=lru_conv_w[l], conv_b=_row(lru_conv_b[l]),
        wa=blockdiag(lru_wa[l]).astype(BF16), ba=lru_ba[l].reshape(2, 1, LRU_W),
        wx=blockdiag(lru_wx[l]).astype(BF16), bx=lru_bx[l].reshape(2, 1, LRU_W),
        lam=lru_lam[l].reshape(2, 1, LRU_W),
    )


def _mix_layer(xa, mod, gain, w_in_l, w_out_l, sink, p, tabs, lc, nb, ctx_out):
    ctx_tiles = lc // TM
    za, zg, zb, zc, zd = _proj(xa, gain, mod, w_in_l, IN_SEGS, IN_DTYPES, nb, ctx_tiles)
    hfa, hba = _mlstm(za, zg, p["gate_b"], lc)
    qm, km, vm, qs, ks, vs = _prep(zb, zc, tabs[0], tabs[1], p)
    yb = _mla_attn(qm, km, vm, lc)
    yc = _swa_attn(qs, ks, vs, sink, lc)
    hfd, hbd = _lru(zd, p, lc)
    off = 0 if ctx_out else ctx_tiles
    return _out_proj(xa, mod, hfa, hba, za, p["out_norm"], yb, yc, hfd, hbd, zd, w_out_l, nb, ctx_tiles, off)


def kernel(x, c, ctx, c_ctx, ada_w, ada_b, norm_mix, norm_ffn, w_in, w_out, ml_gate_b, ml_out_norm, mla_q_norm, mla_w_uq, mla_kv_norm, mla_w_ukv, mla_q_gain, mla_k_gain, sw_q_gain, sw_k_gain, sw_sink, lru_conv_w, lru_conv_b, lru_wa, lru_ba, lru_wx, lru_bx, lru_lam, ffn_w13, ffn_w2, moe_router, moe_router_b, moe_w13, moe_w2):
    nb, t_len, _ = x.shape
    lc = ctx.shape[1]
    depth = ada_w.shape[0]
    assert nb < SUB and lc % TM == 0 and t_len % TM == 0 and t_len % GRID_W == 0
    cc = jnp.zeros((SUB, D), F32).at[:nb].set(c).at[nb].set(c_ctx)
    mods = _ada_mod(cc, ada_w, ada_b)
    tabs = (_rope_tables(t_len, lc, LANE, MLA_ROPE, MLA_NOPE), _rope_tables(t_len, lc, SW_DH, SW_DH, 0))
    xa = jnp.concatenate([ctx, x], axis=1)
    for l in range(depth):
        ctx_out = l < depth - 1
        mod = mods[l].reshape(SUB, 1, 6 * D)
        p = _layer_params(l, ml_gate_b, ml_out_norm, mla_q_norm, mla_w_uq, mla_kv_norm, mla_w_ukv, mla_q_gain,
                          mla_k_gain, sw_q_gain, sw_k_gain, lru_conv_w, lru_conv_b, lru_wa, lru_ba, lru_wx, lru_bx,
                          lru_lam)
        xa = _mix_layer(xa, mod, norm_mix[l], _layout_w_in(w_in[l]), w_out[l].astype(BF16), sw_sink[l], p, tabs,
                        lc, nb, ctx_out)
        if l % 2 == 0:
            xa = _ffn(xa, norm_ffn[l], mod, ffn_w13[l // 2].astype(BF16), ffn_w2[l // 2].astype(BF16), nb,
                      lc // TM if ctx_out else 0)
        else:
            xa = _moe(xa, norm_ffn[l], mod, _pad_cols(moe_router[l // 2], LANE), _row(moe_router_b[l // 2], LANE),
                      moe_w13[l // 2].astype(BF16), moe_w2[l // 2].astype(BF16))
    return xa
```

```python
import functools
import itertools

import numpy as np
import jax
import jax.numpy as jnp
from jax import lax
from jax.experimental import pallas as pl
from jax.experimental.pallas import tpu as pltpu

F32 = jnp.float32
BF16 = jnp.bfloat16
HI = lax.Precision.HIGHEST

D = 1024
GRID_W = 64
EPS = 1e-6
ROPE_BASE = 10000.0
ML_H, ML_DH, ML_W, ML_CHUNK = 4, 64, 256, 128
MLA_H, MLA_QR, MLA_KVR, MLA_NOPE, MLA_ROPE, MLA_V = 4, 192, 128, 64, 32, 64
SW_H, SW_KV, SW_DH, SW_WINDOW, SW_BLOCK = 4, 2, 64, 128, 128
LRU_W, LRU_BLOCKS, LRU_C = 256, 4, 8.0
N_EXP = 8

LANE = 128
SUB = 8
TM = 256
NEG = -1e30
VMEM_LIMIT = 56 * 1024 * 1024


def _cp(sem, vmem=None):
    return pltpu.CompilerParams(dimension_semantics=sem, vmem_limit_bytes=vmem)


def _mod_spec(chunk, nb, ctx_tiles, off):
    return pl.BlockSpec((None, 1, D), lambda b, j: (jnp.where(j + off < ctx_tiles, nb, b), 0, chunk))


def _rows(width, off=0, col=0, tm=TM):
    return pl.BlockSpec((None, tm, width), lambda b, j: (b, j + off, col))


def _whole(shape):
    nd = len(shape)
    return pl.BlockSpec(shape, lambda *_: (0,) * nd)


def _src_specs(src, ctx_tiles, off):
    if len(src) == 1:
        return [_rows(D, off)]
    return [pl.BlockSpec((None, TM, D), lambda b, j: (b, jnp.minimum(j + off, ctx_tiles - 1), 0)),
            pl.BlockSpec((None, TM, D), lambda b, j: (b, jnp.maximum(j + off - ctx_tiles, 0), 0))]


def _src_rows(src_refs, ctx_tiles, off):
    if len(src_refs) == 1:
        return src_refs[0][...]
    return jnp.where(pl.program_id(1) + off < ctx_tiles, src_refs[0][...], src_refs[1][...])


def _rms(x):
    return x * lax.rsqrt(jnp.mean(x * x, axis=-1, keepdims=True) + EPS)


def _lane(shape):
    return lax.broadcasted_iota(jnp.int32, shape, len(shape) - 1)


def _group_mean_matrix(n, bounds):
    i = lax.broadcasted_iota(jnp.int32, (n, n), 0)
    j = lax.broadcasted_iota(jnp.int32, (n, n), 1)
    g = jnp.zeros((n, n), F32)
    for lo, hi in bounds:
        g = jnp.where((i >= lo) & (i < hi) & (j >= lo) & (j < hi), 1.0 / (hi - lo), g)
    return g.astype(BF16)


def _group_rms(blk, bounds, gm):
    lane = _lane(blk.shape)
    sq = blk * blk
    hi_part = sq.astype(BF16)
    lo_part = (sq - hi_part.astype(F32)).astype(BF16)
    ms = (jnp.dot(hi_part, gm, preferred_element_type=F32) + jnp.dot(lo_part, gm, preferred_element_type=F32))
    inside = functools.reduce(jnp.logical_or, [(lane >= lo) & (lane < hi) for lo, hi in bounds])
    return blk * jnp.where(inside, lax.rsqrt(ms + EPS), 0.0)


def _rope(blk, tab_ref, shift):
    n = blk.shape[-1]
    return (blk * tab_ref[0] + pltpu.roll(blk, n - shift, 1) * tab_ref[1]
            + pltpu.roll(blk, shift, 1) * tab_ref[2])


def _rope_tables(t_len, lc, width, dims, offset):
    half, nf = dims // 2, dims // 4
    p = np.arange(dims)
    i = p % half
    f = i % nf
    first = i < nf
    freq = ROPE_BASE ** (-f.astype(np.float64) / nf)
    t = np.arange(t_len)
    pos = np.where(p[None, :] < half, (t // GRID_W)[:, None], (t % GRID_W)[:, None]).astype(np.float64)
    ang = pos * freq[None, :]
    cos, sin = np.cos(ang), np.sin(ang)
    tab = np.zeros((3, lc + t_len, width), np.float64)
    tab[0] = 1.0
    tab[0, lc:, offset:offset + dims] = cos
    tab[1, lc:, offset:offset + dims] = np.where(first[None, :], -sin, 0.0)
    tab[2, lc:, offset:offset + dims] = np.where(first[None, :], 0.0, sin)
    reps = LANE // width
    return jnp.asarray(np.tile(tab, (1, 1, reps)), F32)


def _ada_kernel(c_ref, w_ref, b_ref, o_ref):
    c = c_ref[...]
    s = c * jax.nn.sigmoid(c)
    o_ref[...] = jnp.dot(s, w_ref[...], precision=HI, preferred_element_type=F32) + b_ref[...]


def _ada_mod(cc, ada_w, ada_b):
    depth = ada_w.shape[0]
    n = ada_w.shape[2]
    tn = 1024
    return pl.pallas_call(
        _ada_kernel,
        grid=(depth, n // tn),
        in_specs=[
            pl.BlockSpec((SUB, D), lambda l, j: (0, 0)),
            pl.BlockSpec((None, D, tn), lambda l, j: (l, 0, j)),
            pl.BlockSpec((None, 1, tn), lambda l, j: (l, 0, j)),
        ],
        out_specs=pl.BlockSpec((None, SUB, tn), lambda l, j: (l, 0, j)),
        out_shape=jax.ShapeDtypeStruct((depth, SUB, n), F32),
        compiler_params=_cp(("parallel", "parallel")),
        name="ada_mod",
    )(cc, ada_w, ada_b.reshape(depth, 1, n))


def _proj_kernel(*refs, segs, n_src, ctx_tiles):
    (g_ref, sh_ref, sc_ref, w_ref), out_refs = refs[n_src:n_src + 4], refs[n_src + 4:]
    h = _rms(_src_rows(refs[:n_src], ctx_tiles, 0)) * g_ref[...]
    h = h * (1.0 + sc_ref[...]) + sh_ref[...]
    hb = h.astype(BF16)
    for (start, width), o_ref in zip(segs, out_refs):
        o_ref[...] = jnp.dot(hb, w_ref[:, start:start + width], preferred_element_type=F32).astype(o_ref.dtype)


def _proj(src, gain, mod, w, segs, dtypes, nb, ctx_tiles):
    b, s = src[0].shape[0], sum(a.shape[1] for a in src)
    n = w.shape[1]
    return pl.pallas_call(
        functools.partial(_proj_kernel, segs=segs, n_src=len(src), ctx_tiles=ctx_tiles),
        grid=(b, s // TM),
        in_specs=_src_specs(src, ctx_tiles, 0) + [
            _whole((1, D)),
            _mod_spec(0, nb, ctx_tiles, 0),
            _mod_spec(1, nb, ctx_tiles, 0),
            _whole((D, n)),
        ],
        out_specs=[_rows(wd) for _, wd in segs],
        out_shape=[jax.ShapeDtypeStruct((b, s, wd), dt) for (_, wd), dt in zip(segs, dtypes)],
        compiler_params=_cp(("parallel", "parallel"), VMEM_LIMIT),
        name="in_proj",
    )(*src, gain.reshape(1, D), mod, mod, w)


def _prep_kernel(zb_ref, zc_ref, tm_ref, ts_ref, qn_ref, wuq_ref, kvn_ref, wk_ref, wv_ref, qg_ref, kg_ref,
                 sqg_ref, skg_ref, qm_ref, km_ref, vm_ref, qs_ref, ks_ref, vs_ref):
    zb = zb_ref[...]
    cq = zb[:, :2 * LANE]
    cqn = cq * lax.rsqrt(jnp.sum(cq * cq, axis=-1, keepdims=True) * (1.0 / MLA_QR) + EPS) * qn_ref[...]
    qf = jnp.dot(cqn.astype(BF16), wuq_ref[...], preferred_element_type=F32)
    q_scale = (MLA_NOPE + MLA_ROPE) ** -0.5 * LOG2E
    head_bounds = [(0, MLA_NOPE), (MLA_NOPE, MLA_NOPE + MLA_ROPE)]
    gm_head = _group_mean_matrix(LANE, head_bounds)
    for h in range(MLA_H):
        blk = _group_rms(qf[:, h * LANE:(h + 1) * LANE], head_bounds, gm_head) * qg_ref[...]
        qm_ref[h] = (_rope(blk, tm_ref, MLA_ROPE // 4) * q_scale).astype(BF16)
    ckvn = _rms(zb[:, 2 * LANE:3 * LANE]) * kvn_ref[...]
    ckvb = ckvn.astype(BF16)
    kf = jnp.dot(ckvb, wk_ref[...], preferred_element_type=F32)
    vf = jnp.dot(ckvb, wv_ref[...], preferred_element_type=F32)
    kg = kg_ref[...]
    kr = _group_rms(zb[:, 3 * LANE:4 * LANE], head_bounds[1:], gm_head) * kg
    kr = _rope(kr, tm_ref, MLA_ROPE // 4)
    for h in range(MLA_H):
        kn = _group_rms(kf[:, h * LANE:(h + 1) * LANE], head_bounds[:1], gm_head) * kg
        km_ref[h] = (kn + kr).astype(BF16)
        vm_ref[h] = vf[:, h * MLA_V:(h + 1) * MLA_V].astype(BF16)
    zc = zc_ref[...]
    pair = [(0, SW_DH), (SW_DH, 2 * SW_DH)]
    gm_pair = _group_mean_matrix(LANE, pair)
    sw_scale = SW_DH ** -0.5
    for half in range(2):
        blk = _group_rms(zc[:, half * LANE:(half + 1) * LANE], pair, gm_pair) * sqg_ref[...]
        qs_ref[:, half * LANE:(half + 1) * LANE] = (_rope(blk, ts_ref, SW_DH // 4) * sw_scale).astype(BF16)
    kb = _rope(_group_rms(zc[:, 2 * LANE:3 * LANE], pair, gm_pair) * skg_ref[...], ts_ref, SW_DH // 4).astype(BF16)
    vb = zc[:, 3 * LANE:4 * LANE].astype(BF16)
    for j in range(SW_KV):
        ks_ref[j] = kb[:, j * SW_DH:(j + 1) * SW_DH]
        vs_ref[j] = vb[:, j * SW_DH:(j + 1) * SW_DH]


def _prep(zb, zc, tab_mla, tab_sw, p):
    b, s, _ = zb.shape
    tab = pl.BlockSpec((3, TM, LANE), lambda bb, j: (0, j, 0))
    heads = lambda nh, w: pl.BlockSpec((None, nh, TM, w), lambda bb, j: (bb, 0, j, 0))
    return pl.pallas_call(
        _prep_kernel,
        grid=(b, s // TM),
        in_specs=[_rows(4 * LANE), _rows(4 * LANE), tab, tab,
                  _whole((1, 2 * LANE)), _whole((2 * LANE, 4 * LANE)), _whole((1, LANE)),
                  _whole((LANE, 4 * LANE)), _whole((LANE, 2 * LANE)), _whole((1, LANE)), _whole((1, LANE)),
                  _whole((1, LANE)), _whole((1, LANE))],
        out_specs=[heads(MLA_H, LANE), heads(MLA_H, LANE), heads(MLA_H, MLA_V),
                   _rows(2 * LANE), heads(SW_KV, SW_DH), heads(SW_KV, SW_DH)],
        out_shape=[jax.ShapeDtypeStruct((b, MLA_H, s, LANE), BF16),
                   jax.ShapeDtypeStruct((b, MLA_H, s, LANE), BF16),
                   jax.ShapeDtypeStruct((b, MLA_H, s, MLA_V), BF16),
                   jax.ShapeDtypeStruct((b, s, 2 * LANE), BF16),
                   jax.ShapeDtypeStruct((b, SW_KV, s, SW_DH), BF16),
                   jax.ShapeDtypeStruct((b, SW_KV, s, SW_DH), BF16)],
        compiler_params=_cp(("parallel", "parallel"), VMEM_LIMIT),
        name="head_prep",
    )(zb, zc, tab_mla, tab_sw, p["q_norm"], p["w_uq"], p["kv_norm"], p["w_uk"], p["w_uv"], p["q_gain"],
      p["k_gain"], p["sw_q_gain"], p["sw_k_gain"])


MLA_TQ = 256
MLA_KC = 256
LOG2E = 1.4426950408889634
MLA_VA = MLA_V + 16
MLA_UNROLL = 2


def _mla_kernel(qt_ref, k_ref, vt_ref, o_ref, m_ref, acc_ref, st_ref, cm_ref, p_ref, al_ref, *, n_all, n_ctx,
                ctx_tiles):
    i = pl.program_id(1)
    n_kv = jnp.where(i < ctx_tiles, n_ctx, n_all)
    last = n_kv - 1

    def scores(c, slot):
        off = pl.multiple_of(jnp.minimum(c, last) * MLA_KC, MLA_KC)
        for h in range(MLA_H):
            st = jnp.dot(k_ref[h, pl.ds(off, MLA_KC), :], qt_ref[h], preferred_element_type=F32)
            st_ref[slot, h] = st
            cm_ref[slot, h] = jnp.max(st, axis=0, keepdims=True)

    def softmax(slot, first):
        for h in range(MLA_H):
            cm = cm_ref[slot, h]
            m_new = cm if first else jnp.maximum(m_ref[h], cm)
            p_ref[slot, h] = jnp.exp2(st_ref[slot, h] - m_new).astype(BF16)
            al_ref[slot, h] = jnp.ones_like(cm) if first else jnp.exp2(m_ref[h] - m_new)
            m_ref[h] = m_new

    def weighted_values(c, slot):
        pv = [jnp.dot(vt_ref[h, c], p_ref[slot, h], preferred_element_type=F32) for h in range(MLA_H)]
        return pv

    def accumulate(pv, slot):
        for h in range(MLA_H):
            acc_ref[h] = al_ref[slot, h] * acc_ref[h] + pv[h]

    acc_ref[...] = jnp.zeros_like(acc_ref)
    scores(0, 0)
    softmax(0, True)
    scores(1, 1)

    def body(j, carry):
        for u in range(MLA_UNROLL):
            c, slot = MLA_UNROLL * j + u, u & 1
            pv = weighted_values(c, slot)
            scores(c + 2, slot)
            softmax(1 - slot, False)
            accumulate(pv, slot)
        return carry

    lax.fori_loop(0, last // MLA_UNROLL, body, 0)
    accumulate(weighted_values(last, 0), 0)
    outs = [(acc_ref[h, :MLA_V] / acc_ref[h, MLA_V:MLA_V + 1]).T for h in range(MLA_H)]
    o_ref[...] = jnp.concatenate(outs, axis=-1).astype(o_ref.dtype)


def _mla_attn(q, k, v, lc):
    b, h, s, _ = q.shape
    nc = s // MLA_KC
    assert (nc - 1) % MLA_UNROLL == 0 and lc == MLA_KC
    qt = jnp.swapaxes(q, 2, 3)
    ones = jnp.zeros((b, h, s, MLA_VA - MLA_V), v.dtype).at[..., 0].set(1.0)
    va = jnp.concatenate([v, ones], axis=-1)
    vt = jnp.swapaxes(va.reshape(b, h, nc, MLA_KC, MLA_VA), 3, 4)
    kern = functools.partial(_mla_kernel, n_all=nc, n_ctx=lc // MLA_KC, ctx_tiles=lc // MLA_TQ)
    return pl.pallas_call(
        kern,
        grid=(b, s // MLA_TQ),
        in_specs=[pl.BlockSpec((None, h, LANE, MLA_TQ), lambda bb, i: (bb, 0, 0, i)),
                  pl.BlockSpec((None, h, s, LANE), lambda bb, i: (bb, 0, 0, 0)),
                  pl.BlockSpec((None, h, nc, MLA_VA, MLA_KC), lambda bb, i: (bb, 0, 0, 0, 0))],
        out_specs=pl.BlockSpec((None, MLA_TQ, h * MLA_V), lambda bb, i: (bb, i, 0)),
        out_shape=jax.ShapeDtypeStruct((b, s, h * MLA_V), BF16),
        scratch_shapes=[pltpu.VMEM((h, 1, MLA_TQ), F32),
                        pltpu.VMEM((h, MLA_VA, MLA_TQ), F32), pltpu.VMEM((2, h, MLA_KC, MLA_TQ), F32),
                        pltpu.VMEM((2, h, 1, MLA_TQ), F32),
                        pltpu.VMEM((2, h, MLA_KC, MLA_TQ), BF16), pltpu.VMEM((2, h, 1, MLA_TQ), F32)],
        compiler_params=_cp(("parallel", "arbitrary"), VMEM_LIMIT),
        name="mla_attn",
    )(qt, k, vt)


SW_STEP_BLOCKS = 2


def _swa_kernel(sink_ref, q_ref, k_ref, v_ref, o_ref, *, lc, t_len):
    blk = SW_BLOCK
    nw = blk + 2 * SW_WINDOW
    dn = (((1,), (1,)), ((), ()))
    row = lax.broadcasted_iota(jnp.int32, (2 * blk, nw), 0)
    col = lax.broadcasted_iota(jnp.int32, (2 * blk, nw), 1)
    rows1 = lax.broadcasted_iota(jnp.int32, (2 * blk, 1), 0)
    chains = []
    for part in range(SW_STEP_BLOCKS):
        n = pl.program_id(1) * SW_STEP_BLOCKS + part
        start = pl.multiple_of(jnp.clip((n - 1) * blk, 0, lc + t_len - nw), blk)
        qpos = n * blk - lc + (row & (blk - 1))
        kpos = start - lc + col
        valid = (n * blk >= lc) & (kpos >= 0) & (kpos < t_len) & (jnp.abs(qpos - kpos) <= SW_WINDOW)
        for j in range(SW_KV):
            q2 = q_ref[part * blk:(part + 1) * blk, j * LANE:(j + 1) * LANE]
            qs = jnp.concatenate([q2[:, :SW_DH], q2[:, SW_DH:]], axis=0)
            s_loc = lax.dot_general(qs, k_ref[j, pl.ds(start, nw), :], dn, preferred_element_type=F32)
            s_ctx = lax.dot_general(qs, k_ref[j, pl.ds(0, lc), :], dn, preferred_element_type=F32)
            chains.append(dict(j=j, start=start, valid=valid, s_loc=s_loc, s_ctx=s_ctx))
    for ch in chains:
        j = ch["j"]
        s_loc, s_ctx = jnp.where(ch["valid"], ch["s_loc"], NEG), ch["s_ctx"]
        sink = jnp.where(rows1 < blk, sink_ref[2 * j], sink_ref[2 * j + 1])
        m = jnp.maximum(jnp.maximum(jnp.max(s_loc, axis=-1, keepdims=True),
                                    jnp.max(s_ctx, axis=-1, keepdims=True)), sink)
        ch["e_loc"] = jnp.exp(s_loc - m)
        ch["e_ctx"] = jnp.exp(s_ctx - m)
        ch["den"] = (jnp.sum(ch["e_loc"], axis=-1, keepdims=True) + jnp.sum(ch["e_ctx"], axis=-1, keepdims=True)
                     + jnp.exp(sink - m))
    for ch in chains:
        j = ch["j"]
        ch["o"] = (jnp.dot(ch["e_loc"].astype(BF16), v_ref[j, pl.ds(ch["start"], nw), :], preferred_element_type=F32)
                   + jnp.dot(ch["e_ctx"].astype(BF16), v_ref[j, pl.ds(0, lc), :], preferred_element_type=F32))
    for part in range(SW_STEP_BLOCKS):
        outs = []
        for ch in chains[part * SW_KV:(part + 1) * SW_KV]:
            o = ch["o"] / ch["den"]
            outs += [o[:blk], o[blk:]]
        o_ref[part * blk:(part + 1) * blk, :] = jnp.concatenate(outs, axis=-1).astype(o_ref.dtype)


def _swa_attn(q, k, v, sink, lc):
    b, s, _ = q.shape
    t_len = s - lc
    sp = k.shape[2]
    rows = SW_STEP_BLOCKS * SW_BLOCK
    kern = functools.partial(_swa_kernel, lc=lc, t_len=t_len)
    return pl.pallas_call(
        kern,
        grid=(b, s // rows),
        in_specs=[pl.BlockSpec(memory_space=pltpu.SMEM),
                  pl.BlockSpec((None, rows, SW_KV * LANE), lambda bb, n: (bb, n, 0)),
                  pl.BlockSpec((None, SW_KV, sp, SW_DH), lambda bb, n: (bb, 0, 0, 0)),
                  pl.BlockSpec((None, SW_KV, sp, SW_DH), lambda bb, n: (bb, 0, 0, 0))],
        out_specs=pl.BlockSpec((None, rows, SW_KV * LANE), lambda bb, n: (bb, n, 0)),
        out_shape=jax.ShapeDtypeStruct((b, s, SW_KV * LANE), BF16),
        compiler_params=_cp(("parallel", "arbitrary"), VMEM_LIMIT),
        name="swa_attn",
    )(sink, q, k, v)


def _mirror(j, n_ctx, n_all):
    return jnp.where(j < n_ctx, n_ctx - 1 - j, n_all + n_ctx - 1 - j)


ML_AUG = ML_DH + SUB
ML_ROWS = 4


def _mlstm_kernel(xf_ref, qtf_ref, vtf_ref, gf_ref, xb_ref, qtb_ref, vtb_ref, gb_ref, bias_ref, hf_ref, hb_ref,
                  c_ref, m_ref):
    L = ML_CHUNK

    @pl.when(pl.program_id(1) == 0)
    def _():
        c_ref[...] = jnp.zeros_like(c_ref)
        m_ref[...] = jnp.zeros_like(m_ref)

    row = lax.broadcasted_iota(jnp.int32, (L, L), 0)
    col = lax.broadcasted_iota(jnp.int32, (L, L), 1)
    aug_row = lax.broadcasted_iota(jnp.int32, (SUB, L), 0)
    dirs = ((xf_ref, qtf_ref, vtf_ref, gf_ref, hf_ref), (xb_ref, qtb_ref, vtb_ref, gb_ref, hb_ref))
    chains = []
    for r, (direction, (x_ref, qt_ref, vt_ref, g_ref, o_ref)) in itertools.product(range(ML_ROWS), enumerate(dirs)):
        reach = (row <= col) if direction == 0 else (row >= col)
        gt = (g_ref[r] + bias_ref[...]).T
        brow = jnp.dot(jax.nn.log_sigmoid(gt), reach.astype(F32), precision=HI, preferred_element_type=F32)
        ib = gt - pltpu.roll(brow, L - ML_H, 0)
        ib_t = ib.T
        for h in range(ML_H):
            ch = dict(idx=(r * 2 + direction) * ML_H + h, reach=reach, last=L - 1 if direction == 0 else 0)
            gi, gf = 2 * ML_H * direction + h, 2 * ML_H * direction + ML_H + h
            ch.update(b_row=brow[gf:gf + 1, :], ib_row=ib[gi:gi + 1, :], ib_col=ib_t[:, gi:gi + 1])
            ch["k"] = x_ref[r, :, h * ML_DH:(h + 1) * ML_DH] * (ML_DH ** -0.5)
            ch["qt"] = qt_ref[r, h * ML_DH:(h + 1) * ML_DH, :]
            ch["vt"] = vt_ref[r, h * ML_DH:(h + 1) * ML_DH, :]
            ch["c"], ch["m"] = c_ref[ch["idx"]], m_ref[ch["idx"]]
            chains.append(ch)
    for ch in chains:
        ch["kq"] = jnp.dot(ch["k"], ch["qt"], preferred_element_type=F32)
        ch["cq"] = jnp.dot(ch["c"].astype(BF16), ch["qt"], preferred_element_type=F32)
    for ch in chains:
        b_row, m_st = ch["b_row"], ch["m"]
        d = jnp.where(ch["reach"], b_row + ch["ib_col"], NEG)
        m_inter = b_row + m_st
        m_t = jnp.maximum(m_inter, jnp.max(d, axis=0, keepdims=True))
        ch["w_inter"], ch["m_t"] = jnp.exp(m_inter - m_t), m_t
        ch["st"] = ch["kq"] * jnp.exp(d - m_t)
        b_last = b_row[:, ch["last"]:ch["last"] + 1]
        g_row = b_last + ch["ib_row"]
        m_new = jnp.maximum(b_last + m_st, jnp.max(g_row, axis=-1, keepdims=True))
        w_row = jnp.exp(g_row - m_new)
        ch["decay"], ch["m_new"] = jnp.exp(b_last + m_st - m_new), m_new
        ch["upd"] = jnp.concatenate([ch["vt"].astype(F32) * w_row, jnp.where(aug_row == 0, w_row, 0.0)], axis=0)
    for ch in chains:
        ch["num"] = jnp.dot(ch["vt"], ch["st"].astype(BF16), preferred_element_type=F32)
        ch["dc"] = jnp.dot(ch["upd"].astype(BF16), ch["k"], preferred_element_type=F32)
    outs = []
    for ch in chains:
        num = ch["num"] + ch["w_inter"] * ch["cq"][:ML_DH]
        den = jnp.sum(ch["st"], axis=0, keepdims=True) + ch["w_inter"] * ch["cq"][ML_DH:ML_DH + 1]
        outs.append(num / jnp.maximum(jnp.abs(den), jnp.exp(-ch["m_t"])))
        c_ref[ch["idx"]] = ch["decay"] * ch["c"] + ch["dc"]
        m_ref[ch["idx"]] = ch["m_new"]
    for n, (r, o_ref) in enumerate(itertools.product(range(ML_ROWS), (hf_ref, hb_ref))):
        o_ref[r] = jnp.concatenate(outs[n * ML_H:(n + 1) * ML_H], axis=0).T


def _mlstm(za, zg, bias, lc):
    b, s, _ = za.shape
    n_all, n_ctx = s // ML_CHUNK, lc // ML_CHUNK
    zat = jnp.swapaxes(za, 1, 2)
    ident = lambda j: j
    mirr = lambda j: _mirror(j, n_ctx, n_all)

    def specs(cf):
        rows = lambda w, col: pl.BlockSpec((ML_ROWS, ML_CHUNK, w), lambda bb, j: (bb, cf(j), col))
        feats = lambda blk: pl.BlockSpec((ML_ROWS, ML_W, ML_CHUNK), lambda bb, j: (bb, blk, cf(j)))
        return [rows(ML_W, 1), feats(0), feats(2), rows(LANE, 0)]

    out = lambda cf: pl.BlockSpec((ML_ROWS, ML_CHUNK, ML_W), lambda bb, j: (bb, cf(j), 0))
    assert b % ML_ROWS == 0
    return pl.pallas_call(
        _mlstm_kernel,
        grid=(b // ML_ROWS, n_all),
        in_specs=specs(ident) + specs(mirr) + [_whole((1, LANE))],
        out_specs=[out(ident), out(mirr)],
        out_shape=[jax.ShapeDtypeStruct((b, s, ML_W), F32)] * 2,
        scratch_shapes=[pltpu.VMEM((ML_ROWS * 2 * ML_H, ML_AUG, ML_DH), F32),
                        pltpu.VMEM((ML_ROWS * 2 * ML_H, 1, 1), F32)],
        compiler_params=_cp(("parallel", "arbitrary"), VMEM_LIMIT),
        name="mlstm_scan",
    )(za, zat, zat, zg, za, zat, zat, zg, bias)


LRU_T = 256
LRU_HALO = SUB
LRU_TAPS = 4


def _scan_rows(a, b, reverse):
    axis = a.ndim - 2
    n, sh = a.shape[axis], 1
    idx = lax.broadcasted_iota(jnp.int32, a.shape, axis)
    while sh < n:
        if reverse:
            ok, a_s, b_s = idx < n - sh, pltpu.roll(a, n - sh, axis), pltpu.roll(b, n - sh, axis)
        else:
            ok, a_s, b_s = idx >= sh, pltpu.roll(a, sh, axis), pltpu.roll(b, sh, axis)
        b = jnp.where(ok, a * b_s + b, b)
        a = jnp.where(ok, a * a_s, a)
        sh *= 2
    return a, b


def _lru_kernel(uf_ref, pf_ref, nf_ref, ub_ref, pb_ref, nb_ref, cw_ref, cb_ref, wa_ref, ba_ref, wx_ref, bx_ref,
                lam_ref, hf_ref, hb_ref, carry_ref, sa_ref, sb_ref, ext_ref, *, n_ctx, n_all):
    T = LRU_T
    groups = T // SUB
    j = pl.program_id(1)

    @pl.when(j == 0)
    def _():
        carry_ref[...] = jnp.zeros_like(carry_ref)

    cw = cw_ref[...]
    grow = lax.broadcasted_iota(jnp.int32, (groups, LRU_W), 0)
    for direction, (u_ref, p_ref, n_ref, o_ref) in enumerate(((uf_ref, pf_ref, nf_ref, hf_ref),
                                                              (ub_ref, pb_ref, nb_ref, hb_ref))):
        c = j if direction == 0 else _mirror(j, n_ctx, n_all)
        has_prev = ((c != 0) & (c != n_ctx)).astype(F32)
        has_next = ((c != n_ctx - 1) & (c != n_all - 1)).astype(F32)
        ext_ref[direction, 0:LRU_HALO] = p_ref[...] * has_prev
        ext_ref[direction, LRU_HALO:LRU_HALO + T] = u_ref[...]
        ext_ref[direction, LRU_HALO + T:] = n_ref[...] * has_next
        u = cb_ref[...]
        for tap in range(LRU_TAPS):
            u = u + cw[tap:tap + 1] * ext_ref[direction, pl.ds(LRU_HALO - LRU_TAPS // 2 + tap, T), :]
        ub = u.astype(BF16)
        r = jax.nn.sigmoid(jnp.dot(ub, wa_ref[direction], preferred_element_type=F32) + ba_ref[direction])
        i = jax.nn.sigmoid(jnp.dot(ub, wx_ref[direction], preferred_element_type=F32) + bx_ref[direction])
        lam = lam_ref[direction]
        log_a = (-LRU_C) * r * jnp.log1p(jnp.exp(-lam))
        a = jnp.exp(log_a)
        bb = jnp.sqrt(1.0 - a * a) * (i * u)
        rev = direction == 1
        a, bb = _scan_rows(a.reshape(groups, SUB, LRU_W), bb.reshape(groups, SUB, LRU_W), rev)
        a, bb = a.reshape(T, LRU_W), bb.reshape(T, LRU_W)
        for half in range(LRU_W // LANE):
            sa_ref[direction, half] = a[:, half * LANE:(half + 1) * LANE]
            sb_ref[direction, half] = bb[:, half * LANE:(half + 1) * LANE]
        edge = pl.ds(0 if rev else SUB - 1, groups, stride=SUB)
        ga = jnp.concatenate([sa_ref[direction, half, edge, :] for half in range(LRU_W // LANE)], axis=-1)
        gb = jnp.concatenate([sb_ref[direction, half, edge, :] for half in range(LRU_W // LANE)], axis=-1)
        ga, gb = _scan_rows(ga, gb, rev)
        if rev:
            inner, ga, gb = grow < groups - 1, pltpu.roll(ga, groups - 1, 0), pltpu.roll(gb, groups - 1, 0)
        else:
            inner, ga, gb = grow >= 1, pltpu.roll(ga, 1, 0), pltpu.roll(gb, 1, 0)
        carry = carry_ref[direction]
        h_in = jnp.where(inner, gb + ga * carry, carry)
        h_in = jnp.broadcast_to(h_in[:, None, :], (groups, SUB, LRU_W)).reshape(T, LRU_W)
        hcur = bb + a * h_in
        o_ref[...] = hcur
        last = T - 1 if direction == 0 else 0
        carry_ref[direction] = hcur[last:last + 1, :]


def _lru(zd, p, lc):
    b, s, _ = zd.shape
    n_all, n_ctx = s // LRU_T, lc // LRU_T
    per = LRU_T // LRU_HALO
    n_halo = s // LRU_HALO
    ident = lambda j: j
    mirr = lambda j: _mirror(j, n_ctx, n_all)

    def specs(cf):
        return [pl.BlockSpec((None, LRU_T, LRU_W), lambda bb, j: (bb, cf(j), 0)),
                pl.BlockSpec((None, LRU_HALO, LRU_W), lambda bb, j: (bb, jnp.maximum(cf(j) * per - 1, 0), 0)),
                pl.BlockSpec((None, LRU_HALO, LRU_W), lambda bb, j: (bb, jnp.minimum((cf(j) + 1) * per, n_halo - 1), 0))]

    vec = _whole((2, 1, LRU_W))
    mat = _whole((2, LRU_W, LRU_W))
    return pl.pallas_call(
        functools.partial(_lru_kernel, n_ctx=n_ctx, n_all=n_all),
        grid=(b, n_all),
        in_specs=specs(ident) + specs(mirr) + [_whole((4, LRU_W)), _whole((1, LRU_W)), mat, vec, mat, vec, vec],
        out_specs=[pl.BlockSpec((None, LRU_T, LRU_W), lambda bb, j: (bb, j, 0)),
                   pl.BlockSpec((None, LRU_T, LRU_W), lambda bb, j: (bb, mirr(j), 0))],
        out_shape=[jax.ShapeDtypeStruct((b, s, LRU_W), F32)] * 2,
        scratch_shapes=[pltpu.VMEM((2, 1, LRU_W), F32),
                        pltpu.VMEM((2, LRU_W // LANE, LRU_T, LANE), F32),
                        pltpu.VMEM((2, LRU_W // LANE, LRU_T, LANE), F32),
                        pltpu.VMEM((2, LRU_T + 2 * LRU_HALO, LRU_W), F32)],
        compiler_params=_cp(("parallel", "arbitrary"), VMEM_LIMIT),
        name="rglru_scan",
    )(zd, zd, zd, zd, zd, zd, p["conv_w"], p["conv_b"], p["wa"], p["ba"], p["wx"], p["bx"], p["lam"])


def _out_kernel(*refs, n_src, ctx_tiles, off):
    gate_ref, hfa_ref, hba_ref, o_ref, gn_ref, yb_ref, yc_ref, hfd_ref, hbd_ref, gd_ref, w_ref, out_ref = refs[n_src:]
    pair = [(0, ML_DH), (ML_DH, 2 * ML_DH)]
    gm_pair = _group_mean_matrix(LANE, pair)
    ha = hfa_ref[...] + hba_ref[...]
    ha = jnp.concatenate([_group_rms(ha[:, :LANE], pair, gm_pair), _group_rms(ha[:, LANE:], pair, gm_pair)],
                         axis=-1) * gn_ref[...]
    ya = (jax.nn.sigmoid(o_ref[...].astype(F32)) * ha).astype(BF16)
    yd = (jax.nn.gelu(gd_ref[...]) * (hfd_ref[...] + hbd_ref[...])).astype(BF16)
    acc = jnp.dot(ya, w_ref[0:ML_W, :], preferred_element_type=F32)
    acc = acc + jnp.dot(yb_ref[...], w_ref[ML_W:2 * ML_W, :], preferred_element_type=F32)
    acc = acc + jnp.dot(yc_ref[...], w_ref[2 * ML_W:3 * ML_W, :], preferred_element_type=F32)
    acc = acc + jnp.dot(yd, w_ref[3 * ML_W:4 * ML_W, :], preferred_element_type=F32)
    out_ref[...] = _src_rows(refs[:n_src], ctx_tiles, off) + gate_ref[...] * acc


def _out_proj(src, mod, hfa, hba, za, gn, yb, yc, hfd, hbd, zd, w, nb, ctx_tiles, off):
    b, s = src[0].shape[0], sum(a.shape[1] for a in src)
    nt = s // TM - off
    r = lambda w_, col=0: _rows(w_, off, col)
    return pl.pallas_call(
        functools.partial(_out_kernel, n_src=len(src), ctx_tiles=ctx_tiles, off=off),
        grid=(b, nt),
        in_specs=_src_specs(src, ctx_tiles, off) + [
                  _mod_spec(2, nb, ctx_tiles, off), r(ML_W), r(ML_W), r(ML_W, 3), _whole((1, ML_W)),
                  r(ML_W), r(ML_W), r(ML_W), r(ML_W), r(ML_W, 1), _whole((D, D))],
        out_specs=_rows(D),
        out_shape=jax.ShapeDtypeStruct((b, nt * TM, D), F32),
        compiler_params=_cp(("parallel", "parallel"), VMEM_LIMIT),
        name="out_proj",
    )(*src, mod, hfa, hba, za, gn, yb, yc, hfd, hbd, zd, w)


def _ffn_kernel(x_ref, g_ref, sh_ref, sc_ref, gate_ref, w13_ref, w2_ref, out_ref, *, d_ff):
    x = x_ref[...]
    h = _rms(x) * g_ref[...]
    hb = (h * (1.0 + sc_ref[...]) + sh_ref[...]).astype(BF16)
    a = jnp.dot(hb, w13_ref[:, :d_ff], preferred_element_type=F32)
    g = jnp.dot(hb, w13_ref[:, d_ff:], preferred_element_type=F32)
    u = (g * jax.nn.sigmoid(g) * a).astype(BF16)
    out_ref[...] = x + gate_ref[...] * jnp.dot(u, w2_ref[...], preferred_element_type=F32)


def _ffn(xa, gain, mod, w13, w2, nb, ctx_tiles):
    b, s, _ = xa.shape
    d_ff = w2.shape[0]
    return pl.pallas_call(
        functools.partial(_ffn_kernel, d_ff=d_ff),
        grid=(b, s // TM),
        in_specs=[_rows(D), _whole((1, D)), _mod_spec(3, nb, ctx_tiles, 0), _mod_spec(4, nb, ctx_tiles, 0),
                  _mod_spec(5, nb, ctx_tiles, 0), _whole((D, 2 * d_ff)), _whole((d_ff, D))],
        out_specs=_rows(D),
        out_shape=jax.ShapeDtypeStruct((b, s, D), F32),
        compiler_params=_cp(("parallel", "parallel"), VMEM_LIMIT),
        name="ffn",
    )(xa, gain.reshape(1, D), mod, mod, mod, w13, w2)


META_E, META_RANK, META_GATE = 0, 2, 4


def _router_kernel(x_ref, g_ref, sh_ref, sc_ref, wr_ref, wl_ref, br_ref, h_ref, meta_ref, cnt_ref):
    @pl.when((pl.program_id(0) == 0) & (pl.program_id(1) == 0))
    def _():
        cnt_ref[...] = jnp.zeros_like(cnt_ref)

    h = _rms(x_ref[...]) * g_ref[...]
    h = h * (1.0 + sc_ref[...]) + sh_ref[...]
    h_ref[...] = h
    h_hi = h.astype(BF16)
    h_lo = (h - h_hi.astype(F32)).astype(BF16)
    logits = (jnp.dot(h_hi, wr_ref[...], preferred_element_type=F32)
              + jnp.dot(h_lo, wr_ref[...], preferred_element_type=F32)
              + jnp.dot(h_hi, wl_ref[...], preferred_element_type=F32)) + br_ref[...]
    lane = _lane(logits.shape)
    logits = jnp.where(lane < N_EXP, logits, NEG)
    m1 = jnp.max(logits, axis=-1, keepdims=True)
    i1 = jnp.min(jnp.where(logits == m1, lane, LANE), axis=-1, keepdims=True)
    rest = jnp.where(lane == i1, NEG, logits)
    m2 = jnp.max(rest, axis=-1, keepdims=True)
    i2 = jnp.min(jnp.where(rest == m2, lane, LANE), axis=-1, keepdims=True)
    e2 = jnp.exp(m2 - m1)
    inv = 1.0 / (1.0 + e2)
    hit1, hit2 = lane == i1, lane == i2
    assign = (hit1 | hit2).astype(F32)
    rr = lax.broadcasted_iota(jnp.int32, (TM, TM), 0)
    cc = lax.broadcasted_iota(jnp.int32, (TM, TM), 1)
    before = jnp.dot((cc < rr).astype(BF16), assign.astype(BF16), preferred_element_type=F32) + cnt_ref[...]
    r1 = jnp.sum(jnp.where(hit1, before, 0.0), axis=-1, keepdims=True)
    r2 = jnp.sum(jnp.where(hit2, before, 0.0), axis=-1, keepdims=True)
    cnt_ref[...] = cnt_ref[...] + jnp.sum(assign, axis=0, keepdims=True)
    fields = (i1.astype(F32), i2.astype(F32), r1, r2, inv, e2 * inv)
    meta = jnp.zeros(logits.shape, F32)
    for k, val in enumerate(fields):
        meta = jnp.where(lane == k, val, meta)
    meta_ref[...] = meta


def _router(xl, gain, mod, wr, br):
    b, t, _ = xl.shape
    wr_hi = wr.astype(BF16)
    wr_lo = (wr - wr_hi.astype(F32)).astype(BF16)
    spec = lambda k: pl.BlockSpec((None, 1, D), lambda bb, j: (bb, 0, k))
    return pl.pallas_call(
        _router_kernel,
        grid=(b, t // TM),
        in_specs=[_rows(D), _whole((1, D)), spec(3), spec(4), _whole((D, LANE)), _whole((D, LANE)),
                  _whole((1, LANE))],
        out_specs=[_rows(D), _rows(LANE), _whole((1, LANE))],
        out_shape=[jax.ShapeDtypeStruct((b, t, D), F32), jax.ShapeDtypeStruct((b, t, LANE), F32),
                   jax.ShapeDtypeStruct((1, LANE), F32)],
        compiler_params=_cp(("arbitrary", "arbitrary"), VMEM_LIMIT),
        name="router",
    )(xl, gain.reshape(1, D), mod, mod, wr_hi, wr_lo, br)


TG = 256
TOP_K = 2


ROW_DMA_UNROLL = 8


def _issue_row_dmas(route_ref, make_copy):
    def issue(r, c):
        for k in range(TOP_K):
            make_copy(r, k, route_ref[0, k * TM + r]).start()
        return c

    lax.fori_loop(0, TM, issue, 0, unroll=ROW_DMA_UNROLL)


def _dispatch_kernel(route_ref, fill_ref, h_ref, xs_ref, zero_ref, sem, fill_sem):
    @pl.when(pl.program_id(0) == 0)
    def _():
        zero_ref[...] = jnp.zeros_like(zero_ref)
        zero_row = lambda pos: pltpu.make_async_copy(zero_ref.at[pl.ds(0, 1)], xs_ref.at[pl.ds(pos, 1)], fill_sem)
        for e in range(N_EXP + 1):
            first, count = fill_ref[e], fill_ref[N_EXP + 1 + e]
            lax.fori_loop(0, count, lambda i, c: (zero_row(first + i).start(), c)[1], 0)
            lax.fori_loop(0, count, lambda i, c: (zero_row(first).wait(), c)[1], 0)

    _issue_row_dmas(route_ref, lambda r, k, pos: pltpu.make_async_copy(h_ref.at[pl.ds(r, 1)],
                                                                         xs_ref.at[pl.ds(pos, 1)], sem))
    for _ in range(TOP_K):
        pltpu.make_async_copy(h_ref, xs_ref.at[pl.ds(0, TM)], sem).wait()


def _dispatch(h2, route, fill, n_rows):
    n = h2.shape[0]
    return pl.pallas_call(
        _dispatch_kernel,
        grid=(n // TM,),
        in_specs=[pl.BlockSpec((None, 1, TOP_K * TM), lambda i: (i, 0, 0), memory_space=pltpu.SMEM),
                  pl.BlockSpec(memory_space=pltpu.SMEM),
                  pl.BlockSpec((TM, D), lambda i: (i, 0))],
        out_specs=pl.BlockSpec(memory_space=pl.ANY),
        out_shape=jax.ShapeDtypeStruct((n_rows, D), F32),
        scratch_shapes=[pltpu.VMEM((SUB, D), F32), pltpu.SemaphoreType.DMA(()), pltpu.SemaphoreType.DMA(())],
        compiler_params=_cp(("arbitrary",), VMEM_LIMIT),
        name="moe_dispatch",
    )(route, fill, h2)


def _expert_kernel(te_ref, nu_ref, xs_ref, w13_ref, w2_ref, y_ref, *, d_ff):
    del te_ref
    t = pl.program_id(0)

    @pl.when(t < nu_ref[0])
    def _():
        xb = xs_ref[...].astype(BF16)
        a = jnp.dot(xb, w13_ref[:, :d_ff], preferred_element_type=F32)
        g = jnp.dot(xb, w13_ref[:, d_ff:], preferred_element_type=F32)
        u = (g * jax.nn.sigmoid(g) * a).astype(BF16)
        y_ref[...] = jnp.dot(u, w2_ref[...], preferred_element_type=F32)

    @pl.when(t >= nu_ref[0])
    def _():
        y_ref[...] = jnp.zeros_like(y_ref)


def _experts(xs, tile_expert, n_used, w13, w2):
    n_rows = xs.shape[0]
    d_ff = w2.shape[1]
    once = dict(pipeline_mode=pl.Buffered(1))
    grid_spec = pltpu.PrefetchScalarGridSpec(
        num_scalar_prefetch=2,
        grid=(n_rows // TG,),
        in_specs=[pl.BlockSpec((TG, D), lambda t, te, nu: (jnp.minimum(t, nu[0] - 1), 0)),
                  pl.BlockSpec((None, D, 2 * d_ff), lambda t, te, nu: (te[t], 0, 0), **once),
                  pl.BlockSpec((None, d_ff, D), lambda t, te, nu: (te[t], 0, 0), **once)],
        out_specs=pl.BlockSpec((TG, D), lambda t, te, nu: (t, 0)),
    )
    return pl.pallas_call(
        functools.partial(_expert_kernel, d_ff=d_ff),
        grid_spec=grid_spec,
        out_shape=jax.ShapeDtypeStruct((n_rows, D), F32),
        compiler_params=_cp(("arbitrary",), VMEM_LIMIT),
        name="moe_experts",
    )(tile_expert, n_used, xs, w13, w2)


def _combine_kernel(route_ref, next_ref, x_ref, meta_ref, gate_ref, y_ref, out_ref, buf_ref, sem, *, n_steps):
    step = pl.program_id(0) * pl.num_programs(1) + pl.program_id(1)
    slot = step & 1

    def gather(rows_ref, s):
        _issue_row_dmas(rows_ref, lambda r, k, pos: pltpu.make_async_copy(
            y_ref.at[pl.ds(pos, 1)], buf_ref.at[s, k, pl.ds(r, 1)], sem.at[s]))

    @pl.when(step == 0)
    def _():
        gather(route_ref, 0)

    @pl.when(step + 1 < n_steps)
    def _():
        gather(next_ref, 1 - slot)

    for k in range(TOP_K):
        pltpu.make_async_copy(y_ref.at[pl.ds(0, TM)], buf_ref.at[slot, k], sem.at[slot]).wait()
    meta = meta_ref[...]
    mix = (meta[:, META_GATE:META_GATE + 1] * buf_ref[slot, 0]
           + meta[:, META_GATE + 1:META_GATE + 2] * buf_ref[slot, 1])
    out_ref[...] = x_ref[...] + gate_ref[...] * mix


def _combine(xl, meta, mod, y, route):
    b, t, _ = xl.shape
    tiles = t // TM
    n_steps = b * tiles
    rows_of = lambda ahead: pl.BlockSpec((None, 1, TOP_K * TM),
                                         lambda bb, j: (jnp.minimum(bb * tiles + j + ahead, n_steps - 1), 0, 0),
                                         memory_space=pltpu.SMEM)
    return pl.pallas_call(
        functools.partial(_combine_kernel, n_steps=n_steps),
        grid=(b, tiles),
        in_specs=[rows_of(0), rows_of(1), _rows(D), _rows(LANE), pl.BlockSpec((None, 1, D), lambda bb, j: (bb, 0, 5)),
                  pl.BlockSpec(memory_space=pl.ANY)],
        out_specs=_rows(D),
        out_shape=jax.ShapeDtypeStruct((b, t, D), F32),
        scratch_shapes=[pltpu.VMEM((2, TOP_K, TM, D), F32), pltpu.SemaphoreType.DMA((2,))],
        compiler_params=_cp(("arbitrary", "arbitrary"), VMEM_LIMIT),
        name="moe_combine",
    )(route, route, xl, meta, mod, y)


def _moe(xl, gain, mod, wr, br, w13, w2):
    b, t, _ = xl.shape
    n = b * t
    h, meta, cnt = _router(xl, gain, mod, wr, br)
    counts = cnt[0, :N_EXP].astype(jnp.int32)
    padded = (counts + TG - 1) // TG * TG
    ends = jnp.cumsum(padded)
    start = ends - padded
    n_tiles = TOP_K * n // TG + N_EXP
    tile_lo = jnp.arange(n_tiles, dtype=jnp.int32) * TG
    tile_expert = jnp.minimum(jnp.sum((tile_lo[:, None] >= ends[None, :]).astype(jnp.int32), axis=1), N_EXP - 1)
    n_used = (ends[-1:] // TG).astype(jnp.int32)
    ids = meta.reshape(n // TM, TM, LANE)[:, :, :2 * TOP_K].astype(jnp.int32)
    expert, rank = ids[:, :, META_E:META_E + TOP_K], ids[:, :, META_RANK:META_RANK + TOP_K]
    pos = rank + sum(jnp.where(expert == e, start[e], 0) for e in range(N_EXP))
    route = jnp.swapaxes(pos, 1, 2).reshape(n // TM, 1, TOP_K * TM)
    n_rows = n_tiles * TG
    fill = jnp.concatenate([start + counts, ends[-1:], padded - counts, n_rows - ends[-1:]])
    xs = _dispatch(h.reshape(n, D), route, fill, n_rows)
    y = _experts(xs, tile_expert, n_used, w13, w2)
    return _combine(xl, meta, mod, y, route)


def _pad_cols(w, n):
    return jnp.pad(w, ((0, 0), (0, n - w.shape[1])))


def _layout_w_in(w):
    ml = 4 * ML_W
    a, gates = w[:, :ml], w[:, ml:ml + 4 * ML_H]
    o = ml + 4 * ML_H
    cq, ckv, kr = w[:, o:o + MLA_QR], w[:, o + MLA_QR:o + MLA_QR + MLA_KVR], w[:, o + MLA_QR + MLA_KVR:o + MLA_QR + MLA_KVR + MLA_ROPE]
    o += MLA_QR + MLA_KVR + MLA_ROPE
    sw = w[:, o:o + (SW_H + 2 * SW_KV) * SW_DH]
    o += (SW_H + 2 * SW_KV) * SW_DH
    lru = w[:, o:]
    z = lambda n: jnp.zeros((w.shape[0], n), w.dtype)
    cols = [a, _pad_cols(gates, LANE), _pad_cols(cq, 2 * LANE), ckv, z(MLA_NOPE), kr, z(LANE - MLA_NOPE - MLA_ROPE), sw, lru]
    return jnp.concatenate(cols, axis=1).astype(BF16)


IN_SEGS = ((0, 4 * ML_W), (4 * ML_W, LANE), (4 * ML_W + LANE, 4 * LANE), (4 * ML_W + 5 * LANE, 4 * LANE),
           (4 * ML_W + 9 * LANE, 4 * LANE))
IN_DTYPES = (BF16, F32, F32, F32, F32)


def _row(v, n=None):
    v = v.reshape(1, -1)
    return v if n is None else _pad_cols(v, n)


def _layer_params(l, ml_gate_b, ml_out_norm, mla_q_norm, mla_w_uq, mla_kv_norm, mla_w_ukv, mla_q_gain, mla_k_gain,
                  sw_q_gain, sw_k_gain, lru_conv_w, lru_conv_b, lru_wa, lru_ba, lru_wx, lru_bx, lru_lam):
    hq = MLA_NOPE + MLA_ROPE
    wq = mla_w_uq[l].reshape(MLA_QR, MLA_H, hq)
    wq = jnp.pad(wq, ((0, 2 * LANE - MLA_QR), (0, 0), (0, LANE - hq))).reshape(2 * LANE, MLA_H * LANE)
    wkv = mla_w_ukv[l].reshape(MLA_KVR, MLA_H, MLA_NOPE + MLA_V)
    wk = jnp.pad(wkv[:, :, :MLA_NOPE], ((0, 0), (0, 0), (0, LANE - MLA_NOPE))).reshape(MLA_KVR, MLA_H * LANE)
    wv = wkv[:, :, MLA_NOPE:].reshape(MLA_KVR, MLA_H * MLA_V)

    def blockdiag(wb):
        bw = LRU_W // LRU_BLOCKS
        eye = jnp.eye(LRU_BLOCKS, dtype=wb.dtype)
        return jnp.einsum('xncd,nm->xncmd', wb, eye).reshape(2, LRU_W, LRU_W)

    return dict(
        gate_b=_row(ml_gate_b[l], LANE), out_norm=_row(ml_out_norm[l]),
        q_norm=_row(mla_q_norm[l], 2 * LANE), w_uq=wq.astype(BF16), kv_norm=_row(mla_kv_norm[l]),
        w_uk=wk.astype(BF16), w_uv=wv.astype(BF16),
        q_gain=_row(mla_q_gain[l], LANE), k_gain=_row(mla_k_gain[l], LANE),
        sw_q_gain=_row(jnp.tile(sw_q_gain[l], 2)), sw_k_gain=_row(jnp.tile(sw_k_gain[l], 2)),
        conv_w=lru_conv_w[l], conv_b=_row(lru_conv_b[l]),
        wa=blockdiag(lru_wa[l]).astype(BF16), ba=lru_ba[l].reshape(2, 1, LRU_W),
        wx=blockdiag(lru_wx[l]).astype(BF16), bx=lru_bx[l].reshape(2, 1, LRU_W),
        lam=lru_lam[l].reshape(2, 1, LRU_W),
    )


def _mix_layer(src, mod, gain, w_in_l, w_out_l, sink, p, tabs, lc, nb, ctx_out):
    ctx_tiles = lc // TM
    za, zg, zb, zc, zd = _proj(src, gain, mod, w_in_l, IN_SEGS, IN_DTYPES, nb, ctx_tiles)
    hfa, hba = _mlstm(za, zg, p["gate_b"], lc)
    qm, km, vm, qs, ks, vs = _prep(zb, zc, tabs[0], tabs[1], p)
    yb = _mla_attn(qm, km, vm, lc)
    yc = _swa_attn(qs, ks, vs, sink, lc)
    hfd, hbd = _lru(zd, p, lc)
    off = 0 if ctx_out else ctx_tiles
    return _out_proj(src, mod, hfa, hba, za, p["out_norm"], yb, yc, hfd, hbd, zd, w_out_l, nb, ctx_tiles, off)


def kernel(x, c, ctx, c_ctx, ada_w, ada_b, norm_mix, norm_ffn, w_in, w_out, ml_gate_b, ml_out_norm, mla_q_norm, mla_w_uq, mla_kv_norm, mla_w_ukv, mla_q_gain, mla_k_gain, sw_q_gain, sw_k_gain, sw_sink, lru_conv_w, lru_conv_b, lru_wa, lru_ba, lru_wx, lru_bx, lru_lam, ffn_w13, ffn_w2, moe_router, moe_router_b, moe_w13, moe_w2):
    nb, t_len, _ = x.shape
    lc = ctx.shape[1]
    depth = ada_w.shape[0]
    assert nb < SUB and lc % TM == 0 and t_len % TM == 0 and t_len % GRID_W == 0
    cc = jnp.zeros((SUB, D), F32).at[:nb].set(c).at[nb].set(c_ctx)
    mods = _ada_mod(cc, ada_w, ada_b)
    tabs = (_rope_tables(t_len, lc, LANE, MLA_ROPE, MLA_NOPE), _rope_tables(t_len, lc, SW_DH, SW_DH, 0))
    src = (ctx, x)
    for l in range(depth):
        ctx_out = l < depth - 1
        mod = mods[l].reshape(SUB, 1, 6 * D)
        p = _layer_params(l, ml_gate_b, ml_out_norm, mla_q_norm, mla_w_uq, mla_kv_norm, mla_w_ukv, mla_q_gain,
                          mla_k_gain, sw_q_gain, sw_k_gain, lru_conv_w, lru_conv_b, lru_wa, lru_ba, lru_wx, lru_bx,
                          lru_lam)
        xa = _mix_layer(src, mod, norm_mix[l], _layout_w_in(w_in[l]), w_out[l].astype(BF16), sw_sink[l], p, tabs,
                        lc, nb, ctx_out)
        if l % 2 == 0:
            xa = _ffn(xa, norm_ffn[l], mod, ffn_w13[l // 2].astype(BF16), ffn_w2[l // 2].astype(BF16), nb,
                      lc // TM if ctx_out else 0)
        else:
            xa = _moe(xa, norm_ffn[l], mod, _pad_cols(moe_router[l // 2], LANE), _row(moe_router_b[l // 2], LANE),
                      moe_w13[l // 2].astype(BF16), moe_w2[l // 2].astype(BF16))
        src = (xa,)
    return xa
```

```python
import functools
import itertools

import numpy as np
import jax
import jax.numpy as jnp
from jax import lax
from jax.experimental import pallas as pl
from jax.experimental.pallas import tpu as pltpu

F32 = jnp.float32
BF16 = jnp.bfloat16
HI = lax.Precision.HIGHEST

D = 1024
GRID_W = 64
EPS = 1e-6
ROPE_BASE = 10000.0
ML_H, ML_DH, ML_W, ML_CHUNK = 4, 64, 256, 128
MLA_H, MLA_QR, MLA_KVR, MLA_NOPE, MLA_ROPE, MLA_V = 4, 192, 128, 64, 32, 64
SW_H, SW_KV, SW_DH, SW_WINDOW, SW_BLOCK = 4, 2, 64, 128, 128
LRU_W, LRU_BLOCKS, LRU_C = 256, 4, 8.0
N_EXP = 8

LANE = 128
SUB = 8
TM = 256
NEG = -1e30
VMEM_LIMIT = 56 * 1024 * 1024


def _cp(sem, vmem=None):
    return pltpu.CompilerParams(dimension_semantics=sem, vmem_limit_bytes=vmem)


def _mod_spec(chunk, nb, ctx_tiles, off):
    return pl.BlockSpec((None, 1, D), lambda b, j: (jnp.where(j + off < ctx_tiles, nb, b), 0, chunk))


def _rows(width, off=0, col=0, tm=TM):
    return pl.BlockSpec((None, tm, width), lambda b, j: (b, j + off, col))


def _whole(shape):
    nd = len(shape)
    return pl.BlockSpec(shape, lambda *_: (0,) * nd)


def _src_specs(src, ctx_tiles, off):
    if len(src) == 1:
        return [_rows(D, off)]
    return [pl.BlockSpec((None, TM, D), lambda b, j: (b, jnp.minimum(j + off, ctx_tiles - 1), 0)),
            pl.BlockSpec((None, TM, D), lambda b, j: (b, jnp.maximum(j + off - ctx_tiles, 0), 0))]


def _src_rows(src_refs, ctx_tiles, off):
    if len(src_refs) == 1:
        return src_refs[0][...]
    return jnp.where(pl.program_id(1) + off < ctx_tiles, src_refs[0][...], src_refs[1][...])


def _rms(x):
    return x * lax.rsqrt(jnp.mean(x * x, axis=-1, keepdims=True) + EPS)


def _lane(shape):
    return lax.broadcasted_iota(jnp.int32, shape, len(shape) - 1)


def _group_mean_matrix(n, bounds):
    i = lax.broadcasted_iota(jnp.int32, (n, n), 0)
    j = lax.broadcasted_iota(jnp.int32, (n, n), 1)
    g = jnp.zeros((n, n), F32)
    for lo, hi in bounds:
        g = jnp.where((i >= lo) & (i < hi) & (j >= lo) & (j < hi), 1.0 / (hi - lo), g)
    return g.astype(BF16)


def _group_rms(blk, bounds, gm):
    lane = _lane(blk.shape)
    sq = blk * blk
    hi_part = sq.astype(BF16)
    lo_part = (sq - hi_part.astype(F32)).astype(BF16)
    ms = (jnp.dot(hi_part, gm, preferred_element_type=F32) + jnp.dot(lo_part, gm, preferred_element_type=F32))
    inside = functools.reduce(jnp.logical_or, [(lane >= lo) & (lane < hi) for lo, hi in bounds])
    return blk * jnp.where(inside, lax.rsqrt(ms + EPS), 0.0)


def _rope(blk, tab_ref, shift):
    n = blk.shape[-1]
    return (blk * tab_ref[0] + pltpu.roll(blk, n - shift, 1) * tab_ref[1]
            + pltpu.roll(blk, shift, 1) * tab_ref[2])


def _rope_tables(t_len, lc, width, dims, offset):
    half, nf = dims // 2, dims // 4
    p = np.arange(dims)
    i = p % half
    f = i % nf
    first = i < nf
    freq = ROPE_BASE ** (-f.astype(np.float64) / nf)
    t = np.arange(t_len)
    pos = np.where(p[None, :] < half, (t // GRID_W)[:, None], (t % GRID_W)[:, None]).astype(np.float64)
    ang = pos * freq[None, :]
    cos, sin = np.cos(ang), np.sin(ang)
    tab = np.zeros((3, lc + t_len, width), np.float64)
    tab[0] = 1.0
    tab[0, lc:, offset:offset + dims] = cos
    tab[1, lc:, offset:offset + dims] = np.where(first[None, :], -sin, 0.0)
    tab[2, lc:, offset:offset + dims] = np.where(first[None, :], 0.0, sin)
    reps = LANE // width
    return jnp.asarray(np.tile(tab, (1, 1, reps)), F32)


def _ada_kernel(c_ref, w_ref, b_ref, o_ref):
    c = c_ref[...]
    s = c * jax.nn.sigmoid(c)
    o_ref[...] = jnp.dot(s, w_ref[...], precision=HI, preferred_element_type=F32) + b_ref[...]


def _ada_mod(cc, ada_w, ada_b):
    depth = ada_w.shape[0]
    n = ada_w.shape[2]
    tn = 1024
    return pl.pallas_call(
        _ada_kernel,
        grid=(depth, n // tn),
        in_specs=[
            pl.BlockSpec((SUB, D), lambda l, j: (0, 0)),
            pl.BlockSpec((None, D, tn), lambda l, j: (l, 0, j)),
            pl.BlockSpec((None, 1, tn), lambda l, j: (l, 0, j)),
        ],
        out_specs=pl.BlockSpec((None, SUB, tn), lambda l, j: (l, 0, j)),
        out_shape=jax.ShapeDtypeStruct((depth, SUB, n), F32),
        compiler_params=_cp(("parallel", "parallel")),
        name="ada_mod",
    )(cc, ada_w, ada_b.reshape(depth, 1, n))


def _proj_kernel(*refs, segs, n_src, ctx_tiles):
    (g_ref, sh_ref, sc_ref, w_ref), out_refs = refs[n_src:n_src + 4], refs[n_src + 4:]
    h = _rms(_src_rows(refs[:n_src], ctx_tiles, 0)) * g_ref[...]
    h = h * (1.0 + sc_ref[...]) + sh_ref[...]
    hb = h.astype(BF16)
    for (start, width), o_ref in zip(segs, out_refs):
        o_ref[...] = jnp.dot(hb, w_ref[:, start:start + width], preferred_element_type=F32).astype(o_ref.dtype)


def _proj(src, gain, mod, w, segs, dtypes, nb, ctx_tiles):
    b, s = src[0].shape[0], sum(a.shape[1] for a in src)
    n = w.shape[1]
    return pl.pallas_call(
        functools.partial(_proj_kernel, segs=segs, n_src=len(src), ctx_tiles=ctx_tiles),
        grid=(b, s // TM),
        in_specs=_src_specs(src, ctx_tiles, 0) + [
            _whole((1, D)),
            _mod_spec(0, nb, ctx_tiles, 0),
            _mod_spec(1, nb, ctx_tiles, 0),
            _whole((D, n)),
        ],
        out_specs=[_rows(wd) for _, wd in segs],
        out_shape=[jax.ShapeDtypeStruct((b, s, wd), dt) for (_, wd), dt in zip(segs, dtypes)],
        compiler_params=_cp(("parallel", "parallel"), VMEM_LIMIT),
        name="in_proj",
    )(*src, gain.reshape(1, D), mod, mod, w)


def _prep_kernel(zb_ref, zc_ref, tm_ref, ts_ref, qn_ref, wuq_ref, kvn_ref, wk_ref, wv_ref, qg_ref, kg_ref,
                 sqg_ref, skg_ref, qm_ref, km_ref, vm_ref, qs_ref, ks_ref, vs_ref):
    zb = zb_ref[...]
    cq = zb[:, :2 * LANE]
    cqn = cq * lax.rsqrt(jnp.sum(cq * cq, axis=-1, keepdims=True) * (1.0 / MLA_QR) + EPS) * qn_ref[...]
    qf = jnp.dot(cqn.astype(BF16), wuq_ref[...], preferred_element_type=F32)
    q_scale = (MLA_NOPE + MLA_ROPE) ** -0.5 * LOG2E
    head_bounds = [(0, MLA_NOPE), (MLA_NOPE, MLA_NOPE + MLA_ROPE)]
    gm_head = _group_mean_matrix(LANE, head_bounds)
    for h in range(MLA_H):
        blk = _group_rms(qf[:, h * LANE:(h + 1) * LANE], head_bounds, gm_head) * qg_ref[...]
        qm_ref[h] = (_rope(blk, tm_ref, MLA_ROPE // 4) * q_scale).astype(BF16)
    ckvn = _rms(zb[:, 2 * LANE:3 * LANE]) * kvn_ref[...]
    ckvb = ckvn.astype(BF16)
    kf = jnp.dot(ckvb, wk_ref[...], preferred_element_type=F32)
    vf = jnp.dot(ckvb, wv_ref[...], preferred_element_type=F32)
    kg = kg_ref[...]
    kr = _group_rms(zb[:, 3 * LANE:4 * LANE], head_bounds[1:], gm_head) * kg
    kr = _rope(kr, tm_ref, MLA_ROPE // 4)
    for h in range(MLA_H):
        kn = _group_rms(kf[:, h * LANE:(h + 1) * LANE], head_bounds[:1], gm_head) * kg
        km_ref[h] = (kn + kr).astype(BF16)
        vm_ref[h] = vf[:, h * MLA_V:(h + 1) * MLA_V].astype(BF16)
    zc = zc_ref[...]
    pair = [(0, SW_DH), (SW_DH, 2 * SW_DH)]
    gm_pair = _group_mean_matrix(LANE, pair)
    sw_scale = SW_DH ** -0.5
    for half in range(2):
        blk = _group_rms(zc[:, half * LANE:(half + 1) * LANE], pair, gm_pair) * sqg_ref[...]
        qs_ref[:, half * LANE:(half + 1) * LANE] = (_rope(blk, ts_ref, SW_DH // 4) * sw_scale).astype(BF16)
    kb = _rope(_group_rms(zc[:, 2 * LANE:3 * LANE], pair, gm_pair) * skg_ref[...], ts_ref, SW_DH // 4).astype(BF16)
    vb = zc[:, 3 * LANE:4 * LANE].astype(BF16)
    for j in range(SW_KV):
        ks_ref[j] = kb[:, j * SW_DH:(j + 1) * SW_DH]
        vs_ref[j] = vb[:, j * SW_DH:(j + 1) * SW_DH]


def _prep(zb, zc, tab_mla, tab_sw, p):
    b, s, _ = zb.shape
    tab = pl.BlockSpec((3, TM, LANE), lambda bb, j: (0, j, 0))
    heads = lambda nh, w: pl.BlockSpec((None, nh, TM, w), lambda bb, j: (bb, 0, j, 0))
    return pl.pallas_call(
        _prep_kernel,
        grid=(b, s // TM),
        in_specs=[_rows(4 * LANE), _rows(4 * LANE), tab, tab,
                  _whole((1, 2 * LANE)), _whole((2 * LANE, 4 * LANE)), _whole((1, LANE)),
                  _whole((LANE, 4 * LANE)), _whole((LANE, 2 * LANE)), _whole((1, LANE)), _whole((1, LANE)),
                  _whole((1, LANE)), _whole((1, LANE))],
        out_specs=[heads(MLA_H, LANE), heads(MLA_H, LANE), heads(MLA_H, MLA_V),
                   _rows(2 * LANE), heads(SW_KV, SW_DH), heads(SW_KV, SW_DH)],
        out_shape=[jax.ShapeDtypeStruct((b, MLA_H, s, LANE), BF16),
                   jax.ShapeDtypeStruct((b, MLA_H, s, LANE), BF16),
                   jax.ShapeDtypeStruct((b, MLA_H, s, MLA_V), BF16),
                   jax.ShapeDtypeStruct((b, s, 2 * LANE), BF16),
                   jax.ShapeDtypeStruct((b, SW_KV, s, SW_DH), BF16),
                   jax.ShapeDtypeStruct((b, SW_KV, s, SW_DH), BF16)],
        compiler_params=_cp(("parallel", "parallel"), VMEM_LIMIT),
        name="head_prep",
    )(zb, zc, tab_mla, tab_sw, p["q_norm"], p["w_uq"], p["kv_norm"], p["w_uk"], p["w_uv"], p["q_gain"],
      p["k_gain"], p["sw_q_gain"], p["sw_k_gain"])


MLA_TQ = 256
MLA_KC = 256
LOG2E = 1.4426950408889634
MLA_VA = MLA_V + 16
MLA_UNROLL = 2


def _mla_kernel(qt_ref, k_ref, vt_ref, o_ref, m_ref, acc_ref, st_ref, cm_ref, p_ref, al_ref, *, n_all, n_ctx,
                ctx_tiles):
    i = pl.program_id(1)
    n_kv = jnp.where(i < ctx_tiles, n_ctx, n_all)
    last = n_kv - 1

    def scores(c, slot):
        off = pl.multiple_of(jnp.minimum(c, last) * MLA_KC, MLA_KC)
        for h in range(MLA_H):
            st = jnp.dot(k_ref[h, pl.ds(off, MLA_KC), :], qt_ref[h], preferred_element_type=F32)
            st_ref[slot, h] = st
            cm_ref[slot, h] = jnp.max(st, axis=0, keepdims=True)

    def softmax(slot, first):
        for h in range(MLA_H):
            cm = cm_ref[slot, h]
            m_new = cm if first else jnp.maximum(m_ref[h], cm)
            p_ref[slot, h] = jnp.exp2(st_ref[slot, h] - m_new).astype(BF16)
            al_ref[slot, h] = jnp.ones_like(cm) if first else jnp.exp2(m_ref[h] - m_new)
            m_ref[h] = m_new

    def weighted_values(c, slot):
        pv = [jnp.dot(vt_ref[h, c], p_ref[slot, h], preferred_element_type=F32) for h in range(MLA_H)]
        return pv

    def accumulate(pv, slot):
        for h in range(MLA_H):
            acc_ref[h] = al_ref[slot, h] * acc_ref[h] + pv[h]

    acc_ref[...] = jnp.zeros_like(acc_ref)
    scores(0, 0)
    softmax(0, True)
    scores(1, 1)

    def body(j, carry):
        for u in range(MLA_UNROLL):
            c, slot = MLA_UNROLL * j + u, u & 1
            pv = weighted_values(c, slot)
            scores(c + 2, slot)
            softmax(1 - slot, False)
            accumulate(pv, slot)
        return carry

    lax.fori_loop(0, last // MLA_UNROLL, body, 0)
    accumulate(weighted_values(last, 0), 0)
    outs = [(acc_ref[h, :MLA_V] / acc_ref[h, MLA_V:MLA_V + 1]).T for h in range(MLA_H)]
    o_ref[...] = jnp.concatenate(outs, axis=-1).astype(o_ref.dtype)


def _mla_attn(q, k, v, lc):
    b, h, s, _ = q.shape
    nc = s // MLA_KC
    assert (nc - 1) % MLA_UNROLL == 0 and lc == MLA_KC
    qt = jnp.swapaxes(q, 2, 3)
    ones = jnp.zeros((b, h, s, MLA_VA - MLA_V), v.dtype).at[..., 0].set(1.0)
    va = jnp.concatenate([v, ones], axis=-1)
    vt = jnp.swapaxes(va.reshape(b, h, nc, MLA_KC, MLA_VA), 3, 4)
    kern = functools.partial(_mla_kernel, n_all=nc, n_ctx=lc // MLA_KC, ctx_tiles=lc // MLA_TQ)
    return pl.pallas_call(
        kern,
        grid=(b, s // MLA_TQ),
        in_specs=[pl.BlockSpec((None, h, LANE, MLA_TQ), lambda bb, i: (bb, 0, 0, i)),
                  pl.BlockSpec((None, h, s, LANE), lambda bb, i: (bb, 0, 0, 0)),
                  pl.BlockSpec((None, h, nc, MLA_VA, MLA_KC), lambda bb, i: (bb, 0, 0, 0, 0))],
        out_specs=pl.BlockSpec((None, MLA_TQ, h * MLA_V), lambda bb, i: (bb, i, 0)),
        out_shape=jax.ShapeDtypeStruct((b, s, h * MLA_V), BF16),
        scratch_shapes=[pltpu.VMEM((h, 1, MLA_TQ), F32),
                        pltpu.VMEM((h, MLA_VA, MLA_TQ), F32), pltpu.VMEM((2, h, MLA_KC, MLA_TQ), F32),
                        pltpu.VMEM((2, h, 1, MLA_TQ), F32),
                        pltpu.VMEM((2, h, MLA_KC, MLA_TQ), BF16), pltpu.VMEM((2, h, 1, MLA_TQ), F32)],
        compiler_params=_cp(("parallel", "arbitrary"), VMEM_LIMIT),
        name="mla_attn",
    )(qt, k, vt)


SW_STEP_BLOCKS = 2


def _swa_kernel(sink_ref, q_ref, k_ref, v_ref, o_ref, *, lc, t_len):
    blk = SW_BLOCK
    nw = blk + 2 * SW_WINDOW
    dn = (((1,), (1,)), ((), ()))
    row = lax.broadcasted_iota(jnp.int32, (2 * blk, nw), 0)
    col = lax.broadcasted_iota(jnp.int32, (2 * blk, nw), 1)
    rows1 = lax.broadcasted_iota(jnp.int32, (2 * blk, 1), 0)
    chains = []
    for part in range(SW_STEP_BLOCKS):
        n = pl.program_id(1) * SW_STEP_BLOCKS + part
        start = pl.multiple_of(jnp.clip((n - 1) * blk, 0, lc + t_len - nw), blk)
        qpos = n * blk - lc + (row & (blk - 1))
        kpos = start - lc + col
        valid = (n * blk >= lc) & (kpos >= 0) & (kpos < t_len) & (jnp.abs(qpos - kpos) <= SW_WINDOW)
        for j in range(SW_KV):
            q2 = q_ref[part * blk:(part + 1) * blk, j * LANE:(j + 1) * LANE]
            qs = jnp.concatenate([q2[:, :SW_DH], q2[:, SW_DH:]], axis=0)
            s_loc = lax.dot_general(qs, k_ref[j, pl.ds(start, nw), :], dn, preferred_element_type=F32)
            s_ctx = lax.dot_general(qs, k_ref[j, pl.ds(0, lc), :], dn, preferred_element_type=F32)
            chains.append(dict(j=j, start=start, valid=valid, s_loc=s_loc, s_ctx=s_ctx))
    for ch in chains:
        j = ch["j"]
        s_loc, s_ctx = jnp.where(ch["valid"], ch["s_loc"], NEG), ch["s_ctx"]
        sink = jnp.where(rows1 < blk, sink_ref[2 * j], sink_ref[2 * j + 1])
        m = jnp.maximum(jnp.maximum(jnp.max(s_loc, axis=-1, keepdims=True),
                                    jnp.max(s_ctx, axis=-1, keepdims=True)), sink)
        ch["e_loc"] = jnp.exp(s_loc - m)
        ch["e_ctx"] = jnp.exp(s_ctx - m)
        ch["den"] = (jnp.sum(ch["e_loc"], axis=-1, keepdims=True) + jnp.sum(ch["e_ctx"], axis=-1, keepdims=True)
                     + jnp.exp(sink - m))
    for ch in chains:
        j = ch["j"]
        ch["o"] = (jnp.dot(ch["e_loc"].astype(BF16), v_ref[j, pl.ds(ch["start"], nw), :], preferred_element_type=F32)
                   + jnp.dot(ch["e_ctx"].astype(BF16), v_ref[j, pl.ds(0, lc), :], preferred_element_type=F32))
    for part in range(SW_STEP_BLOCKS):
        outs = []
        for ch in chains[part * SW_KV:(part + 1) * SW_KV]:
            o = ch["o"] / ch["den"]
            outs += [o[:blk], o[blk:]]
        o_ref[part * blk:(part + 1) * blk, :] = jnp.concatenate(outs, axis=-1).astype(o_ref.dtype)


def _swa_attn(q, k, v, sink, lc):
    b, s, _ = q.shape
    t_len = s - lc
    sp = k.shape[2]
    rows = SW_STEP_BLOCKS * SW_BLOCK
    kern = functools.partial(_swa_kernel, lc=lc, t_len=t_len)
    return pl.pallas_call(
        kern,
        grid=(b, s // rows),
        in_specs=[pl.BlockSpec(memory_space=pltpu.SMEM),
                  pl.BlockSpec((None, rows, SW_KV * LANE), lambda bb, n: (bb, n, 0)),
                  pl.BlockSpec((None, SW_KV, sp, SW_DH), lambda bb, n: (bb, 0, 0, 0)),
                  pl.BlockSpec((None, SW_KV, sp, SW_DH), lambda bb, n: (bb, 0, 0, 0))],
        out_specs=pl.BlockSpec((None, rows, SW_KV * LANE), lambda bb, n: (bb, n, 0)),
        out_shape=jax.ShapeDtypeStruct((b, s, SW_KV * LANE), BF16),
        compiler_params=_cp(("parallel", "arbitrary"), VMEM_LIMIT),
        name="swa_attn",
    )(sink, q, k, v)


def _mirror(j, n_ctx, n_all):
    return jnp.where(j < n_ctx, n_ctx - 1 - j, n_all + n_ctx - 1 - j)


ML_AUG = ML_DH + SUB
ML_ROWS = 4


def _mlstm_kernel(xf_ref, qtf_ref, vtf_ref, gf_ref, xb_ref, qtb_ref, vtb_ref, gb_ref, bias_ref, hf_ref, hb_ref,
                  c_ref, m_ref):
    L = ML_CHUNK

    @pl.when(pl.program_id(1) == 0)
    def _():
        c_ref[...] = jnp.zeros_like(c_ref)
        m_ref[...] = jnp.zeros_like(m_ref)

    row = lax.broadcasted_iota(jnp.int32, (L, L), 0)
    col = lax.broadcasted_iota(jnp.int32, (L, L), 1)
    aug_row = lax.broadcasted_iota(jnp.int32, (SUB, L), 0)
    dirs = ((xf_ref, qtf_ref, vtf_ref, gf_ref, hf_ref), (xb_ref, qtb_ref, vtb_ref, gb_ref, hb_ref))
    chains = []
    for r, (direction, (x_ref, qt_ref, vt_ref, g_ref, o_ref)) in itertools.product(range(ML_ROWS), enumerate(dirs)):
        reach = (row <= col) if direction == 0 else (row >= col)
        gt = (g_ref[r] + bias_ref[...]).T
        brow = jnp.dot(jax.nn.log_sigmoid(gt), reach.astype(F32), precision=HI, preferred_element_type=F32)
        ib = gt - pltpu.roll(brow, L - ML_H, 0)
        ib_t = ib.T
        for h in range(ML_H):
            ch = dict(idx=(r * 2 + direction) * ML_H + h, reach=reach, last=L - 1 if direction == 0 else 0)
            gi, gf = 2 * ML_H * direction + h, 2 * ML_H * direction + ML_H + h
            ch.update(b_row=brow[gf:gf + 1, :], ib_row=ib[gi:gi + 1, :], ib_col=ib_t[:, gi:gi + 1])
            ch["k"] = x_ref[r, :, h * ML_DH:(h + 1) * ML_DH] * (ML_DH ** -0.5)
            ch["qt"] = qt_ref[r, h * ML_DH:(h + 1) * ML_DH, :]
            ch["vt"] = vt_ref[r, h * ML_DH:(h + 1) * ML_DH, :]
            ch["c"], ch["m"] = c_ref[ch["idx"]], m_ref[ch["idx"]]
            chains.append(ch)
    for ch in chains:
        ch["kq"] = jnp.dot(ch["k"], ch["qt"], preferred_element_type=F32)
        ch["cq"] = jnp.dot(ch["c"].astype(BF16), ch["qt"], preferred_element_type=F32)
    for ch in chains:
        b_row, m_st = ch["b_row"], ch["m"]
        d = jnp.where(ch["reach"], b_row + ch["ib_col"], NEG)
        m_inter = b_row + m_st
        m_t = jnp.maximum(m_inter, jnp.max(d, axis=0, keepdims=True))
        ch["w_inter"], ch["m_t"] = jnp.exp(m_inter - m_t), m_t
        ch["st"] = ch["kq"] * jnp.exp(d - m_t)
        b_last = b_row[:, ch["last"]:ch["last"] + 1]
        g_row = b_last + ch["ib_row"]
        m_new = jnp.maximum(b_last + m_st, jnp.max(g_row, axis=-1, keepdims=True))
        w_row = jnp.exp(g_row - m_new)
        ch["decay"], ch["m_new"] = jnp.exp(b_last + m_st - m_new), m_new
        ch["upd"] = jnp.concatenate([ch["vt"].astype(F32) * w_row, jnp.where(aug_row == 0, w_row, 0.0)], axis=0)
    for ch in chains:
        ch["num"] = jnp.dot(ch["vt"], ch["st"].astype(BF16), preferred_element_type=F32)
        ch["dc"] = jnp.dot(ch["upd"].astype(BF16), ch["k"], preferred_element_type=F32)
    outs = []
    for ch in chains:
        num = ch["num"] + ch["w_inter"] * ch["cq"][:ML_DH]
        den = jnp.sum(ch["st"], axis=0, keepdims=True) + ch["w_inter"] * ch["cq"][ML_DH:ML_DH + 1]
        outs.append(num / jnp.maximum(jnp.abs(den), jnp.exp(-ch["m_t"])))
        c_ref[ch["idx"]] = ch["decay"] * ch["c"] + ch["dc"]
        m_ref[ch["idx"]] = ch["m_new"]
    for n, (r, o_ref) in enumerate(itertools.product(range(ML_ROWS), (hf_ref, hb_ref))):
        o_ref[r] = jnp.concatenate(outs[n * ML_H:(n + 1) * ML_H], axis=0).T


def _mlstm(za, zg, bias, lc):
    b, s, _ = za.shape
    n_all, n_ctx = s // ML_CHUNK, lc // ML_CHUNK
    zat = jnp.swapaxes(za, 1, 2)
    ident = lambda j: j
    mirr = lambda j: _mirror(j, n_ctx, n_all)

    def specs(cf):
        rows = lambda w, col: pl.BlockSpec((ML_ROWS, ML_CHUNK, w), lambda bb, j: (bb, cf(j), col))
        feats = lambda blk: pl.BlockSpec((ML_ROWS, ML_W, ML_CHUNK), lambda bb, j: (bb, blk, cf(j)))
        return [rows(ML_W, 1), feats(0), feats(2), rows(LANE, 0)]

    out = lambda cf: pl.BlockSpec((ML_ROWS, ML_CHUNK, ML_W), lambda bb, j: (bb, cf(j), 0))
    assert b % ML_ROWS == 0
    return pl.pallas_call(
        _mlstm_kernel,
        grid=(b // ML_ROWS, n_all),
        in_specs=specs(ident) + specs(mirr) + [_whole((1, LANE))],
        out_specs=[out(ident), out(mirr)],
        out_shape=[jax.ShapeDtypeStruct((b, s, ML_W), F32)] * 2,
        scratch_shapes=[pltpu.VMEM((ML_ROWS * 2 * ML_H, ML_AUG, ML_DH), F32),
                        pltpu.VMEM((ML_ROWS * 2 * ML_H, 1, 1), F32)],
        compiler_params=_cp(("parallel", "arbitrary"), VMEM_LIMIT),
        name="mlstm_scan",
    )(za, zat, zat, zg, za, zat, zat, zg, bias)


LRU_T = 256
LRU_HALO = SUB
LRU_TAPS = 4


def _scan_rows(a, b, reverse):
    axis = a.ndim - 2
    n, sh = a.shape[axis], 1
    idx = lax.broadcasted_iota(jnp.int32, a.shape, axis)
    while sh < n:
        if reverse:
            ok, a_s, b_s = idx < n - sh, pltpu.roll(a, n - sh, axis), pltpu.roll(b, n - sh, axis)
        else:
            ok, a_s, b_s = idx >= sh, pltpu.roll(a, sh, axis), pltpu.roll(b, sh, axis)
        b = jnp.where(ok, a * b_s + b, b)
        a = jnp.where(ok, a * a_s, a)
        sh *= 2
    return a, b


def _lru_kernel(uf_ref, pf_ref, nf_ref, ub_ref, pb_ref, nb_ref, cw_ref, cb_ref, wa_ref, ba_ref, wx_ref, bx_ref,
                lam_ref, hf_ref, hb_ref, carry_ref, sa_ref, sb_ref, ext_ref, *, n_ctx, n_all):
    T = LRU_T
    groups = T // SUB
    j = pl.program_id(1)

    @pl.when(j == 0)
    def _():
        carry_ref[...] = jnp.zeros_like(carry_ref)

    cw = cw_ref[...]
    grow = lax.broadcasted_iota(jnp.int32, (groups, LRU_W), 0)
    for direction, (u_ref, p_ref, n_ref, o_ref) in enumerate(((uf_ref, pf_ref, nf_ref, hf_ref),
                                                              (ub_ref, pb_ref, nb_ref, hb_ref))):
        c = j if direction == 0 else _mirror(j, n_ctx, n_all)
        has_prev = ((c != 0) & (c != n_ctx)).astype(F32)
        has_next = ((c != n_ctx - 1) & (c != n_all - 1)).astype(F32)
        ext_ref[direction, 0:LRU_HALO] = p_ref[...] * has_prev
        ext_ref[direction, LRU_HALO:LRU_HALO + T] = u_ref[...]
        ext_ref[direction, LRU_HALO + T:] = n_ref[...] * has_next
        u = cb_ref[...]
        for tap in range(LRU_TAPS):
            u = u + cw[tap:tap + 1] * ext_ref[direction, pl.ds(LRU_HALO - LRU_TAPS // 2 + tap, T), :]
        ub = u.astype(BF16)
        r = jax.nn.sigmoid(jnp.dot(ub, wa_ref[direction], preferred_element_type=F32) + ba_ref[direction])
        i = jax.nn.sigmoid(jnp.dot(ub, wx_ref[direction], preferred_element_type=F32) + bx_ref[direction])
        lam = lam_ref[direction]
        log_a = (-LRU_C) * r * jnp.log1p(jnp.exp(-lam))
        a = jnp.exp(log_a)
        bb = jnp.sqrt(1.0 - a * a) * (i * u)
        rev = direction == 1
        a, bb = _scan_rows(a.reshape(groups, SUB, LRU_W), bb.reshape(groups, SUB, LRU_W), rev)
        a, bb = a.reshape(T, LRU_W), bb.reshape(T, LRU_W)
        for half in range(LRU_W // LANE):
            sa_ref[direction, half] = a[:, half * LANE:(half + 1) * LANE]
            sb_ref[direction, half] = bb[:, half * LANE:(half + 1) * LANE]
        edge = pl.ds(0 if rev else SUB - 1, groups, stride=SUB)
        ga = jnp.concatenate([sa_ref[direction, half, edge, :] for half in range(LRU_W // LANE)], axis=-1)
        gb = jnp.concatenate([sb_ref[direction, half, edge, :] for half in range(LRU_W // LANE)], axis=-1)
        ga, gb = _scan_rows(ga, gb, rev)
        if rev:
            inner, ga, gb = grow < groups - 1, pltpu.roll(ga, groups - 1, 0), pltpu.roll(gb, groups - 1, 0)
        else:
            inner, ga, gb = grow >= 1, pltpu.roll(ga, 1, 0), pltpu.roll(gb, 1, 0)
        carry = carry_ref[direction]
        h_in = jnp.where(inner, gb + ga * carry, carry)
        h_in = jnp.broadcast_to(h_in[:, None, :], (groups, SUB, LRU_W)).reshape(T, LRU_W)
        hcur = bb + a * h_in
        o_ref[...] = hcur
        last = T - 1 if direction == 0 else 0
        carry_ref[direction] = hcur[last:last + 1, :]


def _lru(zd, p, lc):
    b, s, _ = zd.shape
    n_all, n_ctx = s // LRU_T, lc // LRU_T
    per = LRU_T // LRU_HALO
    n_halo = s // LRU_HALO
    ident = lambda j: j
    mirr = lambda j: _mirror(j, n_ctx, n_all)

    def specs(cf):
        return [pl.BlockSpec((None, LRU_T, LRU_W), lambda bb, j: (bb, cf(j), 0)),
                pl.BlockSpec((None, LRU_HALO, LRU_W), lambda bb, j: (bb, jnp.maximum(cf(j) * per - 1, 0), 0)),
                pl.BlockSpec((None, LRU_HALO, LRU_W), lambda bb, j: (bb, jnp.minimum((cf(j) + 1) * per, n_halo - 1), 0))]

    vec = _whole((2, 1, LRU_W))
    mat = _whole((2, LRU_W, LRU_W))
    return pl.pallas_call(
        functools.partial(_lru_kernel, n_ctx=n_ctx, n_all=n_all),
        grid=(b, n_all),
        in_specs=specs(ident) + specs(mirr) + [_whole((4, LRU_W)), _whole((1, LRU_W)), mat, vec, mat, vec, vec],
        out_specs=[pl.BlockSpec((None, LRU_T, LRU_W), lambda bb, j: (bb, j, 0)),
                   pl.BlockSpec((None, LRU_T, LRU_W), lambda bb, j: (bb, mirr(j), 0))],
        out_shape=[jax.ShapeDtypeStruct((b, s, LRU_W), F32)] * 2,
        scratch_shapes=[pltpu.VMEM((2, 1, LRU_W), F32),
                        pltpu.VMEM((2, LRU_W // LANE, LRU_T, LANE), F32),
                        pltpu.VMEM((2, LRU_W // LANE, LRU_T, LANE), F32),
                        pltpu.VMEM((2, LRU_T + 2 * LRU_HALO, LRU_W), F32)],
        compiler_params=_cp(("parallel", "arbitrary"), VMEM_LIMIT),
        name="rglru_scan",
    )(zd, zd, zd, zd, zd, zd, p["conv_w"], p["conv_b"], p["wa"], p["ba"], p["wx"], p["bx"], p["lam"])


def _out_kernel(*refs, n_src, ctx_tiles, off):
    gate_ref, hfa_ref, hba_ref, o_ref, gn_ref, yb_ref, yc_ref, hfd_ref, hbd_ref, gd_ref, w_ref, out_ref = refs[n_src:]
    pair = [(0, ML_DH), (ML_DH, 2 * ML_DH)]
    gm_pair = _group_mean_matrix(LANE, pair)
    ha = hfa_ref[...] + hba_ref[...]
    ha = jnp.concatenate([_group_rms(ha[:, :LANE], pair, gm_pair), _group_rms(ha[:, LANE:], pair, gm_pair)],
                         axis=-1) * gn_ref[...]
    ya = (jax.nn.sigmoid(o_ref[...].astype(F32)) * ha).astype(BF16)
    yd = (jax.nn.gelu(gd_ref[...]) * (hfd_ref[...] + hbd_ref[...])).astype(BF16)
    acc = jnp.dot(ya, w_ref[0:ML_W, :], preferred_element_type=F32)
    acc = acc + jnp.dot(yb_ref[...], w_ref[ML_W:2 * ML_W, :], preferred_element_type=F32)
    acc = acc + jnp.dot(yc_ref[...], w_ref[2 * ML_W:3 * ML_W, :], preferred_element_type=F32)
    acc = acc + jnp.dot(yd, w_ref[3 * ML_W:4 * ML_W, :], preferred_element_type=F32)
    out_ref[...] = _src_rows(refs[:n_src], ctx_tiles, off) + gate_ref[...] * acc


def _out_proj(src, mod, hfa, hba, za, gn, yb, yc, hfd, hbd, zd, w, nb, ctx_tiles, off):
    b, s = src[0].shape[0], sum(a.shape[1] for a in src)
    nt = s // TM - off
    r = lambda w_, col=0: _rows(w_, off, col)
    return pl.pallas_call(
        functools.partial(_out_kernel, n_src=len(src), ctx_tiles=ctx_tiles, off=off),
        grid=(b, nt),
        in_specs=_src_specs(src, ctx_tiles, off) + [
                  _mod_spec(2, nb, ctx_tiles, off), r(ML_W), r(ML_W), r(ML_W, 3), _whole((1, ML_W)),
                  r(ML_W), r(ML_W), r(ML_W), r(ML_W), r(ML_W, 1), _whole((D, D))],
        out_specs=_rows(D),
        out_shape=jax.ShapeDtypeStruct((b, nt * TM, D), F32),
        compiler_params=_cp(("parallel", "parallel"), VMEM_LIMIT),
        name="out_proj",
    )(*src, mod, hfa, hba, za, gn, yb, yc, hfd, hbd, zd, w)


def _ffn_kernel(x_ref, g_ref, sh_ref, sc_ref, gate_ref, w13_ref, w2_ref, out_ref, *, d_ff):
    x = x_ref[...]
    h = _rms(x) * g_ref[...]
    hb = (h * (1.0 + sc_ref[...]) + sh_ref[...]).astype(BF16)
    a = jnp.dot(hb, w13_ref[:, :d_ff], preferred_element_type=F32)
    g = jnp.dot(hb, w13_ref[:, d_ff:], preferred_element_type=F32)
    u = (g * jax.nn.sigmoid(g) * a).astype(BF16)
    out_ref[...] = x + gate_ref[...] * jnp.dot(u, w2_ref[...], preferred_element_type=F32)


def _ffn(xa, gain, mod, w13, w2, nb, ctx_tiles):
    b, s, _ = xa.shape
    d_ff = w2.shape[0]
    return pl.pallas_call(
        functools.partial(_ffn_kernel, d_ff=d_ff),
        grid=(b, s // TM),
        in_specs=[_rows(D), _whole((1, D)), _mod_spec(3, nb, ctx_tiles, 0), _mod_spec(4, nb, ctx_tiles, 0),
                  _mod_spec(5, nb, ctx_tiles, 0), _whole((D, 2 * d_ff)), _whole((d_ff, D))],
        out_specs=_rows(D),
        out_shape=jax.ShapeDtypeStruct((b, s, D), F32),
        compiler_params=_cp(("parallel", "parallel"), VMEM_LIMIT),
        name="ffn",
    )(xa, gain.reshape(1, D), mod, mod, mod, w13, w2)


META_E, META_RANK, META_GATE = 0, 2, 4


def _router_kernel(x_ref, g_ref, sh_ref, sc_ref, wr_ref, br_ref, h_ref, meta_ref, cnt_ref):
    @pl.when((pl.program_id(0) == 0) & (pl.program_id(1) == 0))
    def _():
        cnt_ref[...] = jnp.zeros_like(cnt_ref)

    h = _rms(x_ref[...]) * g_ref[...]
    h = h * (1.0 + sc_ref[...]) + sh_ref[...]
    h_ref[...] = h
    logits = jnp.dot(h, wr_ref[...], precision=HI, preferred_element_type=F32) + br_ref[...]
    lane = _lane(logits.shape)
    logits = jnp.where(lane < N_EXP, logits, NEG)
    m1 = jnp.max(logits, axis=-1, keepdims=True)
    i1 = jnp.min(jnp.where(logits == m1, lane, LANE), axis=-1, keepdims=True)
    rest = jnp.where(lane == i1, NEG, logits)
    m2 = jnp.max(rest, axis=-1, keepdims=True)
    i2 = jnp.min(jnp.where(rest == m2, lane, LANE), axis=-1, keepdims=True)
    e2 = jnp.exp(m2 - m1)
    inv = 1.0 / (1.0 + e2)
    hit1, hit2 = lane == i1, lane == i2
    assign = (hit1 | hit2).astype(F32)
    rr = lax.broadcasted_iota(jnp.int32, (TM, TM), 0)
    cc = lax.broadcasted_iota(jnp.int32, (TM, TM), 1)
    before = jnp.dot((cc < rr).astype(BF16), assign.astype(BF16), preferred_element_type=F32) + cnt_ref[...]
    r1 = jnp.sum(jnp.where(hit1, before, 0.0), axis=-1, keepdims=True)
    r2 = jnp.sum(jnp.where(hit2, before, 0.0), axis=-1, keepdims=True)
    cnt_ref[...] = cnt_ref[...] + jnp.sum(assign, axis=0, keepdims=True)
    fields = (i1.astype(F32), i2.astype(F32), r1, r2, inv, e2 * inv)
    meta = jnp.zeros(logits.shape, F32)
    for k, val in enumerate(fields):
        meta = jnp.where(lane == k, val, meta)
    meta_ref[...] = meta


def _router(xl, gain, mod, wr, br):
    b, t, _ = xl.shape
    spec = lambda k: pl.BlockSpec((None, 1, D), lambda bb, j: (bb, 0, k))
    return pl.pallas_call(
        _router_kernel,
        grid=(b, t // TM),
        in_specs=[_rows(D), _whole((1, D)), spec(3), spec(4), _whole((D, LANE)), _whole((1, LANE))],
        out_specs=[_rows(D), _rows(LANE), _whole((1, LANE))],
        out_shape=[jax.ShapeDtypeStruct((b, t, D), F32), jax.ShapeDtypeStruct((b, t, LANE), F32),
                   jax.ShapeDtypeStruct((1, LANE), F32)],
        compiler_params=_cp(("arbitrary", "arbitrary"), VMEM_LIMIT),
        name="router",
    )(xl, gain.reshape(1, D), mod, mod, wr, br)


TG = 256
TOP_K = 2


ROW_DMA_UNROLL = 8


def _issue_row_dmas(route_ref, make_copy):
    def issue(r, c):
        for k in range(TOP_K):
            make_copy(r, k, route_ref[0, k * TM + r]).start()
        return c

    lax.fori_loop(0, TM, issue, 0, unroll=ROW_DMA_UNROLL)


def _dispatch_kernel(route_ref, fill_ref, h_ref, xs_ref, zero_ref, sem, fill_sem):
    @pl.when(pl.program_id(0) == 0)
    def _():
        zero_ref[...] = jnp.zeros_like(zero_ref)
        zero_row = lambda pos: pltpu.make_async_copy(zero_ref.at[pl.ds(0, 1)], xs_ref.at[pl.ds(pos, 1)], fill_sem)
        for e in range(N_EXP + 1):
            first, count = fill_ref[e], fill_ref[N_EXP + 1 + e]
            lax.fori_loop(0, count, lambda i, c: (zero_row(first + i).start(), c)[1], 0)
            lax.fori_loop(0, count, lambda i, c: (zero_row(first).wait(), c)[1], 0)

    _issue_row_dmas(route_ref, lambda r, k, pos: pltpu.make_async_copy(h_ref.at[pl.ds(r, 1)],
                                                                         xs_ref.at[pl.ds(pos, 1)], sem))
    for _ in range(TOP_K):
        pltpu.make_async_copy(h_ref, xs_ref.at[pl.ds(0, TM)], sem).wait()


def _dispatch(h2, route, fill, n_rows):
    n = h2.shape[0]
    return pl.pallas_call(
        _dispatch_kernel,
        grid=(n // TM,),
        in_specs=[pl.BlockSpec((None, 1, TOP_K * TM), lambda i: (i, 0, 0), memory_space=pltpu.SMEM),
                  pl.BlockSpec(memory_space=pltpu.SMEM),
                  pl.BlockSpec((TM, D), lambda i: (i, 0))],
        out_specs=pl.BlockSpec(memory_space=pl.ANY),
        out_shape=jax.ShapeDtypeStruct((n_rows, D), F32),
        scratch_shapes=[pltpu.VMEM((SUB, D), F32), pltpu.SemaphoreType.DMA(()), pltpu.SemaphoreType.DMA(())],
        compiler_params=_cp(("arbitrary",), VMEM_LIMIT),
        name="moe_dispatch",
    )(route, fill, h2)


def _expert_kernel(te_ref, nu_ref, xs_ref, w13_ref, w2_ref, y_ref, *, d_ff):
    del te_ref
    t = pl.program_id(0)

    @pl.when(t < nu_ref[0])
    def _():
        xb = xs_ref[...].astype(BF16)
        a = jnp.dot(xb, w13_ref[:, :d_ff], preferred_element_type=F32)
        g = jnp.dot(xb, w13_ref[:, d_ff:], preferred_element_type=F32)
        u = (g * jax.nn.sigmoid(g) * a).astype(BF16)
        y_ref[...] = jnp.dot(u, w2_ref[...], preferred_element_type=F32)

    @pl.when(t >= nu_ref[0])
    def _():
        y_ref[...] = jnp.zeros_like(y_ref)


def _experts(xs, tile_expert, n_used, w13, w2):
    n_rows = xs.shape[0]
    d_ff = w2.shape[1]
    once = dict(pipeline_mode=pl.Buffered(1))
    grid_spec = pltpu.PrefetchScalarGridSpec(
        num_scalar_prefetch=2,
        grid=(n_rows // TG,),
        in_specs=[pl.BlockSpec((TG, D), lambda t, te, nu: (jnp.minimum(t, nu[0] - 1), 0)),
                  pl.BlockSpec((None, D, 2 * d_ff), lambda t, te, nu: (te[t], 0, 0), **once),
                  pl.BlockSpec((None, d_ff, D), lambda t, te, nu: (te[t], 0, 0), **once)],
        out_specs=pl.BlockSpec((TG, D), lambda t, te, nu: (t, 0)),
    )
    return pl.pallas_call(
        functools.partial(_expert_kernel, d_ff=d_ff),
        grid_spec=grid_spec,
        out_shape=jax.ShapeDtypeStruct((n_rows, D), F32),
        compiler_params=_cp(("arbitrary",), VMEM_LIMIT),
        name="moe_experts",
    )(tile_expert, n_used, xs, w13, w2)


def _combine_kernel(route_ref, next_ref, x_ref, meta_ref, gate_ref, y_ref, out_ref, buf_ref, sem, *, n_steps):
    step = pl.program_id(0) * pl.num_programs(1) + pl.program_id(1)
    slot = step & 1

    def gather(rows_ref, s):
        _issue_row_dmas(rows_ref, lambda r, k, pos: pltpu.make_async_copy(
            y_ref.at[pl.ds(pos, 1)], buf_ref.at[s, k, pl.ds(r, 1)], sem.at[s]))

    @pl.when(step == 0)
    def _():
        gather(route_ref, 0)

    @pl.when(step + 1 < n_steps)
    def _():
        gather(next_ref, 1 - slot)

    for k in range(TOP_K):
        pltpu.make_async_copy(y_ref.at[pl.ds(0, TM)], buf_ref.at[slot, k], sem.at[slot]).wait()
    meta = meta_ref[...]
    mix = (meta[:, META_GATE:META_GATE + 1] * buf_ref[slot, 0]
           + meta[:, META_GATE + 1:META_GATE + 2] * buf_ref[slot, 1])
    out_ref[...] = x_ref[...] + gate_ref[...] * mix


def _combine(xl, meta, mod, y, route):
    b, t, _ = xl.shape
    tiles = t // TM
    n_steps = b * tiles
    rows_of = lambda ahead: pl.BlockSpec((None, 1, TOP_K * TM),
                                         lambda bb, j: (jnp.minimum(bb * tiles + j + ahead, n_steps - 1), 0, 0),
                                         memory_space=pltpu.SMEM)
    return pl.pallas_call(
        functools.partial(_combine_kernel, n_steps=n_steps),
        grid=(b, tiles),
        in_specs=[rows_of(0), rows_of(1), _rows(D), _rows(LANE), pl.BlockSpec((None, 1, D), lambda bb, j: (bb, 0, 5)),
                  pl.BlockSpec(memory_space=pl.ANY)],
        out_specs=_rows(D),
        out_shape=jax.ShapeDtypeStruct((b, t, D), F32),
        scratch_shapes=[pltpu.VMEM((2, TOP_K, TM, D), F32), pltpu.SemaphoreType.DMA((2,))],
        compiler_params=_cp(("arbitrary", "arbitrary"), VMEM_LIMIT),
        name="moe_combine",
    )(route, route, xl, meta, mod, y)


def _moe(xl, gain, mod, wr, br, w13, w2):
    b, t, _ = xl.shape
    n = b * t
    h, meta, cnt = _router(xl, gain, mod, wr, br)
    counts = cnt[0, :N_EXP].astype(jnp.int32)
    padded = (counts + TG - 1) // TG * TG
    ends = jnp.cumsum(padded)
    start = ends - padded
    n_tiles = TOP_K * n // TG + N_EXP
    tile_lo = jnp.arange(n_tiles, dtype=jnp.int32) * TG
    tile_expert = jnp.minimum(jnp.sum((tile_lo[:, None] >= ends[None, :]).astype(jnp.int32), axis=1), N_EXP - 1)
    n_used = (ends[-1:] // TG).astype(jnp.int32)
    ids = meta.reshape(n // TM, TM, LANE)[:, :, :2 * TOP_K].astype(jnp.int32)
    expert, rank = ids[:, :, META_E:META_E + TOP_K], ids[:, :, META_RANK:META_RANK + TOP_K]
    pos = rank + sum(jnp.where(expert == e, start[e], 0) for e in range(N_EXP))
    route = jnp.swapaxes(pos, 1, 2).reshape(n // TM, 1, TOP_K * TM)
    n_rows = n_tiles * TG
    fill = jnp.concatenate([start + counts, ends[-1:], padded - counts, n_rows - ends[-1:]])
    xs = _dispatch(h.reshape(n, D), route, fill, n_rows)
    y = _experts(xs, tile_expert, n_used, w13, w2)
    return _combine(xl, meta, mod, y, route)


def _pad_cols(w, n):
    return jnp.pad(w, ((0, 0), (0, n - w.shape[1])))


def _layout_w_in(w):
    ml = 4 * ML_W
    a, gates = w[:, :ml], w[:, ml:ml + 4 * ML_H]
    o = ml + 4 * ML_H
    cq, ckv, kr = w[:, o:o + MLA_QR], w[:, o + MLA_QR:o + MLA_QR + MLA_KVR], w[:, o + MLA_QR + MLA_KVR:o + MLA_QR + MLA_KVR + MLA_ROPE]
    o += MLA_QR + MLA_KVR + MLA_ROPE
    sw = w[:, o:o + (SW_H + 2 * SW_KV) * SW_DH]
    o += (SW_H + 2 * SW_KV) * SW_DH
    lru = w[:, o:]
    z = lambda n: jnp.zeros((w.shape[0], n), w.dtype)
    cols = [a, _pad_cols(gates, LANE), _pad_cols(cq, 2 * LANE), ckv, z(MLA_NOPE), kr, z(LANE - MLA_NOPE - MLA_ROPE), sw, lru]
    return jnp.concatenate(cols, axis=1).astype(BF16)


IN_SEGS = ((0, 4 * ML_W), (4 * ML_W, LANE), (4 * ML_W + LANE, 4 * LANE), (4 * ML_W + 5 * LANE, 4 * LANE),
           (4 * ML_W + 9 * LANE, 4 * LANE))
IN_DTYPES = (BF16, F32, F32, F32, F32)


def _row(v, n=None):
    v = v.reshape(1, -1)
    return v if n is None else _pad_cols(v, n)


def _layer_params(l, ml_gate_b, ml_out_norm, mla_q_norm, mla_w_uq, mla_kv_norm, mla_w_ukv, mla_q_gain, mla_k_gain,
                  sw_q_gain, sw_k_gain, lru_conv_w, lru_conv_b, lru_wa, lru_ba, lru_wx, lru_bx, lru_lam):
    hq = MLA_NOPE + MLA_ROPE
    wq = mla_w_uq[l].reshape(MLA_QR, MLA_H, hq)
    wq = jnp.pad(wq, ((0, 2 * LANE - MLA_QR), (0, 0), (0, LANE - hq))).reshape(2 * LANE, MLA_H * LANE)
    wkv = mla_w_ukv[l].reshape(MLA_KVR, MLA_H, MLA_NOPE + MLA_V)
    wk = jnp.pad(wkv[:, :, :MLA_NOPE], ((0, 0), (0, 0), (0, LANE - MLA_NOPE))).reshape(MLA_KVR, MLA_H * LANE)
    wv = wkv[:, :, MLA_NOPE:].reshape(MLA_KVR, MLA_H * MLA_V)

    def blockdiag(wb):
        bw = LRU_W // LRU_BLOCKS
        eye = jnp.eye(LRU_BLOCKS, dtype=wb.dtype)
        return jnp.einsum('xncd,nm->xncmd', wb, eye).reshape(2, LRU_W, LRU_W)

    return dict(
        gate_b=_row(ml_gate_b[l], LANE), out_norm=_row(ml_out_norm[l]),
        q_norm=_row(mla_q_norm[l], 2 * LANE), w_uq=wq.astype(BF16), kv_norm=_row(mla_kv_norm[l]),
        w_uk=wk.astype(BF16), w_uv=wv.astype(BF16),
        q_gain=_row(mla_q_gain[l], LANE), k_gain=_row(mla_k_gain[l], LANE),
        sw_q_gain=_row(jnp.tile(sw_q_gain[l], 2)), sw_k_gain=_row(jnp.tile(sw_k_gain[l], 2)),
        conv_w=lru_conv_w[l], conv_b=_row(lru_conv_b[l]),
        wa=blockdiag(lru_wa[l]).astype(BF16), ba=lru_ba[l].reshape(2, 1, LRU_W),
        wx=blockdiag(lru_wx[l]).astype(BF16), bx=lru_bx[l].reshape(2, 1, LRU_W),
        lam=lru_lam[l].reshape(2, 1, LRU_W),
    )


def _mix_layer(src, mod, gain, w_in_l, w_out_l, sink, p, tabs, lc, nb, ctx_out):
    ctx_tiles = lc // TM
    za, zg, zb, zc, zd = _proj(src, gain, mod, w_in_l, IN_SEGS, IN_DTYPES, nb, ctx_tiles)
    hfa, hba = _mlstm(za, zg, p["gate_b"], lc)
    qm, km, vm, qs, ks, vs = _prep(zb, zc, tabs[0], tabs[1], p)
    yb = _mla_attn(qm, km, vm, lc)
    yc = _swa_attn(qs, ks, vs, sink, lc)
    hfd, hbd = _lru(zd, p, lc)
    off = 0 if ctx_out else ctx_tiles
    return _out_proj(src, mod, hfa, hba, za, p["out_norm"], yb, yc, hfd, hbd, zd, w_out_l, nb, ctx_tiles, off)


def kernel(x, c, ctx, c_ctx, ada_w, ada_b, norm_mix, norm_ffn, w_in, w_out, ml_gate_b, ml_out_norm, mla_q_norm, mla_w_uq, mla_kv_norm, mla_w_ukv, mla_q_gain, mla_k_gain, sw_q_gain, sw_k_gain, sw_sink, lru_conv_w, lru_conv_b, lru_wa, lru_ba, lru_wx, lru_bx, lru_lam, ffn_w13, ffn_w2, moe_router, moe_router_b, moe_w13, moe_w2):
    nb, t_len, _ = x.shape
    lc = ctx.shape[1]
    depth = ada_w.shape[0]
    assert nb < SUB and lc % TM == 0 and t_len % TM == 0 and t_len % GRID_W == 0
    cc = jnp.zeros((SUB, D), F32).at[:nb].set(c).at[nb].set(c_ctx)
    mods = _ada_mod(cc, ada_w, ada_b)
    tabs = (_rope_tables(t_len, lc, LANE, MLA_ROPE, MLA_NOPE), _rope_tables(t_len, lc, SW_DH, SW_DH, 0))
    src = (ctx, x)
    for l in range(depth):
        ctx_out = l < depth - 1
        mod = mods[l].reshape(SUB, 1, 6 * D)
        p = _layer_params(l, ml_gate_b, ml_out_norm, mla_q_norm, mla_w_uq, mla_kv_norm, mla_w_ukv, mla_q_gain,
                          mla_k_gain, sw_q_gain, sw_k_gain, lru_conv_w, lru_conv_b, lru_wa, lru_ba, lru_wx, lru_bx,
                          lru_lam)
        xa = _mix_layer(src, mod, norm_mix[l], _layout_w_in(w_in[l]), w_out[l].astype(BF16), sw_sink[l], p, tabs,
                        lc, nb, ctx_out)
        if l % 2 == 0:
            xa = _ffn(xa, norm_ffn[l], mod, ffn_w13[l // 2].astype(BF16), ffn_w2[l // 2].astype(BF16), nb,
                      lc // TM if ctx_out else 0)
        else:
            xa = _moe(xa, norm_ffn[l], mod, _pad_cols(moe_router[l // 2], LANE), _row(moe_router_b[l // 2], LANE),
                      moe_w13[l // 2].astype(BF16), moe_w2[l // 2].astype(BF16))
        src = (xa,)
    return xa
```

```python
import functools
import itertools

import numpy as np
import jax
import jax.numpy as jnp
from jax import lax
from jax.experimental import pallas as pl
from jax.experimental.pallas import tpu as pltpu

F32 = jnp.float32
BF16 = jnp.bfloat16
HI = lax.Precision.HIGHEST

D = 1024
GRID_W = 64
EPS = 1e-6
ROPE_BASE = 10000.0
ML_H, ML_DH, ML_W, ML_CHUNK = 4, 64, 256, 128
MLA_H, MLA_QR, MLA_KVR, MLA_NOPE, MLA_ROPE, MLA_V = 4, 192, 128, 64, 32, 64
SW_H, SW_KV, SW_DH, SW_WINDOW, SW_BLOCK = 4, 2, 64, 128, 128
LRU_W, LRU_BLOCKS, LRU_C = 256, 4, 8.0
N_EXP = 8

LANE = 128
SUB = 8
TM = 256
NEG = -1e30
VMEM_LIMIT = 56 * 1024 * 1024


def _cp(sem, vmem=None):
    return pltpu.CompilerParams(dimension_semantics=sem, vmem_limit_bytes=vmem)


def _mod_spec(chunk, nb, ctx_tiles, off):
    return pl.BlockSpec((None, 1, D), lambda b, j: (jnp.where(j + off < ctx_tiles, nb, b), 0, chunk))


def _rows(width, off=0, col=0, tm=TM):
    return pl.BlockSpec((None, tm, width), lambda b, j: (b, j + off, col))


def _whole(shape):
    nd = len(shape)
    return pl.BlockSpec(shape, lambda *_: (0,) * nd)


def _src_specs(src, ctx_tiles, off):
    if len(src) == 1:
        return [_rows(D, off)]
    return [pl.BlockSpec((None, TM, D), lambda b, j: (b, jnp.minimum(j + off, ctx_tiles - 1), 0)),
            pl.BlockSpec((None, TM, D), lambda b, j: (b, jnp.maximum(j + off - ctx_tiles, 0), 0))]


def _src_rows(src_refs, ctx_tiles, off):
    if len(src_refs) == 1:
        return src_refs[0][...]
    return jnp.where(pl.program_id(1) + off < ctx_tiles, src_refs[0][...], src_refs[1][...])


def _rms(x):
    return x * lax.rsqrt(jnp.mean(x * x, axis=-1, keepdims=True) + EPS)


def _lane(shape):
    return lax.broadcasted_iota(jnp.int32, shape, len(shape) - 1)


def _group_mean_matrix(n, bounds):
    i = lax.broadcasted_iota(jnp.int32, (n, n), 0)
    j = lax.broadcasted_iota(jnp.int32, (n, n), 1)
    g = jnp.zeros((n, n), F32)
    for lo, hi in bounds:
        g = jnp.where((i >= lo) & (i < hi) & (j >= lo) & (j < hi), 1.0 / (hi - lo), g)
    return g.astype(BF16)


def _group_rms(blk, bounds, gm):
    lane = _lane(blk.shape)
    sq = blk * blk
    hi_part = sq.astype(BF16)
    lo_part = (sq - hi_part.astype(F32)).astype(BF16)
    ms = (jnp.dot(hi_part, gm, preferred_element_type=F32) + jnp.dot(lo_part, gm, preferred_element_type=F32))
    inside = functools.reduce(jnp.logical_or, [(lane >= lo) & (lane < hi) for lo, hi in bounds])
    return blk * jnp.where(inside, lax.rsqrt(ms + EPS), 0.0)


def _rope(blk, tab_ref, shift):
    n = blk.shape[-1]
    return (blk * tab_ref[0] + pltpu.roll(blk, n - shift, 1) * tab_ref[1]
            + pltpu.roll(blk, shift, 1) * tab_ref[2])


def _rope_tables(t_len, lc, width, dims, offset):
    half, nf = dims // 2, dims // 4
    p = np.arange(dims)
    i = p % half
    f = i % nf
    first = i < nf
    freq = ROPE_BASE ** (-f.astype(np.float64) / nf)
    t = np.arange(t_len)
    pos = np.where(p[None, :] < half, (t // GRID_W)[:, None], (t % GRID_W)[:, None]).astype(np.float64)
    ang = pos * freq[None, :]
    cos, sin = np.cos(ang), np.sin(ang)
    tab = np.zeros((3, lc + t_len, width), np.float64)
    tab[0] = 1.0
    tab[0, lc:, offset:offset + dims] = cos
    tab[1, lc:, offset:offset + dims] = np.where(first[None, :], -sin, 0.0)
    tab[2, lc:, offset:offset + dims] = np.where(first[None, :], 0.0, sin)
    reps = LANE // width
    return jnp.asarray(np.tile(tab, (1, 1, reps)), F32)


def _ada_kernel(c_ref, w_ref, b_ref, o_ref):
    c = c_ref[...]
    s = c * jax.nn.sigmoid(c)
    o_ref[...] = jnp.dot(s, w_ref[...], precision=HI, preferred_element_type=F32) + b_ref[...]


def _ada_mod(cc, ada_w, ada_b):
    depth = ada_w.shape[0]
    n = ada_w.shape[2]
    tn = 1024
    return pl.pallas_call(
        _ada_kernel,
        grid=(depth, n // tn),
        in_specs=[
            pl.BlockSpec((SUB, D), lambda l, j: (0, 0)),
            pl.BlockSpec((None, D, tn), lambda l, j: (l, 0, j)),
            pl.BlockSpec((None, 1, tn), lambda l, j: (l, 0, j)),
        ],
        out_specs=pl.BlockSpec((None, SUB, tn), lambda l, j: (l, 0, j)),
        out_shape=jax.ShapeDtypeStruct((depth, SUB, n), F32),
        compiler_params=_cp(("parallel", "parallel")),
        name="ada_mod",
    )(cc, ada_w, ada_b.reshape(depth, 1, n))


def _proj_kernel(*refs, segs, n_src, ctx_tiles):
    (g_ref, sh_ref, sc_ref, w_ref), out_refs = refs[n_src:n_src + 4], refs[n_src + 4:]
    h = _rms(_src_rows(refs[:n_src], ctx_tiles, 0)) * g_ref[...]
    h = h * (1.0 + sc_ref[...]) + sh_ref[...]
    hb = h.astype(BF16)
    for (start, width), o_ref in zip(segs, out_refs):
        o_ref[...] = jnp.dot(hb, w_ref[:, start:start + width], preferred_element_type=F32).astype(o_ref.dtype)


def _proj(src, gain, mod, w, segs, dtypes, nb, ctx_tiles):
    b, s = src[0].shape[0], sum(a.shape[1] for a in src)
    n = w.shape[1]
    return pl.pallas_call(
        functools.partial(_proj_kernel, segs=segs, n_src=len(src), ctx_tiles=ctx_tiles),
        grid=(b, s // TM),
        in_specs=_src_specs(src, ctx_tiles, 0) + [
            _whole((1, D)),
            _mod_spec(0, nb, ctx_tiles, 0),
            _mod_spec(1, nb, ctx_tiles, 0),
            _whole((D, n)),
        ],
        out_specs=[_rows(wd) for _, wd in segs],
        out_shape=[jax.ShapeDtypeStruct((b, s, wd), dt) for (_, wd), dt in zip(segs, dtypes)],
        compiler_params=_cp(("parallel", "parallel"), VMEM_LIMIT),
        name="in_proj",
    )(*src, gain.reshape(1, D), mod, mod, w)


def _prep_kernel(zb_ref, zc_ref, tm_ref, ts_ref, qn_ref, wuq_ref, kvn_ref, wk_ref, wv_ref, qg_ref, kg_ref,
                 sqg_ref, skg_ref, qm_ref, km_ref, vm_ref, qs_ref, ks_ref, vs_ref):
    zb = zb_ref[...]
    cq = zb[:, :2 * LANE]
    cqn = cq * lax.rsqrt(jnp.sum(cq * cq, axis=-1, keepdims=True) * (1.0 / MLA_QR) + EPS) * qn_ref[...]
    qf = jnp.dot(cqn.astype(BF16), wuq_ref[...], preferred_element_type=F32)
    q_scale = (MLA_NOPE + MLA_ROPE) ** -0.5 * LOG2E
    head_bounds = [(0, MLA_NOPE), (MLA_NOPE, MLA_NOPE + MLA_ROPE)]
    gm_head = _group_mean_matrix(LANE, head_bounds)
    for h in range(MLA_H):
        blk = _group_rms(qf[:, h * LANE:(h + 1) * LANE], head_bounds, gm_head) * qg_ref[...]
        qm_ref[h] = (_rope(blk, tm_ref, MLA_ROPE // 4) * q_scale).astype(BF16)
    ckvn = _rms(zb[:, 2 * LANE:3 * LANE]) * kvn_ref[...]
    ckvb = ckvn.astype(BF16)
    kf = jnp.dot(ckvb, wk_ref[...], preferred_element_type=F32)
    vf = jnp.dot(ckvb, wv_ref[...], preferred_element_type=F32)
    kg = kg_ref[...]
    kr = _group_rms(zb[:, 3 * LANE:4 * LANE], head_bounds[1:], gm_head) * kg
    kr = _rope(kr, tm_ref, MLA_ROPE // 4)
    for h in range(MLA_H):
        kn = _group_rms(kf[:, h * LANE:(h + 1) * LANE], head_bounds[:1], gm_head) * kg
        km_ref[h] = (kn + kr).astype(BF16)
        ones = (_lane((vf.shape[0], MLA_VA - MLA_V)) == 0).astype(F32)
        vm_ref[h] = jnp.concatenate([vf[:, h * MLA_V:(h + 1) * MLA_V], ones], axis=-1).astype(BF16)
    zc = zc_ref[...]
    pair = [(0, SW_DH), (SW_DH, 2 * SW_DH)]
    gm_pair = _group_mean_matrix(LANE, pair)
    sw_scale = SW_DH ** -0.5
    for half in range(2):
        blk = _group_rms(zc[:, half * LANE:(half + 1) * LANE], pair, gm_pair) * sqg_ref[...]
        qs_ref[:, half * LANE:(half + 1) * LANE] = (_rope(blk, ts_ref, SW_DH // 4) * sw_scale).astype(BF16)
    kb = _rope(_group_rms(zc[:, 2 * LANE:3 * LANE], pair, gm_pair) * skg_ref[...], ts_ref, SW_DH // 4).astype(BF16)
    vb = zc[:, 3 * LANE:4 * LANE].astype(BF16)
    for j in range(SW_KV):
        ks_ref[j] = kb[:, j * SW_DH:(j + 1) * SW_DH]
        vs_ref[j] = vb[:, j * SW_DH:(j + 1) * SW_DH]


def _prep(zb, zc, tab_mla, tab_sw, p):
    b, s, _ = zb.shape
    tab = pl.BlockSpec((3, TM, LANE), lambda bb, j: (0, j, 0))
    heads = lambda nh, w: pl.BlockSpec((None, nh, TM, w), lambda bb, j: (bb, 0, j, 0))
    return pl.pallas_call(
        _prep_kernel,
        grid=(b, s // TM),
        in_specs=[_rows(4 * LANE), _rows(4 * LANE), tab, tab,
                  _whole((1, 2 * LANE)), _whole((2 * LANE, 4 * LANE)), _whole((1, LANE)),
                  _whole((LANE, 4 * LANE)), _whole((LANE, 2 * LANE)), _whole((1, LANE)), _whole((1, LANE)),
                  _whole((1, LANE)), _whole((1, LANE))],
        out_specs=[heads(MLA_H, LANE), heads(MLA_H, LANE), heads(MLA_H, MLA_VA),
                   _rows(2 * LANE), heads(SW_KV, SW_DH), heads(SW_KV, SW_DH)],
        out_shape=[jax.ShapeDtypeStruct((b, MLA_H, s, LANE), BF16),
                   jax.ShapeDtypeStruct((b, MLA_H, s, LANE), BF16),
                   jax.ShapeDtypeStruct((b, MLA_H, s, MLA_VA), BF16),
                   jax.ShapeDtypeStruct((b, s, 2 * LANE), BF16),
                   jax.ShapeDtypeStruct((b, SW_KV, s, SW_DH), BF16),
                   jax.ShapeDtypeStruct((b, SW_KV, s, SW_DH), BF16)],
        compiler_params=_cp(("parallel", "parallel"), VMEM_LIMIT),
        name="head_prep",
    )(zb, zc, tab_mla, tab_sw, p["q_norm"], p["w_uq"], p["kv_norm"], p["w_uk"], p["w_uv"], p["q_gain"],
      p["k_gain"], p["sw_q_gain"], p["sw_k_gain"])


MLA_TQ = 256
MLA_KC = 256
LOG2E = 1.4426950408889634
MLA_VA = MLA_V + 16
MLA_UNROLL = 2


def _mla_kernel(qt_ref, k_ref, vt_ref, o_ref, m_ref, acc_ref, st_ref, cm_ref, p_ref, al_ref, *, n_all, n_ctx,
                ctx_tiles):
    i = pl.program_id(1)
    n_kv = jnp.where(i < ctx_tiles, n_ctx, n_all)
    last = n_kv - 1

    def scores(c, slot):
        off = pl.multiple_of(jnp.minimum(c, last) * MLA_KC, MLA_KC)
        for h in range(MLA_H):
            st = jnp.dot(k_ref[h, pl.ds(off, MLA_KC), :], qt_ref[h], preferred_element_type=F32)
            st_ref[slot, h] = st
            cm_ref[slot, h] = jnp.max(st, axis=0, keepdims=True)

    def softmax(slot, first):
        for h in range(MLA_H):
            cm = cm_ref[slot, h]
            m_new = cm if first else jnp.maximum(m_ref[h], cm)
            p_ref[slot, h] = jnp.exp2(st_ref[slot, h] - m_new).astype(BF16)
            al_ref[slot, h] = jnp.ones_like(cm) if first else jnp.exp2(m_ref[h] - m_new)
            m_ref[h] = m_new

    def weighted_values(c, slot):
        pv = [jnp.dot(vt_ref[h, c], p_ref[slot, h], preferred_element_type=F32) for h in range(MLA_H)]
        return pv

    def accumulate(pv, slot):
        for h in range(MLA_H):
            acc_ref[h] = al_ref[slot, h] * acc_ref[h] + pv[h]

    acc_ref[...] = jnp.zeros_like(acc_ref)
    scores(0, 0)
    softmax(0, True)
    scores(1, 1)

    def body(j, carry):
        for u in range(MLA_UNROLL):
            c, slot = MLA_UNROLL * j + u, u & 1
            pv = weighted_values(c, slot)
            scores(c + 2, slot)
            softmax(1 - slot, False)
            accumulate(pv, slot)
        return carry

    lax.fori_loop(0, last // MLA_UNROLL, body, 0)
    accumulate(weighted_values(last, 0), 0)
    outs = [(acc_ref[h, :MLA_V] / acc_ref[h, MLA_V:MLA_V + 1]).T for h in range(MLA_H)]
    o_ref[...] = jnp.concatenate(outs, axis=-1).astype(o_ref.dtype)


def _mla_attn(q, k, v, lc):
    b, h, s, _ = q.shape
    nc = s // MLA_KC
    assert (nc - 1) % MLA_UNROLL == 0 and lc == MLA_KC
    qt = jnp.swapaxes(q, 2, 3)
    vt = jnp.swapaxes(v.reshape(b, h, nc, MLA_KC, MLA_VA), 3, 4)
    kern = functools.partial(_mla_kernel, n_all=nc, n_ctx=lc // MLA_KC, ctx_tiles=lc // MLA_TQ)
    return pl.pallas_call(
        kern,
        grid=(b, s // MLA_TQ),
        in_specs=[pl.BlockSpec((None, h, LANE, MLA_TQ), lambda bb, i: (bb, 0, 0, i)),
                  pl.BlockSpec((None, h, s, LANE), lambda bb, i: (bb, 0, 0, 0)),
                  pl.BlockSpec((None, h, nc, MLA_VA, MLA_KC), lambda bb, i: (bb, 0, 0, 0, 0))],
        out_specs=pl.BlockSpec((None, MLA_TQ, h * MLA_V), lambda bb, i: (bb, i, 0)),
        out_shape=jax.ShapeDtypeStruct((b, s, h * MLA_V), BF16),
        scratch_shapes=[pltpu.VMEM((h, 1, MLA_TQ), F32),
                        pltpu.VMEM((h, MLA_VA, MLA_TQ), F32), pltpu.VMEM((2, h, MLA_KC, MLA_TQ), F32),
                        pltpu.VMEM((2, h, 1, MLA_TQ), F32),
                        pltpu.VMEM((2, h, MLA_KC, MLA_TQ), BF16), pltpu.VMEM((2, h, 1, MLA_TQ), F32)],
        compiler_params=_cp(("parallel", "arbitrary"), VMEM_LIMIT),
        name="mla_attn",
    )(qt, k, vt)


SW_STEP_BLOCKS = 2


def _swa_kernel(sink_ref, q_ref, k_ref, v_ref, o_ref, *, lc, t_len):
    blk = SW_BLOCK
    nw = blk + 2 * SW_WINDOW
    dn = (((1,), (1,)), ((), ()))
    row = lax.broadcasted_iota(jnp.int32, (2 * blk, nw), 0)
    col = lax.broadcasted_iota(jnp.int32, (2 * blk, nw), 1)
    rows1 = lax.broadcasted_iota(jnp.int32, (2 * blk, 1), 0)
    chains = []
    for part in range(SW_STEP_BLOCKS):
        n = pl.program_id(1) * SW_STEP_BLOCKS + part
        start = pl.multiple_of(jnp.clip((n - 1) * blk, 0, lc + t_len - nw), blk)
        qpos = n * blk - lc + (row & (blk - 1))
        kpos = start - lc + col
        valid = (n * blk >= lc) & (kpos >= 0) & (kpos < t_len) & (jnp.abs(qpos - kpos) <= SW_WINDOW)
        for j in range(SW_KV):
            q2 = q_ref[part * blk:(part + 1) * blk, j * LANE:(j + 1) * LANE]
            qs = jnp.concatenate([q2[:, :SW_DH], q2[:, SW_DH:]], axis=0)
            s_loc = lax.dot_general(qs, k_ref[j, pl.ds(start, nw), :], dn, preferred_element_type=F32)
            s_ctx = lax.dot_general(qs, k_ref[j, pl.ds(0, lc), :], dn, preferred_element_type=F32)
            chains.append(dict(j=j, start=start, valid=valid, s_loc=s_loc, s_ctx=s_ctx))
    for ch in chains:
        j = ch["j"]
        s_loc, s_ctx = jnp.where(ch["valid"], ch["s_loc"], NEG), ch["s_ctx"]
        sink = jnp.where(rows1 < blk, sink_ref[2 * j], sink_ref[2 * j + 1])
        m = jnp.maximum(jnp.maximum(jnp.max(s_loc, axis=-1, keepdims=True),
                                    jnp.max(s_ctx, axis=-1, keepdims=True)), sink)
        ch["e_loc"] = jnp.exp(s_loc - m)
        ch["e_ctx"] = jnp.exp(s_ctx - m)
        ch["den"] = (jnp.sum(ch["e_loc"], axis=-1, keepdims=True) + jnp.sum(ch["e_ctx"], axis=-1, keepdims=True)
                     + jnp.exp(sink - m))
    for ch in chains:
        j = ch["j"]
        ch["o"] = (jnp.dot(ch["e_loc"].astype(BF16), v_ref[j, pl.ds(ch["start"], nw), :], preferred_element_type=F32)
                   + jnp.dot(ch["e_ctx"].astype(BF16), v_ref[j, pl.ds(0, lc), :], preferred_element_type=F32))
    for part in range(SW_STEP_BLOCKS):
        outs = []
        for ch in chains[part * SW_KV:(part + 1) * SW_KV]:
            o = ch["o"] / ch["den"]
            outs += [o[:blk], o[blk:]]
        o_ref[part * blk:(part + 1) * blk, :] = jnp.concatenate(outs, axis=-1).astype(o_ref.dtype)


def _swa_attn(q, k, v, sink, lc):
    b, s, _ = q.shape
    t_len = s - lc
    sp = k.shape[2]
    rows = SW_STEP_BLOCKS * SW_BLOCK
    kern = functools.partial(_swa_kernel, lc=lc, t_len=t_len)
    return pl.pallas_call(
        kern,
        grid=(b, s // rows),
        in_specs=[pl.BlockSpec(memory_space=pltpu.SMEM),
                  pl.BlockSpec((None, rows, SW_KV * LANE), lambda bb, n: (bb, n, 0)),
                  pl.BlockSpec((None, SW_KV, sp, SW_DH), lambda bb, n: (bb, 0, 0, 0)),
                  pl.BlockSpec((None, SW_KV, sp, SW_DH), lambda bb, n: (bb, 0, 0, 0))],
        out_specs=pl.BlockSpec((None, rows, SW_KV * LANE), lambda bb, n: (bb, n, 0)),
        out_shape=jax.ShapeDtypeStruct((b, s, SW_KV * LANE), BF16),
        compiler_params=_cp(("parallel", "arbitrary"), VMEM_LIMIT),
        name="swa_attn",
    )(sink, q, k, v)


def _mirror(j, n_ctx, n_all):
    return jnp.where(j < n_ctx, n_ctx - 1 - j, n_all + n_ctx - 1 - j)


ML_AUG = ML_DH + SUB
ML_ROWS = 4


def _mlstm_kernel(xf_ref, qtf_ref, vtf_ref, gf_ref, xb_ref, qtb_ref, vtb_ref, gb_ref, bias_ref, hf_ref, hb_ref,
                  c_ref, m_ref):
    L = ML_CHUNK

    @pl.when(pl.program_id(1) == 0)
    def _():
        c_ref[...] = jnp.zeros_like(c_ref)
        m_ref[...] = jnp.zeros_like(m_ref)

    row = lax.broadcasted_iota(jnp.int32, (L, L), 0)
    col = lax.broadcasted_iota(jnp.int32, (L, L), 1)
    aug_row = lax.broadcasted_iota(jnp.int32, (SUB, L), 0)
    dirs = ((xf_ref, qtf_ref, vtf_ref, gf_ref, hf_ref), (xb_ref, qtb_ref, vtb_ref, gb_ref, hb_ref))
    chains = []
    for r, (direction, (x_ref, qt_ref, vt_ref, g_ref, o_ref)) in itertools.product(range(ML_ROWS), enumerate(dirs)):
        reach = (row <= col) if direction == 0 else (row >= col)
        gt = (g_ref[r] + bias_ref[...]).T
        brow = jnp.dot(jax.nn.log_sigmoid(gt), reach.astype(F32), precision=HI, preferred_element_type=F32)
        ib = gt - pltpu.roll(brow, L - ML_H, 0)
        ib_t = ib.T
        for h in range(ML_H):
            ch = dict(idx=(r * 2 + direction) * ML_H + h, reach=reach, last=L - 1 if direction == 0 else 0)
            gi, gf = 2 * ML_H * direction + h, 2 * ML_H * direction + ML_H + h
            ch.update(b_row=brow[gf:gf + 1, :], ib_row=ib[gi:gi + 1, :], ib_col=ib_t[:, gi:gi + 1])
            ch["k"] = x_ref[r, :, h * ML_DH:(h + 1) * ML_DH] * (ML_DH ** -0.5)
            ch["qt"] = qt_ref[r, h * ML_DH:(h + 1) * ML_DH, :]
            ch["vt"] = vt_ref[r, h * ML_DH:(h + 1) * ML_DH, :]
            ch["c"], ch["m"] = c_ref[ch["idx"]], m_ref[ch["idx"]]
            chains.append(ch)
    for ch in chains:
        ch["kq"] = jnp.dot(ch["k"], ch["qt"], preferred_element_type=F32)
        ch["cq"] = jnp.dot(ch["c"].astype(BF16), ch["qt"], preferred_element_type=F32)
    for ch in chains:
        b_row, m_st = ch["b_row"], ch["m"]
        d = jnp.where(ch["reach"], b_row + ch["ib_col"], NEG)
        m_inter = b_row + m_st
        m_t = jnp.maximum(m_inter, jnp.max(d, axis=0, keepdims=True))
        ch["w_inter"], ch["m_t"] = jnp.exp(m_inter - m_t), m_t
        ch["st"] = ch["kq"] * jnp.exp(d - m_t)
        b_last = b_row[:, ch["last"]:ch["last"] + 1]
        g_row = b_last + ch["ib_row"]
        m_new = jnp.maximum(b_last + m_st, jnp.max(g_row, axis=-1, keepdims=True))
        w_row = jnp.exp(g_row - m_new)
        ch["decay"], ch["m_new"] = jnp.exp(b_last + m_st - m_new), m_new
        ch["upd"] = jnp.concatenate([ch["vt"].astype(F32) * w_row, jnp.where(aug_row == 0, w_row, 0.0)], axis=0)
    for ch in chains:
        ch["num"] = jnp.dot(ch["vt"], ch["st"].astype(BF16), preferred_element_type=F32)
        ch["dc"] = jnp.dot(ch["upd"].astype(BF16), ch["k"], preferred_element_type=F32)
    outs = []
    for ch in chains:
        num = ch["num"] + ch["w_inter"] * ch["cq"][:ML_DH]
        den = jnp.sum(ch["st"], axis=0, keepdims=True) + ch["w_inter"] * ch["cq"][ML_DH:ML_DH + 1]
        outs.append(num / jnp.maximum(jnp.abs(den), jnp.exp(-ch["m_t"])))
        c_ref[ch["idx"]] = ch["decay"] * ch["c"] + ch["dc"]
        m_ref[ch["idx"]] = ch["m_new"]
    for n, (r, o_ref) in enumerate(itertools.product(range(ML_ROWS), (hf_ref, hb_ref))):
        o_ref[r] = jnp.concatenate(outs[n * ML_H:(n + 1) * ML_H], axis=0).T


def _mlstm(za, zg, bias, lc):
    b, s, _ = za.shape
    n_all, n_ctx = s // ML_CHUNK, lc // ML_CHUNK
    zat = jnp.swapaxes(za, 1, 2)
    ident = lambda j: j
    mirr = lambda j: _mirror(j, n_ctx, n_all)

    def specs(cf):
        rows = lambda w, col: pl.BlockSpec((ML_ROWS, ML_CHUNK, w), lambda bb, j: (bb, cf(j), col))
        feats = lambda blk: pl.BlockSpec((ML_ROWS, ML_W, ML_CHUNK), lambda bb, j: (bb, blk, cf(j)))
        return [rows(ML_W, 1), feats(0), feats(2), rows(LANE, 0)]

    out = lambda cf: pl.BlockSpec((ML_ROWS, ML_CHUNK, ML_W), lambda bb, j: (bb, cf(j), 0))
    assert b % ML_ROWS == 0
    return pl.pallas_call(
        _mlstm_kernel,
        grid=(b // ML_ROWS, n_all),
        in_specs=specs(ident) + specs(mirr) + [_whole((1, LANE))],
        out_specs=[out(ident), out(mirr)],
        out_shape=[jax.ShapeDtypeStruct((b, s, ML_W), F32)] * 2,
        scratch_shapes=[pltpu.VMEM((ML_ROWS * 2 * ML_H, ML_AUG, ML_DH), F32),
                        pltpu.VMEM((ML_ROWS * 2 * ML_H, 1, 1), F32)],
        compiler_params=_cp(("parallel", "arbitrary"), VMEM_LIMIT),
        name="mlstm_scan",
    )(za, zat, zat, zg, za, zat, zat, zg, bias)


LRU_T = 256
LRU_HALO = SUB
LRU_TAPS = 4


def _scan_rows(a, b, reverse):
    axis = a.ndim - 2
    n, sh = a.shape[axis], 1
    idx = lax.broadcasted_iota(jnp.int32, a.shape, axis)
    while sh < n:
        if reverse:
            ok, a_s, b_s = idx < n - sh, pltpu.roll(a, n - sh, axis), pltpu.roll(b, n - sh, axis)
        else:
            ok, a_s, b_s = idx >= sh, pltpu.roll(a, sh, axis), pltpu.roll(b, sh, axis)
        b = jnp.where(ok, a * b_s + b, b)
        a = jnp.where(ok, a * a_s, a)
        sh *= 2
    return a, b


def _lru_kernel(uf_ref, pf_ref, nf_ref, ub_ref, pb_ref, nb_ref, cw_ref, cb_ref, wa_ref, ba_ref, wx_ref, bx_ref,
                lam_ref, hf_ref, hb_ref, carry_ref, sa_ref, sb_ref, ext_ref, *, n_ctx, n_all):
    T = LRU_T
    groups = T // SUB
    j = pl.program_id(1)

    @pl.when(j == 0)
    def _():
        carry_ref[...] = jnp.zeros_like(carry_ref)

    cw = cw_ref[...]
    grow = lax.broadcasted_iota(jnp.int32, (groups, LRU_W), 0)
    for direction, (u_ref, p_ref, n_ref, o_ref) in enumerate(((uf_ref, pf_ref, nf_ref, hf_ref),
                                                              (ub_ref, pb_ref, nb_ref, hb_ref))):
        c = j if direction == 0 else _mirror(j, n_ctx, n_all)
        has_prev = ((c != 0) & (c != n_ctx)).astype(F32)
        has_next = ((c != n_ctx - 1) & (c != n_all - 1)).astype(F32)
        ext_ref[direction, 0:LRU_HALO] = p_ref[...] * has_prev
        ext_ref[direction, LRU_HALO:LRU_HALO + T] = u_ref[...]
        ext_ref[direction, LRU_HALO + T:] = n_ref[...] * has_next
        u = cb_ref[...]
        for tap in range(LRU_TAPS):
            u = u + cw[tap:tap + 1] * ext_ref[direction, pl.ds(LRU_HALO - LRU_TAPS // 2 + tap, T), :]
        ub = u.astype(BF16)
        r = jax.nn.sigmoid(jnp.dot(ub, wa_ref[direction], preferred_element_type=F32) + ba_ref[direction])
        i = jax.nn.sigmoid(jnp.dot(ub, wx_ref[direction], preferred_element_type=F32) + bx_ref[direction])
        lam = lam_ref[direction]
        log_a = (-LRU_C) * r * jnp.log1p(jnp.exp(-lam))
        a = jnp.exp(log_a)
        bb = jnp.sqrt(1.0 - a * a) * (i * u)
        rev = direction == 1
        a, bb = _scan_rows(a.reshape(groups, SUB, LRU_W), bb.reshape(groups, SUB, LRU_W), rev)
        a, bb = a.reshape(T, LRU_W), bb.reshape(T, LRU_W)
        for half in range(LRU_W // LANE):
            sa_ref[direction, half] = a[:, half * LANE:(half + 1) * LANE]
            sb_ref[direction, half] = bb[:, half * LANE:(half + 1) * LANE]
        edge = pl.ds(0 if rev else SUB - 1, groups, stride=SUB)
        ga = jnp.concatenate([sa_ref[direction, half, edge, :] for half in range(LRU_W // LANE)], axis=-1)
        gb = jnp.concatenate([sb_ref[direction, half, edge, :] for half in range(LRU_W // LANE)], axis=-1)
        ga, gb = _scan_rows(ga, gb, rev)
        if rev:
            inner, ga, gb = grow < groups - 1, pltpu.roll(ga, groups - 1, 0), pltpu.roll(gb, groups - 1, 0)
        else:
            inner, ga, gb = grow >= 1, pltpu.roll(ga, 1, 0), pltpu.roll(gb, 1, 0)
        carry = carry_ref[direction]
        h_in = jnp.where(inner, gb + ga * carry, carry)
        h_in = jnp.broadcast_to(h_in[:, None, :], (groups, SUB, LRU_W)).reshape(T, LRU_W)
        hcur = bb + a * h_in
        o_ref[...] = hcur
        last = T - 1 if direction == 0 else 0
        carry_ref[direction] = hcur[last:last + 1, :]


def _lru(zd, p, lc):
    b, s, _ = zd.shape
    n_all, n_ctx = s // LRU_T, lc // LRU_T
    per = LRU_T // LRU_HALO
    n_halo = s // LRU_HALO
    ident = lambda j: j
    mirr = lambda j: _mirror(j, n_ctx, n_all)

    def specs(cf):
        return [pl.BlockSpec((None, LRU_T, LRU_W), lambda bb, j: (bb, cf(j), 0)),
                pl.BlockSpec((None, LRU_HALO, LRU_W), lambda bb, j: (bb, jnp.maximum(cf(j) * per - 1, 0), 0)),
                pl.BlockSpec((None, LRU_HALO, LRU_W), lambda bb, j: (bb, jnp.minimum((cf(j) + 1) * per, n_halo - 1), 0))]

    vec = _whole((2, 1, LRU_W))
    mat = _whole((2, LRU_W, LRU_W))
    return pl.pallas_call(
        functools.partial(_lru_kernel, n_ctx=n_ctx, n_all=n_all),
        grid=(b, n_all),
        in_specs=specs(ident) + specs(mirr) + [_whole((4, LRU_W)), _whole((1, LRU_W)), mat, vec, mat, vec, vec],
        out_specs=[pl.BlockSpec((None, LRU_T, LRU_W), lambda bb, j: (bb, j, 0)),
                   pl.BlockSpec((None, LRU_T, LRU_W), lambda bb, j: (bb, mirr(j), 0))],
        out_shape=[jax.ShapeDtypeStruct((b, s, LRU_W), F32)] * 2,
        scratch_shapes=[pltpu.VMEM((2, 1, LRU_W), F32),
                        pltpu.VMEM((2, LRU_W // LANE, LRU_T, LANE), F32),
                        pltpu.VMEM((2, LRU_W // LANE, LRU_T, LANE), F32),
                        pltpu.VMEM((2, LRU_T + 2 * LRU_HALO, LRU_W), F32)],
        compiler_params=_cp(("parallel", "arbitrary"), VMEM_LIMIT),
        name="rglru_scan",
    )(zd, zd, zd, zd, zd, zd, p["conv_w"], p["conv_b"], p["wa"], p["ba"], p["wx"], p["bx"], p["lam"])


def _out_kernel(*refs, n_src, ctx_tiles, off):
    gate_ref, hfa_ref, hba_ref, o_ref, gn_ref, yb_ref, yc_ref, hfd_ref, hbd_ref, gd_ref, w_ref, out_ref = refs[n_src:]
    pair = [(0, ML_DH), (ML_DH, 2 * ML_DH)]
    gm_pair = _group_mean_matrix(LANE, pair)
    ha = hfa_ref[...] + hba_ref[...]
    ha = jnp.concatenate([_group_rms(ha[:, :LANE], pair, gm_pair), _group_rms(ha[:, LANE:], pair, gm_pair)],
                         axis=-1) * gn_ref[...]
    ya = (jax.nn.sigmoid(o_ref[...].astype(F32)) * ha).astype(BF16)
    yd = (jax.nn.gelu(gd_ref[...]) * (hfd_ref[...] + hbd_ref[...])).astype(BF16)
    acc = jnp.dot(ya, w_ref[0:ML_W, :], preferred_element_type=F32)
    acc = acc + jnp.dot(yb_ref[...], w_ref[ML_W:2 * ML_W, :], preferred_element_type=F32)
    acc = acc + jnp.dot(yc_ref[...], w_ref[2 * ML_W:3 * ML_W, :], preferred_element_type=F32)
    acc = acc + jnp.dot(yd, w_ref[3 * ML_W:4 * ML_W, :], preferred_element_type=F32)
    out_ref[...] = _src_rows(refs[:n_src], ctx_tiles, off) + gate_ref[...] * acc


def _out_proj(src, mod, hfa, hba, za, gn, yb, yc, hfd, hbd, zd, w, nb, ctx_tiles, off):
    b, s = src[0].shape[0], sum(a.shape[1] for a in src)
    nt = s // TM - off
    r = lambda w_, col=0: _rows(w_, off, col)
    return pl.pallas_call(
        functools.partial(_out_kernel, n_src=len(src), ctx_tiles=ctx_tiles, off=off),
        grid=(b, nt),
        in_specs=_src_specs(src, ctx_tiles, off) + [
                  _mod_spec(2, nb, ctx_tiles, off), r(ML_W), r(ML_W), r(ML_W, 3), _whole((1, ML_W)),
                  r(ML_W), r(ML_W), r(ML_W), r(ML_W), r(ML_W, 1), _whole((D, D))],
        out_specs=_rows(D),
        out_shape=jax.ShapeDtypeStruct((b, nt * TM, D), F32),
        compiler_params=_cp(("parallel", "parallel"), VMEM_LIMIT),
        name="out_proj",
    )(*src, mod, hfa, hba, za, gn, yb, yc, hfd, hbd, zd, w)


def _ffn_kernel(x_ref, g_ref, sh_ref, sc_ref, gate_ref, w13_ref, w2_ref, out_ref, *, d_ff):
    x = x_ref[...]
    h = _rms(x) * g_ref[...]
    hb = (h * (1.0 + sc_ref[...]) + sh_ref[...]).astype(BF16)
    a = jnp.dot(hb, w13_ref[:, :d_ff], preferred_element_type=F32)
    g = jnp.dot(hb, w13_ref[:, d_ff:], preferred_element_type=F32)
    u = (g * jax.nn.sigmoid(g) * a).astype(BF16)
    out_ref[...] = x + gate_ref[...] * jnp.dot(u, w2_ref[...], preferred_element_type=F32)


def _ffn(xa, gain, mod, w13, w2, nb, ctx_tiles):
    b, s, _ = xa.shape
    d_ff = w2.shape[0]
    return pl.pallas_call(
        functools.partial(_ffn_kernel, d_ff=d_ff),
        grid=(b, s // TM),
        in_specs=[_rows(D), _whole((1, D)), _mod_spec(3, nb, ctx_tiles, 0), _mod_spec(4, nb, ctx_tiles, 0),
                  _mod_spec(5, nb, ctx_tiles, 0), _whole((D, 2 * d_ff)), _whole((d_ff, D))],
        out_specs=_rows(D),
        out_shape=jax.ShapeDtypeStruct((b, s, D), F32),
        compiler_params=_cp(("parallel", "parallel"), VMEM_LIMIT),
        name="ffn",
    )(xa, gain.reshape(1, D), mod, mod, mod, w13, w2)


META_E, META_RANK, META_GATE = 0, 2, 4


def _router_kernel(x_ref, g_ref, sh_ref, sc_ref, wr_ref, br_ref, h_ref, meta_ref, cnt_ref):
    @pl.when((pl.program_id(0) == 0) & (pl.program_id(1) == 0))
    def _():
        cnt_ref[...] = jnp.zeros_like(cnt_ref)

    h = _rms(x_ref[...]) * g_ref[...]
    h = h * (1.0 + sc_ref[...]) + sh_ref[...]
    h_ref[...] = h
    logits = jnp.dot(h, wr_ref[...], precision=HI, preferred_element_type=F32) + br_ref[...]
    lane = _lane(logits.shape)
    logits = jnp.where(lane < N_EXP, logits, NEG)
    m1 = jnp.max(logits, axis=-1, keepdims=True)
    i1 = jnp.min(jnp.where(logits == m1, lane, LANE), axis=-1, keepdims=True)
    rest = jnp.where(lane == i1, NEG, logits)
    m2 = jnp.max(rest, axis=-1, keepdims=True)
    i2 = jnp.min(jnp.where(rest == m2, lane, LANE), axis=-1, keepdims=True)
    e2 = jnp.exp(m2 - m1)
    inv = 1.0 / (1.0 + e2)
    hit1, hit2 = lane == i1, lane == i2
    assign = (hit1 | hit2).astype(F32)
    rr = lax.broadcasted_iota(jnp.int32, (TM, TM), 0)
    cc = lax.broadcasted_iota(jnp.int32, (TM, TM), 1)
    before = jnp.dot((cc < rr).astype(BF16), assign.astype(BF16), preferred_element_type=F32) + cnt_ref[...]
    r1 = jnp.sum(jnp.where(hit1, before, 0.0), axis=-1, keepdims=True)
    r2 = jnp.sum(jnp.where(hit2, before, 0.0), axis=-1, keepdims=True)
    cnt_ref[...] = cnt_ref[...] + jnp.sum(assign, axis=0, keepdims=True)
    fields = (i1.astype(F32), i2.astype(F32), r1, r2, inv, e2 * inv)
    meta = jnp.zeros(logits.shape, F32)
    for k, val in enumerate(fields):
        meta = jnp.where(lane == k, val, meta)
    meta_ref[...] = meta


def _router(xl, gain, mod, wr, br):
    b, t, _ = xl.shape
    spec = lambda k: pl.BlockSpec((None, 1, D), lambda bb, j: (bb, 0, k))
    return pl.pallas_call(
        _router_kernel,
        grid=(b, t // TM),
        in_specs=[_rows(D), _whole((1, D)), spec(3), spec(4), _whole((D, LANE)), _whole((1, LANE))],
        out_specs=[_rows(D), _rows(LANE), _whole((1, LANE))],
        out_shape=[jax.ShapeDtypeStruct((b, t, D), F32), jax.ShapeDtypeStruct((b, t, LANE), F32),
                   jax.ShapeDtypeStruct((1, LANE), F32)],
        compiler_params=_cp(("arbitrary", "arbitrary"), VMEM_LIMIT),
        name="router",
    )(xl, gain.reshape(1, D), mod, mod, wr, br)


TG = 256
TOP_K = 2


ROW_DMA_UNROLL = 8


def _issue_row_dmas(route_ref, make_copy):
    def issue(r, c):
        for k in range(TOP_K):
            make_copy(r, k, route_ref[0, k * TM + r]).start()
        return c

    lax.fori_loop(0, TM, issue, 0, unroll=ROW_DMA_UNROLL)


def _dispatch_kernel(route_ref, fill_ref, h_ref, xs_ref, zero_ref, sem, fill_sem):
    @pl.when(pl.program_id(0) == 0)
    def _():
        zero_ref[...] = jnp.zeros_like(zero_ref)
        zero_row = lambda pos: pltpu.make_async_copy(zero_ref.at[pl.ds(0, 1)], xs_ref.at[pl.ds(pos, 1)], fill_sem)
        for e in range(N_EXP + 1):
            first, count = fill_ref[e], fill_ref[N_EXP + 1 + e]
            lax.fori_loop(0, count, lambda i, c: (zero_row(first + i).start(), c)[1], 0)
            lax.fori_loop(0, count, lambda i, c: (zero_row(first).wait(), c)[1], 0)

    _issue_row_dmas(route_ref, lambda r, k, pos: pltpu.make_async_copy(h_ref.at[pl.ds(r, 1)],
                                                                         xs_ref.at[pl.ds(pos, 1)], sem))
    for _ in range(TOP_K):
        pltpu.make_async_copy(h_ref, xs_ref.at[pl.ds(0, TM)], sem).wait()


def _dispatch(h2, route, fill, n_rows):
    n = h2.shape[0]
    return pl.pallas_call(
        _dispatch_kernel,
        grid=(n // TM,),
        in_specs=[pl.BlockSpec((None, 1, TOP_K * TM), lambda i: (i, 0, 0), memory_space=pltpu.SMEM),
                  pl.BlockSpec(memory_space=pltpu.SMEM),
                  pl.BlockSpec((TM, D), lambda i: (i, 0))],
        out_specs=pl.BlockSpec(memory_space=pl.ANY),
        out_shape=jax.ShapeDtypeStruct((n_rows, D), F32),
        scratch_shapes=[pltpu.VMEM((SUB, D), F32), pltpu.SemaphoreType.DMA(()), pltpu.SemaphoreType.DMA(())],
        compiler_params=_cp(("arbitrary",), VMEM_LIMIT),
        name="moe_dispatch",
    )(route, fill, h2)


def _expert_kernel(te_ref, nu_ref, xs_ref, w13_ref, w2_ref, y_ref, *, d_ff):
    del te_ref
    t = pl.program_id(0)

    @pl.when(t < nu_ref[0])
    def _():
        xb = xs_ref[...].astype(BF16)
        a = jnp.dot(xb, w13_ref[:, :d_ff], preferred_element_type=F32)
        g = jnp.dot(xb, w13_ref[:, d_ff:], preferred_element_type=F32)
        u = (g * jax.nn.sigmoid(g) * a).astype(BF16)
        y_ref[...] = jnp.dot(u, w2_ref[...], preferred_element_type=F32)

    @pl.when(t >= nu_ref[0])
    def _():
        y_ref[...] = jnp.zeros_like(y_ref)


def _experts(xs, tile_expert, n_used, w13, w2):
    n_rows = xs.shape[0]
    d_ff = w2.shape[1]
    once = dict(pipeline_mode=pl.Buffered(1))
    grid_spec = pltpu.PrefetchScalarGridSpec(
        num_scalar_prefetch=2,
        grid=(n_rows // TG,),
        in_specs=[pl.BlockSpec((TG, D), lambda t, te, nu: (jnp.minimum(t, nu[0] - 1), 0)),
                  pl.BlockSpec((None, D, 2 * d_ff), lambda t, te, nu: (te[t], 0, 0), **once),
                  pl.BlockSpec((None, d_ff, D), lambda t, te, nu: (te[t], 0, 0), **once)],
        out_specs=pl.BlockSpec((TG, D), lambda t, te, nu: (t, 0)),
    )
    return pl.pallas_call(
        functools.partial(_expert_kernel, d_ff=d_ff),
        grid_spec=grid_spec,
        out_shape=jax.ShapeDtypeStruct((n_rows, D), F32),
        compiler_params=_cp(("arbitrary",), VMEM_LIMIT),
        name="moe_experts",
    )(tile_expert, n_used, xs, w13, w2)


def _combine_kernel(route_ref, next_ref, x_ref, meta_ref, gate_ref, y_ref, out_ref, buf_ref, sem, *, n_steps):
    step = pl.program_id(0) * pl.num_programs(1) + pl.program_id(1)
    slot = step & 1

    def gather(rows_ref, s):
        _issue_row_dmas(rows_ref, lambda r, k, pos: pltpu.make_async_copy(
            y_ref.at[pl.ds(pos, 1)], buf_ref.at[s, k, pl.ds(r, 1)], sem.at[s]))

    @pl.when(step == 0)
    def _():
        gather(route_ref, 0)

    @pl.when(step + 1 < n_steps)
    def _():
        gather(next_ref, 1 - slot)

    for k in range(TOP_K):
        pltpu.make_async_copy(y_ref.at[pl.ds(0, TM)], buf_ref.at[slot, k], sem.at[slot]).wait()
    meta = meta_ref[...]
    mix = (meta[:, META_GATE:META_GATE + 1] * buf_ref[slot, 0]
           + meta[:, META_GATE + 1:META_GATE + 2] * buf_ref[slot, 1])
    out_ref[...] = x_ref[...] + gate_ref[...] * mix


def _combine(xl, meta, mod, y, route):
    b, t, _ = xl.shape
    tiles = t // TM
    n_steps = b * tiles
    rows_of = lambda ahead: pl.BlockSpec((None, 1, TOP_K * TM),
                                         lambda bb, j: (jnp.minimum(bb * tiles + j + ahead, n_steps - 1), 0, 0),
                                         memory_space=pltpu.SMEM)
    return pl.pallas_call(
        functools.partial(_combine_kernel, n_steps=n_steps),
        grid=(b, tiles),
        in_specs=[rows_of(0), rows_of(1), _rows(D), _rows(LANE), pl.BlockSpec((None, 1, D), lambda bb, j: (bb, 0, 5)),
                  pl.BlockSpec(memory_space=pl.ANY)],
        out_specs=_rows(D),
        out_shape=jax.ShapeDtypeStruct((b, t, D), F32),
        scratch_shapes=[pltpu.VMEM((2, TOP_K, TM, D), F32), pltpu.SemaphoreType.DMA((2,))],
        compiler_params=_cp(("arbitrary", "arbitrary"), VMEM_LIMIT),
        name="moe_combine",
    )(route, route, xl, meta, mod, y)


def _moe(xl, gain, mod, wr, br, w13, w2):
    b, t, _ = xl.shape
    n = b * t
    h, meta, cnt = _router(xl, gain, mod, wr, br)
    counts = cnt[0, :N_EXP].astype(jnp.int32)
    padded = (counts + TG - 1) // TG * TG
    ends = jnp.cumsum(padded)
    start = ends - padded
    n_tiles = TOP_K * n // TG + N_EXP
    tile_lo = jnp.arange(n_tiles, dtype=jnp.int32) * TG
    tile_expert = jnp.minimum(jnp.sum((tile_lo[:, None] >= ends[None, :]).astype(jnp.int32), axis=1), N_EXP - 1)
    n_used = (ends[-1:] // TG).astype(jnp.int32)
    ids = meta.reshape(n // TM, TM, LANE)[:, :, :2 * TOP_K].astype(jnp.int32)
    expert, rank = ids[:, :, META_E:META_E + TOP_K], ids[:, :, META_RANK:META_RANK + TOP_K]
    pos = rank + sum(jnp.where(expert == e, start[e], 0) for e in range(N_EXP))
    route = jnp.swapaxes(pos, 1, 2).reshape(n // TM, 1, TOP_K * TM)
    n_rows = n_tiles * TG
    fill = jnp.concatenate([start + counts, ends[-1:], padded - counts, n_rows - ends[-1:]])
    xs = _dispatch(h.reshape(n, D), route, fill, n_rows)
    y = _experts(xs, tile_expert, n_used, w13, w2)
    return _combine(xl, meta, mod, y, route)


def _pad_cols(w, n):
    return jnp.pad(w, ((0, 0), (0, n - w.shape[1])))


def _layout_w_in(w):
    ml = 4 * ML_W
    a, gates = w[:, :ml], w[:, ml:ml + 4 * ML_H]
    o = ml + 4 * ML_H
    cq, ckv, kr = w[:, o:o + MLA_QR], w[:, o + MLA_QR:o + MLA_QR + MLA_KVR], w[:, o + MLA_QR + MLA_KVR:o + MLA_QR + MLA_KVR + MLA_ROPE]
    o += MLA_QR + MLA_KVR + MLA_ROPE
    sw = w[:, o:o + (SW_H + 2 * SW_KV) * SW_DH]
    o += (SW_H + 2 * SW_KV) * SW_DH
    lru = w[:, o:]
    slots = ((a, ml), (gates, LANE), (cq, 2 * LANE), (ckv, MLA_KVR), (None, MLA_NOPE), (kr, LANE - MLA_NOPE),
             (sw, sw.shape[1]), (lru, lru.shape[1]))
    out = jnp.zeros((w.shape[0], sum(width for _, width in slots)), BF16)
    col = 0
    for piece, width in slots:
        if piece is not None:
            out = out.at[:, col:col + piece.shape[1]].set(piece.astype(BF16))
        col += width
    return out


IN_SEGS = ((0, 4 * ML_W), (4 * ML_W, LANE), (4 * ML_W + LANE, 4 * LANE), (4 * ML_W + 5 * LANE, 4 * LANE),
           (4 * ML_W + 9 * LANE, 4 * LANE))
IN_DTYPES = (BF16, F32, F32, F32, F32)


def _row(v, n=None):
    v = v.reshape(1, -1)
    return v if n is None else _pad_cols(v, n)


def _layer_params(l, ml_gate_b, ml_out_norm, mla_q_norm, mla_w_uq, mla_kv_norm, mla_w_ukv, mla_q_gain, mla_k_gain,
                  sw_q_gain, sw_k_gain, lru_conv_w, lru_conv_b, lru_wa, lru_ba, lru_wx, lru_bx, lru_lam):
    hq = MLA_NOPE + MLA_ROPE
    wq = mla_w_uq[l].reshape(MLA_QR, MLA_H, hq)
    wq = jnp.pad(wq, ((0, 2 * LANE - MLA_QR), (0, 0), (0, LANE - hq))).reshape(2 * LANE, MLA_H * LANE)
    wkv = mla_w_ukv[l].reshape(MLA_KVR, MLA_H, MLA_NOPE + MLA_V)
    wk = jnp.pad(wkv[:, :, :MLA_NOPE], ((0, 0), (0, 0), (0, LANE - MLA_NOPE))).reshape(MLA_KVR, MLA_H * LANE)
    wv = wkv[:, :, MLA_NOPE:].reshape(MLA_KVR, MLA_H * MLA_V)

    def blockdiag(wb):
        bw = LRU_W // LRU_BLOCKS
        eye = jnp.eye(LRU_BLOCKS, dtype=wb.dtype)
        return jnp.einsum('xncd,nm->xncmd', wb, eye).reshape(2, LRU_W, LRU_W)

    return dict(
        gate_b=_row(ml_gate_b[l], LANE), out_norm=_row(ml_out_norm[l]),
        q_norm=_row(mla_q_norm[l], 2 * LANE), w_uq=wq.astype(BF16), kv_norm=_row(mla_kv_norm[l]),
        w_uk=wk.astype(BF16), w_uv=wv.astype(BF16),
        q_gain=_row(mla_q_gain[l], LANE), k_gain=_row(mla_k_gain[l], LANE),
        sw_q_gain=_row(jnp.tile(sw_q_gain[l], 2)), sw_k_gain=_row(jnp.tile(sw_k_gain[l], 2)),
        conv_w=lru_conv_w[l], conv_b=_row(lru_conv_b[l]),
        wa=blockdiag(lru_wa[l]).astype(BF16), ba=lru_ba[l].reshape(2, 1, LRU_W),
        wx=blockdiag(lru_wx[l]).astype(BF16), bx=lru_bx[l].reshape(2, 1, LRU_W),
        lam=lru_lam[l].reshape(2, 1, LRU_W),
    )


def _mix_layer(src, mod, gain, w_in_l, w_out_l, sink, p, tabs, lc, nb, ctx_out):
    ctx_tiles = lc // TM
    za, zg, zb, zc, zd = _proj(src, gain, mod, w_in_l, IN_SEGS, IN_DTYPES, nb, ctx_tiles)
    hfa, hba = _mlstm(za, zg, p["gate_b"], lc)
    qm, km, vm, qs, ks, vs = _prep(zb, zc, tabs[0], tabs[1], p)
    yb = _mla_attn(qm, km, vm, lc)
    yc = _swa_attn(qs, ks, vs, sink, lc)
    hfd, hbd = _lru(zd, p, lc)
    off = 0 if ctx_out else ctx_tiles
    return _out_proj(src, mod, hfa, hba, za, p["out_norm"], yb, yc, hfd, hbd, zd, w_out_l, nb, ctx_tiles, off)


def kernel(x, c, ctx, c_ctx, ada_w, ada_b, norm_mix, norm_ffn, w_in, w_out, ml_gate_b, ml_out_norm, mla_q_norm, mla_w_uq, mla_kv_norm, mla_w_ukv, mla_q_gain, mla_k_gain, sw_q_gain, sw_k_gain, sw_sink, lru_conv_w, lru_conv_b, lru_wa, lru_ba, lru_wx, lru_bx, lru_lam, ffn_w13, ffn_w2, moe_router, moe_router_b, moe_w13, moe_w2):
    nb, t_len, _ = x.shape
    lc = ctx.shape[1]
    depth = ada_w.shape[0]
    assert nb < SUB and lc % TM == 0 and t_len % TM == 0 and t_len % GRID_W == 0
    cc = jnp.zeros((SUB, D), F32).at[:nb].set(c).at[nb].set(c_ctx)
    mods = _ada_mod(cc, ada_w, ada_b)
    tabs = (_rope_tables(t_len, lc, LANE, MLA_ROPE, MLA_NOPE), _rope_tables(t_len, lc, SW_DH, SW_DH, 0))
    src = (ctx, x)
    for l in range(depth):
        ctx_out = l < depth - 1
        mod = mods[l].reshape(SUB, 1, 6 * D)
        p = _layer_params(l, ml_gate_b, ml_out_norm, mla_q_norm, mla_w_uq, mla_kv_norm, mla_w_ukv, mla_q_gain,
                          mla_k_gain, sw_q_gain, sw_k_gain, lru_conv_w, lru_conv_b, lru_wa, lru_ba, lru_wx, lru_bx,
                          lru_lam)
        xa = _mix_layer(src, mod, norm_mix[l], _layout_w_in(w_in[l]), w_out[l].astype(BF16), sw_sink[l], p, tabs,
                        lc, nb, ctx_out)
        if l % 2 == 0:
            xa = _ffn(xa, norm_ffn[l], mod, ffn_w13[l // 2].astype(BF16), ffn_w2[l // 2].astype(BF16), nb,
                      lc // TM if ctx_out else 0)
        else:
            xa = _moe(xa, norm_ffn[l], mod, _pad_cols(moe_router[l // 2], LANE), _row(moe_router_b[l // 2], LANE),
                      moe_w13[l // 2].astype(BF16), moe_w2[l // 2].astype(BF16))
        src = (xa,)
    return xa
```

```python
import functools
import itertools

import numpy as np
import jax
import jax.numpy as jnp
from jax import lax
from jax.experimental import pallas as pl
from jax.experimental.pallas import tpu as pltpu

F32 = jnp.float32
BF16 = jnp.bfloat16
HI = lax.Precision.HIGHEST

D = 1024
GRID_W = 64
EPS = 1e-6
ROPE_BASE = 10000.0
ML_H, ML_DH, ML_W, ML_CHUNK = 4, 64, 256, 128
MLA_H, MLA_QR, MLA_KVR, MLA_NOPE, MLA_ROPE, MLA_V = 4, 192, 128, 64, 32, 64
SW_H, SW_KV, SW_DH, SW_WINDOW, SW_BLOCK = 4, 2, 64, 128, 128
LRU_W, LRU_BLOCKS, LRU_C = 256, 4, 8.0
N_EXP = 8

LANE = 128
SUB = 8
TM = 256
NEG = -1e30
VMEM_LIMIT = 56 * 1024 * 1024


def _cp(sem, vmem=None):
    return pltpu.CompilerParams(dimension_semantics=sem, vmem_limit_bytes=vmem)


def _mod_spec(chunk, nb, ctx_tiles, off):
    return pl.BlockSpec((None, 1, D), lambda b, j: (jnp.where(j + off < ctx_tiles, nb, b), 0, chunk))


def _rows(width, off=0, col=0, tm=TM):
    return pl.BlockSpec((None, tm, width), lambda b, j: (b, j + off, col))


def _whole(shape):
    nd = len(shape)
    return pl.BlockSpec(shape, lambda *_: (0,) * nd)


def _src_specs(src, ctx_tiles, off):
    if len(src) == 1:
        return [_rows(D, off)]
    return [pl.BlockSpec((None, TM, D), lambda b, j: (b, jnp.minimum(j + off, ctx_tiles - 1), 0)),
            pl.BlockSpec((None, TM, D), lambda b, j: (b, jnp.maximum(j + off - ctx_tiles, 0), 0))]


def _src_rows(src_refs, ctx_tiles, off):
    if len(src_refs) == 1:
        return src_refs[0][...]
    return jnp.where(pl.program_id(1) + off < ctx_tiles, src_refs[0][...], src_refs[1][...])


def _rms(x):
    return x * lax.rsqrt(jnp.mean(x * x, axis=-1, keepdims=True) + EPS)


def _lane(shape):
    return lax.broadcasted_iota(jnp.int32, shape, len(shape) - 1)


def _group_mean_matrix(n, bounds):
    i = lax.broadcasted_iota(jnp.int32, (n, n), 0)
    j = lax.broadcasted_iota(jnp.int32, (n, n), 1)
    g = jnp.zeros((n, n), F32)
    for lo, hi in bounds:
        g = jnp.where((i >= lo) & (i < hi) & (j >= lo) & (j < hi), 1.0 / (hi - lo), g)
    return g.astype(BF16)


def _group_rms(blk, bounds, gm):
    lane = _lane(blk.shape)
    sq = blk * blk
    hi_part = sq.astype(BF16)
    lo_part = (sq - hi_part.astype(F32)).astype(BF16)
    ms = (jnp.dot(hi_part, gm, preferred_element_type=F32) + jnp.dot(lo_part, gm, preferred_element_type=F32))
    inside = functools.reduce(jnp.logical_or, [(lane >= lo) & (lane < hi) for lo, hi in bounds])
    return blk * jnp.where(inside, lax.rsqrt(ms + EPS), 0.0)


def _rope(blk, tab_ref, shift):
    n = blk.shape[-1]
    return (blk * tab_ref[0] + pltpu.roll(blk, n - shift, 1) * tab_ref[1]
            + pltpu.roll(blk, shift, 1) * tab_ref[2])


def _rope_tables(t_len, lc, width, dims, offset):
    half, nf = dims // 2, dims // 4
    p = np.arange(dims)
    i = p % half
    f = i % nf
    first = i < nf
    freq = ROPE_BASE ** (-f.astype(np.float64) / nf)
    t = np.arange(t_len)
    pos = np.where(p[None, :] < half, (t // GRID_W)[:, None], (t % GRID_W)[:, None]).astype(np.float64)
    ang = pos * freq[None, :]
    cos, sin = np.cos(ang), np.sin(ang)
    tab = np.zeros((3, lc + t_len, width), np.float64)
    tab[0] = 1.0
    tab[0, lc:, offset:offset + dims] = cos
    tab[1, lc:, offset:offset + dims] = np.where(first[None, :], -sin, 0.0)
    tab[2, lc:, offset:offset + dims] = np.where(first[None, :], 0.0, sin)
    reps = LANE // width
    return jnp.asarray(np.tile(tab, (1, 1, reps)), F32)


def _ada_kernel(c_ref, w_ref, b_ref, o_ref):
    c = c_ref[...]
    s = c * jax.nn.sigmoid(c)
    o_ref[...] = jnp.dot(s, w_ref[...], precision=HI, preferred_element_type=F32) + b_ref[...]


def _ada_mod(cc, ada_w, ada_b):
    depth = ada_w.shape[0]
    n = ada_w.shape[2]
    tn = 1024
    return pl.pallas_call(
        _ada_kernel,
        grid=(depth, n // tn),
        in_specs=[
            pl.BlockSpec((SUB, D), lambda l, j: (0, 0)),
            pl.BlockSpec((None, D, tn), lambda l, j: (l, 0, j)),
            pl.BlockSpec((None, 1, tn), lambda l, j: (l, 0, j)),
        ],
        out_specs=pl.BlockSpec((None, SUB, tn), lambda l, j: (l, 0, j)),
        out_shape=jax.ShapeDtypeStruct((depth, SUB, n), F32),
        compiler_params=_cp(("parallel", "parallel")),
        name="ada_mod",
    )(cc, ada_w, ada_b.reshape(depth, 1, n))


def _proj_kernel(*refs, segs, n_src, ctx_tiles):
    (g_ref, sh_ref, sc_ref, w_ref), out_refs = refs[n_src:n_src + 4], refs[n_src + 4:]
    h = _rms(_src_rows(refs[:n_src], ctx_tiles, 0)) * g_ref[...]
    h = h * (1.0 + sc_ref[...]) + sh_ref[...]
    hb = h.astype(BF16)
    for (start, width), o_ref in zip(segs, out_refs):
        o_ref[...] = jnp.dot(hb, w_ref[:, start:start + width], preferred_element_type=F32).astype(o_ref.dtype)


def _proj(src, gain, mod, w, segs, dtypes, nb, ctx_tiles):
    b, s = src[0].shape[0], sum(a.shape[1] for a in src)
    n = w.shape[1]
    return pl.pallas_call(
        functools.partial(_proj_kernel, segs=segs, n_src=len(src), ctx_tiles=ctx_tiles),
        grid=(b, s // TM),
        in_specs=_src_specs(src, ctx_tiles, 0) + [
            _whole((1, D)),
            _mod_spec(0, nb, ctx_tiles, 0),
            _mod_spec(1, nb, ctx_tiles, 0),
            _whole((D, n)),
        ],
        out_specs=[_rows(wd) for _, wd in segs],
        out_shape=[jax.ShapeDtypeStruct((b, s, wd), dt) for (_, wd), dt in zip(segs, dtypes)],
        compiler_params=_cp(("parallel", "parallel"), VMEM_LIMIT),
        name="in_proj",
    )(*src, gain.reshape(1, D), mod, mod, w)


def _prep_kernel(zb_ref, zc_ref, tm_ref, ts_ref, qn_ref, wuq_ref, kvn_ref, wk_ref, wv_ref, qg_ref, kg_ref,
                 sqg_ref, skg_ref, qm_ref, km_ref, vm_ref, qs_ref, ks_ref, vs_ref):
    zb = zb_ref[...]
    cq = zb[:, :2 * LANE]
    cqn = cq * lax.rsqrt(jnp.sum(cq * cq, axis=-1, keepdims=True) * (1.0 / MLA_QR) + EPS) * qn_ref[...]
    qf = jnp.dot(cqn.astype(BF16), wuq_ref[...], preferred_element_type=F32)
    q_scale = (MLA_NOPE + MLA_ROPE) ** -0.5 * LOG2E
    head_bounds = [(0, MLA_NOPE), (MLA_NOPE, MLA_NOPE + MLA_ROPE)]
    gm_head = _group_mean_matrix(LANE, head_bounds)
    for h in range(MLA_H):
        blk = _group_rms(qf[:, h * LANE:(h + 1) * LANE], head_bounds, gm_head) * qg_ref[...]
        qm_ref[h] = (_rope(blk, tm_ref, MLA_ROPE // 4) * q_scale).astype(BF16)
    ckvn = _rms(zb[:, 2 * LANE:3 * LANE]) * kvn_ref[...]
    ckvb = ckvn.astype(BF16)
    kf = jnp.dot(ckvb, wk_ref[...], preferred_element_type=F32)
    vf = jnp.dot(ckvb, wv_ref[...], preferred_element_type=F32)
    kg = kg_ref[...]
    kr = _group_rms(zb[:, 3 * LANE:4 * LANE], head_bounds[1:], gm_head) * kg
    kr = _rope(kr, tm_ref, MLA_ROPE // 4)
    for h in range(MLA_H):
        kn = _group_rms(kf[:, h * LANE:(h + 1) * LANE], head_bounds[:1], gm_head) * kg
        km_ref[h] = (kn + kr).astype(BF16)
        ones = (_lane((vf.shape[0], MLA_VA - MLA_V)) == 0).astype(F32)
        vm_ref[h] = jnp.concatenate([vf[:, h * MLA_V:(h + 1) * MLA_V], ones], axis=-1).astype(BF16)
    zc = zc_ref[...]
    pair = [(0, SW_DH), (SW_DH, 2 * SW_DH)]
    gm_pair = _group_mean_matrix(LANE, pair)
    sw_scale = SW_DH ** -0.5
    for half in range(2):
        blk = _group_rms(zc[:, half * LANE:(half + 1) * LANE], pair, gm_pair) * sqg_ref[...]
        qs_ref[:, half * LANE:(half + 1) * LANE] = (_rope(blk, ts_ref, SW_DH // 4) * sw_scale).astype(BF16)
    kb = _rope(_group_rms(zc[:, 2 * LANE:3 * LANE], pair, gm_pair) * skg_ref[...], ts_ref, SW_DH // 4).astype(BF16)
    vb = zc[:, 3 * LANE:4 * LANE].astype(BF16)
    for j in range(SW_KV):
        ks_ref[j] = kb[:, j * SW_DH:(j + 1) * SW_DH]
        vs_ref[j] = vb[:, j * SW_DH:(j + 1) * SW_DH]


def _prep(zb, zc, tab_mla, tab_sw, p):
    b, s, _ = zb.shape
    tab = pl.BlockSpec((3, TM, LANE), lambda bb, j: (0, j, 0))
    heads = lambda nh, w: pl.BlockSpec((None, nh, TM, w), lambda bb, j: (bb, 0, j, 0))
    return pl.pallas_call(
        _prep_kernel,
        grid=(b, s // TM),
        in_specs=[_rows(4 * LANE), _rows(4 * LANE), tab, tab,
                  _whole((1, 2 * LANE)), _whole((2 * LANE, 4 * LANE)), _whole((1, LANE)),
                  _whole((LANE, 4 * LANE)), _whole((LANE, 2 * LANE)), _whole((1, LANE)), _whole((1, LANE)),
                  _whole((1, LANE)), _whole((1, LANE))],
        out_specs=[heads(MLA_H, LANE), heads(MLA_H, LANE), heads(MLA_H, MLA_VA),
                   _rows(2 * LANE), heads(SW_KV, SW_DH), heads(SW_KV, SW_DH)],
        out_shape=[jax.ShapeDtypeStruct((b, MLA_H, s, LANE), BF16),
                   jax.ShapeDtypeStruct((b, MLA_H, s, LANE), BF16),
                   jax.ShapeDtypeStruct((b, MLA_H, s, MLA_VA), BF16),
                   jax.ShapeDtypeStruct((b, s, 2 * LANE), BF16),
                   jax.ShapeDtypeStruct((b, SW_KV, s, SW_DH), BF16),
                   jax.ShapeDtypeStruct((b, SW_KV, s, SW_DH), BF16)],
        compiler_params=_cp(("parallel", "parallel"), VMEM_LIMIT),
        name="head_prep",
    )(zb, zc, tab_mla, tab_sw, p["q_norm"], p["w_uq"], p["kv_norm"], p["w_uk"], p["w_uv"], p["q_gain"],
      p["k_gain"], p["sw_q_gain"], p["sw_k_gain"])


MLA_TQ = 256
MLA_KC = 256
LOG2E = 1.4426950408889634
MLA_VA = MLA_V + 16
MLA_UNROLL = 2


def _mla_kernel(qt_ref, k_ref, vt_ref, o_ref, m_ref, acc_ref, st_ref, cm_ref, p_ref, al_ref, *, n_all, n_ctx,
                ctx_tiles):
    i = pl.program_id(1)
    n_kv = jnp.where(i < ctx_tiles, n_ctx, n_all)
    last = n_kv - 1

    def scores(c, slot):
        off = pl.multiple_of(jnp.minimum(c, last) * MLA_KC, MLA_KC)
        for h in range(MLA_H):
            st = jnp.dot(k_ref[h, pl.ds(off, MLA_KC), :], qt_ref[h], preferred_element_type=F32)
            st_ref[slot, h] = st
            cm_ref[slot, h] = jnp.max(st, axis=0, keepdims=True)

    def softmax(slot, first):
        for h in range(MLA_H):
            cm = cm_ref[slot, h]
            m_new = cm if first else jnp.maximum(m_ref[h], cm)
            p_ref[slot, h] = jnp.exp2(st_ref[slot, h] - m_new).astype(BF16)
            al_ref[slot, h] = jnp.ones_like(cm) if first else jnp.exp2(m_ref[h] - m_new)
            m_ref[h] = m_new

    def weighted_values(c, slot):
        pv = [jnp.dot(vt_ref[h, c], p_ref[slot, h], preferred_element_type=F32) for h in range(MLA_H)]
        return pv

    def accumulate(pv, slot):
        for h in range(MLA_H):
            acc_ref[h] = al_ref[slot, h] * acc_ref[h] + pv[h]

    acc_ref[...] = jnp.zeros_like(acc_ref)
    scores(0, 0)
    softmax(0, True)
    scores(1, 1)

    def body(j, carry):
        for u in range(MLA_UNROLL):
            c, slot = MLA_UNROLL * j + u, u & 1
            pv = weighted_values(c, slot)
            scores(c + 2, slot)
            softmax(1 - slot, False)
            accumulate(pv, slot)
        return carry

    lax.fori_loop(0, last // MLA_UNROLL, body, 0)
    accumulate(weighted_values(last, 0), 0)
    outs = [(acc_ref[h, :MLA_V] / acc_ref[h, MLA_V:MLA_V + 1]).T for h in range(MLA_H)]
    o_ref[...] = jnp.concatenate(outs, axis=-1).astype(o_ref.dtype)


def _mla_attn(q, k, v, lc):
    b, h, s, _ = q.shape
    nc = s // MLA_KC
    assert (nc - 1) % MLA_UNROLL == 0 and lc == MLA_KC
    qt = jnp.swapaxes(q, 2, 3)
    vt = jnp.swapaxes(v.reshape(b, h, nc, MLA_KC, MLA_VA), 3, 4)
    kern = functools.partial(_mla_kernel, n_all=nc, n_ctx=lc // MLA_KC, ctx_tiles=lc // MLA_TQ)
    return pl.pallas_call(
        kern,
        grid=(b, s // MLA_TQ),
        in_specs=[pl.BlockSpec((None, h, LANE, MLA_TQ), lambda bb, i: (bb, 0, 0, i)),
                  pl.BlockSpec((None, h, s, LANE), lambda bb, i: (bb, 0, 0, 0)),
                  pl.BlockSpec((None, h, nc, MLA_VA, MLA_KC), lambda bb, i: (bb, 0, 0, 0, 0))],
        out_specs=pl.BlockSpec((None, MLA_TQ, h * MLA_V), lambda bb, i: (bb, i, 0)),
        out_shape=jax.ShapeDtypeStruct((b, s, h * MLA_V), BF16),
        scratch_shapes=[pltpu.VMEM((h, 1, MLA_TQ), F32),
                        pltpu.VMEM((h, MLA_VA, MLA_TQ), F32), pltpu.VMEM((2, h, MLA_KC, MLA_TQ), F32),
                        pltpu.VMEM((2, h, 1, MLA_TQ), F32),
                        pltpu.VMEM((2, h, MLA_KC, MLA_TQ), BF16), pltpu.VMEM((2, h, 1, MLA_TQ), F32)],
        compiler_params=_cp(("parallel", "arbitrary"), VMEM_LIMIT),
        name="mla_attn",
    )(qt, k, vt)


SW_STEP_BLOCKS = 2


def _swa_kernel(sink_ref, q_ref, k_ref, v_ref, o_ref, *, lc, t_len):
    blk = SW_BLOCK
    nw = blk + 2 * SW_WINDOW
    dn = (((1,), (1,)), ((), ()))
    row = lax.broadcasted_iota(jnp.int32, (2 * blk, nw), 0)
    col = lax.broadcasted_iota(jnp.int32, (2 * blk, nw), 1)
    rows1 = lax.broadcasted_iota(jnp.int32, (2 * blk, 1), 0)
    chains = []
    for part in range(SW_STEP_BLOCKS):
        n = pl.program_id(1) * SW_STEP_BLOCKS + part
        start = pl.multiple_of(jnp.clip((n - 1) * blk, 0, lc + t_len - nw), blk)
        qpos = n * blk - lc + (row & (blk - 1))
        kpos = start - lc + col
        valid = (n * blk >= lc) & (kpos >= 0) & (kpos < t_len) & (jnp.abs(qpos - kpos) <= SW_WINDOW)
        for j in range(SW_KV):
            q2 = q_ref[part * blk:(part + 1) * blk, j * LANE:(j + 1) * LANE]
            qs = jnp.concatenate([q2[:, :SW_DH], q2[:, SW_DH:]], axis=0)
            s_loc = lax.dot_general(qs, k_ref[j, pl.ds(start, nw), :], dn, preferred_element_type=F32)
            s_ctx = lax.dot_general(qs, k_ref[j, pl.ds(0, lc), :], dn, preferred_element_type=F32)
            chains.append(dict(j=j, start=start, valid=valid, s_loc=s_loc, s_ctx=s_ctx))
    for ch in chains:
        j = ch["j"]
        s_loc, s_ctx = jnp.where(ch["valid"], ch["s_loc"], NEG), ch["s_ctx"]
        sink = jnp.where(rows1 < blk, sink_ref[2 * j], sink_ref[2 * j + 1])
        m = jnp.maximum(jnp.maximum(jnp.max(s_loc, axis=-1, keepdims=True),
                                    jnp.max(s_ctx, axis=-1, keepdims=True)), sink)
        ch["e_loc"] = jnp.exp(s_loc - m)
        ch["e_ctx"] = jnp.exp(s_ctx - m)
        ch["den"] = (jnp.sum(ch["e_loc"], axis=-1, keepdims=True) + jnp.sum(ch["e_ctx"], axis=-1, keepdims=True)
                     + jnp.exp(sink - m))
    for ch in chains:
        j = ch["j"]
        ch["o"] = (jnp.dot(ch["e_loc"].astype(BF16), v_ref[j, pl.ds(ch["start"], nw), :], preferred_element_type=F32)
                   + jnp.dot(ch["e_ctx"].astype(BF16), v_ref[j, pl.ds(0, lc), :], preferred_element_type=F32))
    for part in range(SW_STEP_BLOCKS):
        outs = []
        for ch in chains[part * SW_KV:(part + 1) * SW_KV]:
            o = ch["o"] / ch["den"]
            outs += [o[:blk], o[blk:]]
        o_ref[part * blk:(part + 1) * blk, :] = jnp.concatenate(outs, axis=-1).astype(o_ref.dtype)


def _swa_attn(q, k, v, sink, lc):
    b, s, _ = q.shape
    t_len = s - lc
    sp = k.shape[2]
    rows = SW_STEP_BLOCKS * SW_BLOCK
    kern = functools.partial(_swa_kernel, lc=lc, t_len=t_len)
    return pl.pallas_call(
        kern,
        grid=(b, s // rows),
        in_specs=[pl.BlockSpec(memory_space=pltpu.SMEM),
                  pl.BlockSpec((None, rows, SW_KV * LANE), lambda bb, n: (bb, n, 0)),
                  pl.BlockSpec((None, SW_KV, sp, SW_DH), lambda bb, n: (bb, 0, 0, 0)),
                  pl.BlockSpec((None, SW_KV, sp, SW_DH), lambda bb, n: (bb, 0, 0, 0))],
        out_specs=pl.BlockSpec((None, rows, SW_KV * LANE), lambda bb, n: (bb, n, 0)),
        out_shape=jax.ShapeDtypeStruct((b, s, SW_KV * LANE), BF16),
        compiler_params=_cp(("parallel", "arbitrary"), VMEM_LIMIT),
        name="swa_attn",
    )(sink, q, k, v)


def _mirror(j, n_ctx, n_all):
    return jnp.where(j < n_ctx, n_ctx - 1 - j, n_all + n_ctx - 1 - j)


ML_AUG = ML_DH + SUB
ML_ROWS = 4


def _mlstm_kernel(xf_ref, qtf_ref, vtf_ref, gf_ref, xb_ref, qtb_ref, vtb_ref, gb_ref, bias_ref, hf_ref, hb_ref,
                  c_ref, m_ref):
    L = ML_CHUNK

    @pl.when(pl.program_id(1) == 0)
    def _():
        c_ref[...] = jnp.zeros_like(c_ref)
        m_ref[...] = jnp.zeros_like(m_ref)

    row = lax.broadcasted_iota(jnp.int32, (L, L), 0)
    col = lax.broadcasted_iota(jnp.int32, (L, L), 1)
    aug_row = lax.broadcasted_iota(jnp.int32, (SUB, L), 0)
    dirs = ((xf_ref, qtf_ref, vtf_ref, gf_ref, hf_ref), (xb_ref, qtb_ref, vtb_ref, gb_ref, hb_ref))
    chains = []
    for r, (direction, (x_ref, qt_ref, vt_ref, g_ref, o_ref)) in itertools.product(range(ML_ROWS), enumerate(dirs)):
        reach = (row <= col) if direction == 0 else (row >= col)
        gt = (g_ref[r] + bias_ref[...]).T
        brow = jnp.dot(jax.nn.log_sigmoid(gt), reach.astype(F32), precision=HI, preferred_element_type=F32)
        ib = gt - pltpu.roll(brow, L - ML_H, 0)
        ib_t = ib.T
        for h in range(ML_H):
            ch = dict(idx=(r * 2 + direction) * ML_H + h, reach=reach, last=L - 1 if direction == 0 else 0)
            gi, gf = 2 * ML_H * direction + h, 2 * ML_H * direction + ML_H + h
            ch.update(b_row=brow[gf:gf + 1, :], ib_row=ib[gi:gi + 1, :], ib_col=ib_t[:, gi:gi + 1])
            ch["k"] = x_ref[r, :, h * ML_DH:(h + 1) * ML_DH] * (ML_DH ** -0.5)
            ch["qt"] = qt_ref[r, h * ML_DH:(h + 1) * ML_DH, :]
            ch["vt"] = vt_ref[r, h * ML_DH:(h + 1) * ML_DH, :]
            ch["c"], ch["m"] = c_ref[ch["idx"]], m_ref[ch["idx"]]
            chains.append(ch)
    for ch in chains:
        ch["kq"] = jnp.dot(ch["k"], ch["qt"], preferred_element_type=F32)
        ch["cq"] = jnp.dot(ch["c"].astype(BF16), ch["qt"], preferred_element_type=F32)
    for ch in chains:
        b_row, m_st = ch["b_row"], ch["m"]
        d = jnp.where(ch["reach"], b_row + ch["ib_col"], NEG)
        m_inter = b_row + m_st
        m_t = jnp.maximum(m_inter, jnp.max(d, axis=0, keepdims=True))
        ch["w_inter"], ch["m_t"] = jnp.exp(m_inter - m_t), m_t
        ch["st"] = ch["kq"] * jnp.exp(d - m_t)
        b_last = b_row[:, ch["last"]:ch["last"] + 1]
        g_row = b_last + ch["ib_row"]
        m_new = jnp.maximum(b_last + m_st, jnp.max(g_row, axis=-1, keepdims=True))
        w_row = jnp.exp(g_row - m_new)
        ch["decay"], ch["m_new"] = jnp.exp(b_last + m_st - m_new), m_new
        ch["upd"] = jnp.concatenate([ch["vt"].astype(F32) * w_row, jnp.where(aug_row == 0, w_row, 0.0)], axis=0)
    for ch in chains:
        ch["num"] = jnp.dot(ch["vt"], ch["st"].astype(BF16), preferred_element_type=F32)
        ch["dc"] = jnp.dot(ch["upd"].astype(BF16), ch["k"], preferred_element_type=F32)
    outs = []
    for ch in chains:
        num = ch["num"] + ch["w_inter"] * ch["cq"][:ML_DH]
        den = jnp.sum(ch["st"], axis=0, keepdims=True) + ch["w_inter"] * ch["cq"][ML_DH:ML_DH + 1]
        outs.append(num / jnp.maximum(jnp.abs(den), jnp.exp(-ch["m_t"])))
        c_ref[ch["idx"]] = ch["decay"] * ch["c"] + ch["dc"]
        m_ref[ch["idx"]] = ch["m_new"]
    for n, (r, o_ref) in enumerate(itertools.product(range(ML_ROWS), (hf_ref, hb_ref))):
        o_ref[r] = jnp.concatenate(outs[n * ML_H:(n + 1) * ML_H], axis=0).T


def _mlstm(za, zg, bias, lc):
    b, s, _ = za.shape
    n_all, n_ctx = s // ML_CHUNK, lc // ML_CHUNK
    zat = jnp.swapaxes(za, 1, 2)
    ident = lambda j: j
    mirr = lambda j: _mirror(j, n_ctx, n_all)

    def specs(cf):
        rows = lambda w, col: pl.BlockSpec((ML_ROWS, ML_CHUNK, w), lambda bb, j: (bb, cf(j), col))
        feats = lambda blk: pl.BlockSpec((ML_ROWS, ML_W, ML_CHUNK), lambda bb, j: (bb, blk, cf(j)))
        return [rows(ML_W, 1), feats(0), feats(2), rows(LANE, 0)]

    out = lambda cf: pl.BlockSpec((ML_ROWS, ML_CHUNK, ML_W), lambda bb, j: (bb, cf(j), 0))
    assert b % ML_ROWS == 0
    return pl.pallas_call(
        _mlstm_kernel,
        grid=(b // ML_ROWS, n_all),
        in_specs=specs(ident) + specs(mirr) + [_whole((1, LANE))],
        out_specs=[out(ident), out(mirr)],
        out_shape=[jax.ShapeDtypeStruct((b, s, ML_W), F32)] * 2,
        scratch_shapes=[pltpu.VMEM((ML_ROWS * 2 * ML_H, ML_AUG, ML_DH), F32),
                        pltpu.VMEM((ML_ROWS * 2 * ML_H, 1, 1), F32)],
        compiler_params=_cp(("parallel", "arbitrary"), VMEM_LIMIT),
        name="mlstm_scan",
    )(za, zat, zat, zg, za, zat, zat, zg, bias)


LRU_T = 256
LRU_HALO = SUB
LRU_TAPS = 4


def _scan_rows(a, b, reverse):
    axis = a.ndim - 2
    n, sh = a.shape[axis], 1
    idx = lax.broadcasted_iota(jnp.int32, a.shape, axis)
    while sh < n:
        if reverse:
            ok, a_s, b_s = idx < n - sh, pltpu.roll(a, n - sh, axis), pltpu.roll(b, n - sh, axis)
        else:
            ok, a_s, b_s = idx >= sh, pltpu.roll(a, sh, axis), pltpu.roll(b, sh, axis)
        b = jnp.where(ok, a * b_s + b, b)
        a = jnp.where(ok, a * a_s, a)
        sh *= 2
    return a, b


def _lru_kernel(uf_ref, pf_ref, nf_ref, ub_ref, pb_ref, nb_ref, cw_ref, cb_ref, wa_ref, ba_ref, wx_ref, bx_ref,
                lam_ref, hf_ref, hb_ref, carry_ref, sa_ref, sb_ref, ext_ref, *, n_ctx, n_all):
    T = LRU_T
    groups = T // SUB
    j = pl.program_id(1)

    @pl.when(j == 0)
    def _():
        carry_ref[...] = jnp.zeros_like(carry_ref)

    cw = cw_ref[...]
    grow = lax.broadcasted_iota(jnp.int32, (groups, LRU_W), 0)
    for direction, (u_ref, p_ref, n_ref, o_ref) in enumerate(((uf_ref, pf_ref, nf_ref, hf_ref),
                                                              (ub_ref, pb_ref, nb_ref, hb_ref))):
        c = j if direction == 0 else _mirror(j, n_ctx, n_all)
        has_prev = ((c != 0) & (c != n_ctx)).astype(F32)
        has_next = ((c != n_ctx - 1) & (c != n_all - 1)).astype(F32)
        ext_ref[direction, 0:LRU_HALO] = p_ref[...] * has_prev
        ext_ref[direction, LRU_HALO:LRU_HALO + T] = u_ref[...]
        ext_ref[direction, LRU_HALO + T:] = n_ref[...] * has_next
        u = cb_ref[...]
        for tap in range(LRU_TAPS):
            u = u + cw[tap:tap + 1] * ext_ref[direction, pl.ds(LRU_HALO - LRU_TAPS // 2 + tap, T), :]
        ub = u.astype(BF16)
        r = jax.nn.sigmoid(jnp.dot(ub, wa_ref[direction], preferred_element_type=F32) + ba_ref[direction])
        i = jax.nn.sigmoid(jnp.dot(ub, wx_ref[direction], preferred_element_type=F32) + bx_ref[direction])
        lam = lam_ref[direction]
        log_a = (-LRU_C) * r * jnp.log1p(jnp.exp(-lam))
        a = jnp.exp(log_a)
        bb = jnp.sqrt(1.0 - a * a) * (i * u)
        rev = direction == 1
        a, bb = _scan_rows(a.reshape(groups, SUB, LRU_W), bb.reshape(groups, SUB, LRU_W), rev)
        a, bb = a.reshape(T, LRU_W), bb.reshape(T, LRU_W)
        for half in range(LRU_W // LANE):
            sa_ref[direction, half] = a[:, half * LANE:(half + 1) * LANE]
            sb_ref[direction, half] = bb[:, half * LANE:(half + 1) * LANE]
        edge = pl.ds(0 if rev else SUB - 1, groups, stride=SUB)
        ga = jnp.concatenate([sa_ref[direction, half, edge, :] for half in range(LRU_W // LANE)], axis=-1)
        gb = jnp.concatenate([sb_ref[direction, half, edge, :] for half in range(LRU_W // LANE)], axis=-1)
        ga, gb = _scan_rows(ga, gb, rev)
        if rev:
            inner, ga, gb = grow < groups - 1, pltpu.roll(ga, groups - 1, 0), pltpu.roll(gb, groups - 1, 0)
        else:
            inner, ga, gb = grow >= 1, pltpu.roll(ga, 1, 0), pltpu.roll(gb, 1, 0)
        carry = carry_ref[direction]
        h_in = jnp.where(inner, gb + ga * carry, carry)
        h_in = jnp.broadcast_to(h_in[:, None, :], (groups, SUB, LRU_W)).reshape(T, LRU_W)
        hcur = bb + a * h_in
        o_ref[...] = hcur
        last = T - 1 if direction == 0 else 0
        carry_ref[direction] = hcur[last:last + 1, :]


def _lru(zd, p, lc):
    b, s, _ = zd.shape
    n_all, n_ctx = s // LRU_T, lc // LRU_T
    per = LRU_T // LRU_HALO
    n_halo = s // LRU_HALO
    ident = lambda j: j
    mirr = lambda j: _mirror(j, n_ctx, n_all)

    def specs(cf):
        return [pl.BlockSpec((None, LRU_T, LRU_W), lambda bb, j: (bb, cf(j), 0)),
                pl.BlockSpec((None, LRU_HALO, LRU_W), lambda bb, j: (bb, jnp.maximum(cf(j) * per - 1, 0), 0)),
                pl.BlockSpec((None, LRU_HALO, LRU_W), lambda bb, j: (bb, jnp.minimum((cf(j) + 1) * per, n_halo - 1), 0))]

    vec = _whole((2, 1, LRU_W))
    mat = _whole((2, LRU_W, LRU_W))
    return pl.pallas_call(
        functools.partial(_lru_kernel, n_ctx=n_ctx, n_all=n_all),
        grid=(b, n_all),
        in_specs=specs(ident) + specs(mirr) + [_whole((4, LRU_W)), _whole((1, LRU_W)), mat, vec, mat, vec, vec],
        out_specs=[pl.BlockSpec((None, LRU_T, LRU_W), lambda bb, j: (bb, j, 0)),
                   pl.BlockSpec((None, LRU_T, LRU_W), lambda bb, j: (bb, mirr(j), 0))],
        out_shape=[jax.ShapeDtypeStruct((b, s, LRU_W), F32)] * 2,
        scratch_shapes=[pltpu.VMEM((2, 1, LRU_W), F32),
                        pltpu.VMEM((2, LRU_W // LANE, LRU_T, LANE), F32),
                        pltpu.VMEM((2, LRU_W // LANE, LRU_T, LANE), F32),
                        pltpu.VMEM((2, LRU_T + 2 * LRU_HALO, LRU_W), F32)],
        compiler_params=_cp(("parallel", "arbitrary"), VMEM_LIMIT),
        name="rglru_scan",
    )(zd, zd, zd, zd, zd, zd, p["conv_w"], p["conv_b"], p["wa"], p["ba"], p["wx"], p["bx"], p["lam"])


def _out_kernel(*refs, n_src, ctx_tiles, off):
    gate_ref, hfa_ref, hba_ref, o_ref, gn_ref, yb_ref, yc_ref, hfd_ref, hbd_ref, gd_ref, w_ref, out_ref = refs[n_src:]
    pair = [(0, ML_DH), (ML_DH, 2 * ML_DH)]
    gm_pair = _group_mean_matrix(LANE, pair)
    ha = hfa_ref[...] + hba_ref[...]
    ha = jnp.concatenate([_group_rms(ha[:, :LANE], pair, gm_pair), _group_rms(ha[:, LANE:], pair, gm_pair)],
                         axis=-1) * gn_ref[...]
    ya = (jax.nn.sigmoid(o_ref[...].astype(F32)) * ha).astype(BF16)
    yd = (jax.nn.gelu(gd_ref[...]) * (hfd_ref[...] + hbd_ref[...])).astype(BF16)
    acc = jnp.dot(ya, w_ref[0:ML_W, :], preferred_element_type=F32)
    acc = acc + jnp.dot(yb_ref[...], w_ref[ML_W:2 * ML_W, :], preferred_element_type=F32)
    acc = acc + jnp.dot(yc_ref[...], w_ref[2 * ML_W:3 * ML_W, :], preferred_element_type=F32)
    acc = acc + jnp.dot(yd, w_ref[3 * ML_W:4 * ML_W, :], preferred_element_type=F32)
    out_ref[...] = _src_rows(refs[:n_src], ctx_tiles, off) + gate_ref[...] * acc


def _out_proj(src, mod, hfa, hba, za, gn, yb, yc, hfd, hbd, zd, w, nb, ctx_tiles, off):
    b, s = src[0].shape[0], sum(a.shape[1] for a in src)
    nt = s // TM - off
    r = lambda w_, col=0: _rows(w_, off, col)
    return pl.pallas_call(
        functools.partial(_out_kernel, n_src=len(src), ctx_tiles=ctx_tiles, off=off),
        grid=(b, nt),
        in_specs=_src_specs(src, ctx_tiles, off) + [
                  _mod_spec(2, nb, ctx_tiles, off), r(ML_W), r(ML_W), r(ML_W, 3), _whole((1, ML_W)),
                  r(ML_W), r(ML_W), r(ML_W), r(ML_W), r(ML_W, 1), _whole((D, D))],
        out_specs=_rows(D),
        out_shape=jax.ShapeDtypeStruct((b, nt * TM, D), F32),
        compiler_params=_cp(("parallel", "parallel"), VMEM_LIMIT),
        name="out_proj",
    )(*src, mod, hfa, hba, za, gn, yb, yc, hfd, hbd, zd, w)


def _ffn_kernel(x_ref, g_ref, sh_ref, sc_ref, gate_ref, w13_ref, w2_ref, out_ref, *, d_ff):
    x = x_ref[...]
    h = _rms(x) * g_ref[...]
    hb = (h * (1.0 + sc_ref[...]) + sh_ref[...]).astype(BF16)
    a = jnp.dot(hb, w13_ref[:, :d_ff], preferred_element_type=F32)
    g = jnp.dot(hb, w13_ref[:, d_ff:], preferred_element_type=F32)
    u = (g * jax.nn.sigmoid(g) * a).astype(BF16)
    out_ref[...] = x + gate_ref[...] * jnp.dot(u, w2_ref[...], preferred_element_type=F32)


def _ffn(xa, gain, mod, w13, w2, nb, ctx_tiles):
    b, s, _ = xa.shape
    d_ff = w2.shape[0]
    return pl.pallas_call(
        functools.partial(_ffn_kernel, d_ff=d_ff),
        grid=(b, s // TM),
        in_specs=[_rows(D), _whole((1, D)), _mod_spec(3, nb, ctx_tiles, 0), _mod_spec(4, nb, ctx_tiles, 0),
                  _mod_spec(5, nb, ctx_tiles, 0), _whole((D, 2 * d_ff)), _whole((d_ff, D))],
        out_specs=_rows(D),
        out_shape=jax.ShapeDtypeStruct((b, s, D), F32),
        compiler_params=_cp(("parallel", "parallel"), VMEM_LIMIT),
        name="ffn",
    )(xa, gain.reshape(1, D), mod, mod, mod, w13, w2)


META_E, META_RANK, META_GATE = 0, 2, 4


def _router_kernel(x_ref, g_ref, sh_ref, sc_ref, wr_ref, br_ref, h_ref, meta_ref, cnt_ref):
    @pl.when((pl.program_id(0) == 0) & (pl.program_id(1) == 0))
    def _():
        cnt_ref[...] = jnp.zeros_like(cnt_ref)

    h = _rms(x_ref[...]) * g_ref[...]
    h = h * (1.0 + sc_ref[...]) + sh_ref[...]
    h_ref[...] = h
    logits = jnp.dot(h, wr_ref[...], precision=HI, preferred_element_type=F32) + br_ref[...]
    lane = _lane(logits.shape)
    logits = jnp.where(lane < N_EXP, logits, NEG)
    m1 = jnp.max(logits, axis=-1, keepdims=True)
    i1 = jnp.min(jnp.where(logits == m1, lane, LANE), axis=-1, keepdims=True)
    rest = jnp.where(lane == i1, NEG, logits)
    m2 = jnp.max(rest, axis=-1, keepdims=True)
    i2 = jnp.min(jnp.where(rest == m2, lane, LANE), axis=-1, keepdims=True)
    e2 = jnp.exp(m2 - m1)
    inv = 1.0 / (1.0 + e2)
    hit1, hit2 = lane == i1, lane == i2
    assign = (hit1 | hit2).astype(F32)
    rr = lax.broadcasted_iota(jnp.int32, (TM, TM), 0)
    cc = lax.broadcasted_iota(jnp.int32, (TM, TM), 1)
    before = jnp.dot((cc < rr).astype(BF16), assign.astype(BF16), preferred_element_type=F32) + cnt_ref[...]
    r1 = jnp.sum(jnp.where(hit1, before, 0.0), axis=-1, keepdims=True)
    r2 = jnp.sum(jnp.where(hit2, before, 0.0), axis=-1, keepdims=True)
    cnt_ref[...] = cnt_ref[...] + jnp.sum(assign, axis=0, keepdims=True)
    fields = (i1.astype(F32), i2.astype(F32), r1, r2, inv, e2 * inv)
    meta = jnp.zeros(logits.shape, F32)
    for k, val in enumerate(fields):
        meta = jnp.where(lane == k, val, meta)
    meta_ref[...] = meta


def _router(xl, gain, mod, wr, br):
    b, t, _ = xl.shape
    spec = lambda k: pl.BlockSpec((None, 1, D), lambda bb, j: (bb, 0, k))
    return pl.pallas_call(
        _router_kernel,
        grid=(b, t // TM),
        in_specs=[_rows(D), _whole((1, D)), spec(3), spec(4), _whole((D, LANE)), _whole((1, LANE))],
        out_specs=[_rows(D), _rows(LANE), _whole((1, LANE))],
        out_shape=[jax.ShapeDtypeStruct((b, t, D), F32), jax.ShapeDtypeStruct((b, t, LANE), F32),
                   jax.ShapeDtypeStruct((1, LANE), F32)],
        compiler_params=_cp(("arbitrary", "arbitrary"), VMEM_LIMIT),
        name="router",
    )(xl, gain.reshape(1, D), mod, mod, wr, br)


TG = 256
TOP_K = 2


ROW_DMA_UNROLL = 8


def _issue_row_dmas(route_ref, make_copy):
    def issue(r, c):
        for k in range(TOP_K):
            make_copy(r, k, route_ref[0, k * TM + r]).start()
        return c

    lax.fori_loop(0, TM, issue, 0, unroll=ROW_DMA_UNROLL)


def _dispatch_kernel(route_ref, fill_ref, h_ref, xs_ref, zero_ref, buf_ref, row_sem, tile_sem, fill_sem):
    step, n_steps = pl.program_id(0), pl.num_programs(0)
    slot = step & 1

    def fetch(t, s):
        return pltpu.make_async_copy(h_ref.at[pl.ds(pl.multiple_of(t * TM, TM), TM)], buf_ref.at[s], tile_sem.at[s])

    def drain_rows(s):
        for _ in range(TOP_K):
            pltpu.make_async_copy(buf_ref.at[s], xs_ref.at[pl.ds(0, TM)], row_sem.at[s]).wait()

    @pl.when(step == 0)
    def _():
        fetch(0, 0).start()
        zero_ref[...] = jnp.zeros_like(zero_ref)
        zero_row = lambda pos: pltpu.make_async_copy(zero_ref.at[pl.ds(0, 1)], xs_ref.at[pl.ds(pos, 1)], fill_sem)
        for e in range(N_EXP + 1):
            first, count = fill_ref[e], fill_ref[N_EXP + 1 + e]
            lax.fori_loop(0, count, lambda i, c: (zero_row(first + i).start(), c)[1], 0)
            lax.fori_loop(0, count, lambda i, c: (zero_row(first).wait(), c)[1], 0)

    @pl.when(step > 0)
    def _():
        drain_rows(1 - slot)

    @pl.when(step + 1 < n_steps)
    def _():
        fetch(step + 1, 1 - slot).start()

    fetch(step, slot).wait()
    _issue_row_dmas(route_ref, lambda r, k, pos: pltpu.make_async_copy(buf_ref.at[slot, pl.ds(r, 1)],
                                                                         xs_ref.at[pl.ds(pos, 1)], row_sem.at[slot]))

    @pl.when(step == n_steps - 1)
    def _():
        drain_rows(slot)


def _dispatch(h2, route, fill, n_rows):
    n = h2.shape[0]
    return pl.pallas_call(
        _dispatch_kernel,
        grid=(n // TM,),
        in_specs=[pl.BlockSpec((None, 1, TOP_K * TM), lambda i: (i, 0, 0), memory_space=pltpu.SMEM),
                  pl.BlockSpec(memory_space=pltpu.SMEM),
                  pl.BlockSpec(memory_space=pl.ANY)],
        out_specs=pl.BlockSpec(memory_space=pl.ANY),
        out_shape=jax.ShapeDtypeStruct((n_rows, D), F32),
        scratch_shapes=[pltpu.VMEM((SUB, D), F32), pltpu.VMEM((2, TM, D), F32), pltpu.SemaphoreType.DMA((2,)),
                        pltpu.SemaphoreType.DMA((2,)), pltpu.SemaphoreType.DMA(())],
        compiler_params=_cp(("arbitrary",), VMEM_LIMIT),
        name="moe_dispatch",
    )(route, fill, h2)


def _expert_kernel(te_ref, nu_ref, xs_ref, w13_ref, w2_ref, y_ref, *, d_ff):
    del te_ref
    t = pl.program_id(0)

    @pl.when(t < nu_ref[0])
    def _():
        xb = xs_ref[...].astype(BF16)
        a = jnp.dot(xb, w13_ref[:, :d_ff], preferred_element_type=F32)
        g = jnp.dot(xb, w13_ref[:, d_ff:], preferred_element_type=F32)
        u = (g * jax.nn.sigmoid(g) * a).astype(BF16)
        y_ref[...] = jnp.dot(u, w2_ref[...], preferred_element_type=F32)

    @pl.when(t >= nu_ref[0])
    def _():
        y_ref[...] = jnp.zeros_like(y_ref)


def _experts(xs, tile_expert, n_used, w13, w2):
    n_rows = xs.shape[0]
    d_ff = w2.shape[1]
    once = dict(pipeline_mode=pl.Buffered(1))
    grid_spec = pltpu.PrefetchScalarGridSpec(
        num_scalar_prefetch=2,
        grid=(n_rows // TG,),
        in_specs=[pl.BlockSpec((TG, D), lambda t, te, nu: (jnp.minimum(t, nu[0] - 1), 0)),
                  pl.BlockSpec((None, D, 2 * d_ff), lambda t, te, nu: (te[t], 0, 0), **once),
                  pl.BlockSpec((None, d_ff, D), lambda t, te, nu: (te[t], 0, 0), **once)],
        out_specs=pl.BlockSpec((TG, D), lambda t, te, nu: (t, 0)),
    )
    return pl.pallas_call(
        functools.partial(_expert_kernel, d_ff=d_ff),
        grid_spec=grid_spec,
        out_shape=jax.ShapeDtypeStruct((n_rows, D), F32),
        compiler_params=_cp(("arbitrary",), VMEM_LIMIT),
        name="moe_experts",
    )(tile_expert, n_used, xs, w13, w2)


def _combine_kernel(route_ref, next_ref, x_ref, meta_ref, gate_ref, y_ref, out_ref, buf_ref, sem, *, n_steps):
    step = pl.program_id(0) * pl.num_programs(1) + pl.program_id(1)
    slot = step & 1

    def gather(rows_ref, s):
        _issue_row_dmas(rows_ref, lambda r, k, pos: pltpu.make_async_copy(
            y_ref.at[pl.ds(pos, 1)], buf_ref.at[s, k, pl.ds(r, 1)], sem.at[s]))

    @pl.when(step == 0)
    def _():
        gather(route_ref, 0)

    @pl.when(step + 1 < n_steps)
    def _():
        gather(next_ref, 1 - slot)

    for k in range(TOP_K):
        pltpu.make_async_copy(y_ref.at[pl.ds(0, TM)], buf_ref.at[slot, k], sem.at[slot]).wait()
    meta = meta_ref[...]
    mix = (meta[:, META_GATE:META_GATE + 1] * buf_ref[slot, 0]
           + meta[:, META_GATE + 1:META_GATE + 2] * buf_ref[slot, 1])
    out_ref[...] = x_ref[...] + gate_ref[...] * mix


def _combine(xl, meta, mod, y, route):
    b, t, _ = xl.shape
    tiles = t // TM
    n_steps = b * tiles
    rows_of = lambda ahead: pl.BlockSpec((None, 1, TOP_K * TM),
                                         lambda bb, j: (jnp.minimum(bb * tiles + j + ahead, n_steps - 1), 0, 0),
                                         memory_space=pltpu.SMEM)
    return pl.pallas_call(
        functools.partial(_combine_kernel, n_steps=n_steps),
        grid=(b, tiles),
        in_specs=[rows_of(0), rows_of(1), _rows(D), _rows(LANE), pl.BlockSpec((None, 1, D), lambda bb, j: (bb, 0, 5)),
                  pl.BlockSpec(memory_space=pl.ANY)],
        out_specs=_rows(D),
        out_shape=jax.ShapeDtypeStruct((b, t, D), F32),
        scratch_shapes=[pltpu.VMEM((2, TOP_K, TM, D), F32), pltpu.SemaphoreType.DMA((2,))],
        compiler_params=_cp(("arbitrary", "arbitrary"), VMEM_LIMIT),
        name="moe_combine",
    )(route, route, xl, meta, mod, y)


def _moe(xl, gain, mod, wr, br, w13, w2):
    b, t, _ = xl.shape
    n = b * t
    h, meta, cnt = _router(xl, gain, mod, wr, br)
    counts = cnt[0, :N_EXP].astype(jnp.int32)
    padded = (counts + TG - 1) // TG * TG
    ends = jnp.cumsum(padded)
    start = ends - padded
    n_tiles = TOP_K * n // TG + N_EXP
    tile_lo = jnp.arange(n_tiles, dtype=jnp.int32) * TG
    tile_expert = jnp.minimum(jnp.sum((tile_lo[:, None] >= ends[None, :]).astype(jnp.int32), axis=1), N_EXP - 1)
    n_used = (ends[-1:] // TG).astype(jnp.int32)
    ids = meta.reshape(n // TM, TM, LANE)[:, :, :2 * TOP_K].astype(jnp.int32)
    expert, rank = ids[:, :, META_E:META_E + TOP_K], ids[:, :, META_RANK:META_RANK + TOP_K]
    pos = rank + sum(jnp.where(expert == e, start[e], 0) for e in range(N_EXP))
    route = jnp.swapaxes(pos, 1, 2).reshape(n // TM, 1, TOP_K * TM)
    n_rows = n_tiles * TG
    fill = jnp.concatenate([start + counts, ends[-1:], padded - counts, n_rows - ends[-1:]])
    xs = _dispatch(h.reshape(n, D), route, fill, n_rows)
    y = _experts(xs, tile_expert, n_used, w13, w2)
    return _combine(xl, meta, mod, y, route)


def _pad_cols(w, n):
    return jnp.pad(w, ((0, 0), (0, n - w.shape[1])))


def _layout_w_in(w):
    ml = 4 * ML_W
    a, gates = w[:, :ml], w[:, ml:ml + 4 * ML_H]
    o = ml + 4 * ML_H
    cq, ckv, kr = w[:, o:o + MLA_QR], w[:, o + MLA_QR:o + MLA_QR + MLA_KVR], w[:, o + MLA_QR + MLA_KVR:o + MLA_QR + MLA_KVR + MLA_ROPE]
    o += MLA_QR + MLA_KVR + MLA_ROPE
    sw = w[:, o:o + (SW_H + 2 * SW_KV) * SW_DH]
    o += (SW_H + 2 * SW_KV) * SW_DH
    lru = w[:, o:]
    slots = ((a, ml), (gates, LANE), (cq, 2 * LANE), (ckv, MLA_KVR), (None, MLA_NOPE), (kr, LANE - MLA_NOPE),
             (sw, sw.shape[1]), (lru, lru.shape[1]))
    out = jnp.zeros((w.shape[0], sum(width for _, width in slots)), BF16)
    col = 0
    for piece, width in slots:
        if piece is not None:
            out = out.at[:, col:col + piece.shape[1]].set(piece.astype(BF16))
        col += width
    return out


IN_SEGS = ((0, 4 * ML_W), (4 * ML_W, LANE), (4 * ML_W + LANE, 4 * LANE), (4 * ML_W + 5 * LANE, 4 * LANE),
           (4 * ML_W + 9 * LANE, 4 * LANE))
IN_DTYPES = (BF16, F32, F32, F32, F32)


def _row(v, n=None):
    v = v.reshape(1, -1)
    return v if n is None else _pad_cols(v, n)


def _layer_params(l, ml_gate_b, ml_out_norm, mla_q_norm, mla_w_uq, mla_kv_norm, mla_w_ukv, mla_q_gain, mla_k_gain,
                  sw_q_gain, sw_k_gain, lru_conv_w, lru_conv_b, lru_wa, lru_ba, lru_wx, lru_bx, lru_lam):
    hq = MLA_NOPE + MLA_ROPE
    wq = mla_w_uq[l].reshape(MLA_QR, MLA_H, hq)
    wq = jnp.pad(wq, ((0, 2 * LANE - MLA_QR), (0, 0), (0, LANE - hq))).reshape(2 * LANE, MLA_H * LANE)
    wkv = mla_w_ukv[l].reshape(MLA_KVR, MLA_H, MLA_NOPE + MLA_V)
    wk = jnp.pad(wkv[:, :, :MLA_NOPE], ((0, 0), (0, 0), (0, LANE - MLA_NOPE))).reshape(MLA_KVR, MLA_H * LANE)
    wv = wkv[:, :, MLA_NOPE:].reshape(MLA_KVR, MLA_H * MLA_V)

    def blockdiag(wb):
        bw = LRU_W // LRU_BLOCKS
        eye = jnp.eye(LRU_BLOCKS, dtype=wb.dtype)
        return jnp.einsum('xncd,nm->xncmd', wb, eye).reshape(2, LRU_W, LRU_W)

    return dict(
        gate_b=_row(ml_gate_b[l], LANE), out_norm=_row(ml_out_norm[l]),
        q_norm=_row(mla_q_norm[l], 2 * LANE), w_uq=wq.astype(BF16), kv_norm=_row(mla_kv_norm[l]),
        w_uk=wk.astype(BF16), w_uv=wv.astype(BF16),
        q_gain=_row(mla_q_gain[l], LANE), k_gain=_row(mla_k_gain[l], LANE),
        sw_q_gain=_row(jnp.tile(sw_q_gain[l], 2)), sw_k_gain=_row(jnp.tile(sw_k_gain[l], 2)),
        conv_w=lru_conv_w[l], conv_b=_row(lru_conv_b[l]),
        wa=blockdiag(lru_wa[l]).astype(BF16), ba=lru_ba[l].reshape(2, 1, LRU_W),
        wx=blockdiag(lru_wx[l]).astype(BF16), bx=lru_bx[l].reshape(2, 1, LRU_W),
        lam=lru_lam[l].reshape(2, 1, LRU_W),
    )


def _mix_layer(src, mod, gain, w_in_l, w_out_l, sink, p, tabs, lc, nb, ctx_out):
    ctx_tiles = lc // TM
    za, zg, zb, zc, zd = _proj(src, gain, mod, w_in_l, IN_SEGS, IN_DTYPES, nb, ctx_tiles)
    hfa, hba = _mlstm(za, zg, p["gate_b"], lc)
    qm, km, vm, qs, ks, vs = _prep(zb, zc, tabs[0], tabs[1], p)
    yb = _mla_attn(qm, km, vm, lc)
    yc = _swa_attn(qs, ks, vs, sink, lc)
    hfd, hbd = _lru(zd, p, lc)
    off = 0 if ctx_out else ctx_tiles
    return _out_proj(src, mod, hfa, hba, za, p["out_norm"], yb, yc, hfd, hbd, zd, w_out_l, nb, ctx_tiles, off)


def kernel(x, c, ctx, c_ctx, ada_w, ada_b, norm_mix, norm_ffn, w_in, w_out, ml_gate_b, ml_out_norm, mla_q_norm, mla_w_uq, mla_kv_norm, mla_w_ukv, mla_q_gain, mla_k_gain, sw_q_gain, sw_k_gain, sw_sink, lru_conv_w, lru_conv_b, lru_wa, lru_ba, lru_wx, lru_bx, lru_lam, ffn_w13, ffn_w2, moe_router, moe_router_b, moe_w13, moe_w2):
    nb, t_len, _ = x.shape
    lc = ctx.shape[1]
    depth = ada_w.shape[0]
    assert nb < SUB and lc % TM == 0 and t_len % TM == 0 and t_len % GRID_W == 0
    cc = jnp.zeros((SUB, D), F32).at[:nb].set(c).at[nb].set(c_ctx)
    mods = _ada_mod(cc, ada_w, ada_b)
    tabs = (_rope_tables(t_len, lc, LANE, MLA_ROPE, MLA_NOPE), _rope_tables(t_len, lc, SW_DH, SW_DH, 0))
    src = (ctx, x)
    for l in range(depth):
        ctx_out = l < depth - 1
        mod = mods[l].reshape(SUB, 1, 6 * D)
        p = _layer_params(l, ml_gate_b, ml_out_norm, mla_q_norm, mla_w_uq, mla_kv_norm, mla_w_ukv, mla_q_gain,
                          mla_k_gain, sw_q_gain, sw_k_gain, lru_conv_w, lru_conv_b, lru_wa, lru_ba, lru_wx, lru_bx,
                          lru_lam)
        xa = _mix_layer(src, mod, norm_mix[l], _layout_w_in(w_in[l]), w_out[l].astype(BF16), sw_sink[l], p, tabs,
                        lc, nb, ctx_out)
        if l % 2 == 0:
            xa = _ffn(xa, norm_ffn[l], mod, ffn_w13[l // 2].astype(BF16), ffn_w2[l // 2].astype(BF16), nb,
                      lc // TM if ctx_out else 0)
        else:
            xa = _moe(xa, norm_ffn[l], mod, _pad_cols(moe_router[l // 2], LANE), _row(moe_router_b[l // 2], LANE),
                      moe_w13[l // 2].astype(BF16), moe_w2[l // 2].astype(BF16))
        src = (xa,)
    return xa
```
